```python
import math
import jax, jax.numpy as jnp
from jax import lax
import numpy as np

D_MODEL = 1024
BATCH = 4
SEQ = 8192
DEPTH = 2

N_MIXERS = 2
N_EVEN = (DEPTH + 1) // 2
N_ODD = DEPTH // 2

MLA_HEADS = 16
QK_NOPE = 64
QK_ROPE = 32
V_DIM = 64
Q_LORA = 384
KV_LORA = 256
MLA_IN = Q_LORA + KV_LORA + QK_ROPE
ROPE_THETA = 10000.0
Q_BLOCK = 128
NEG_INF = -1e30

S5_GROUP = 16
S5_GROUPS = D_MODEL // S5_GROUP
S5_STATE = 64
DT_MIN = 0.001
DT_MAX = 0.1

D_FF = 2816
N_EXPERTS = 8
TOP_K = 2
D_FF_EXPERT = 3584

EPS = 1e-6

kernel_name = "hybrid_mla_s5_moe_trunk"


def rmsnorm(x, g):
    xf = x.astype(jnp.float32)
    y = xf * lax.rsqrt(jnp.mean(xf * xf, axis=-1, keepdims=True) + EPS)
    return (y * g.astype(jnp.float32)).astype(x.dtype)


def rope(x, cos, sin):
    x1, x2 = jnp.split(x, 2, axis=-1)
    return jnp.concatenate([x1 * cos - x2 * sin, x2 * cos + x1 * sin], axis=-1).astype(x.dtype)


def mla_mixer(h, positions, w_in, q_norm, w_uq, kv_norm, w_ukv, w_o):
    B, S, _ = h.shape
    f32 = jnp.float32
    proj = h @ w_in
    c_q = rmsnorm(proj[..., :Q_LORA], q_norm)
    c_kv = rmsnorm(proj[..., Q_LORA:Q_LORA + KV_LORA], kv_norm)
    k_pe = proj[..., Q_LORA + KV_LORA:]
    q = (c_q @ w_uq).reshape(B, S, MLA_HEADS, QK_NOPE + QK_ROPE)
    q_nope, q_pe = q[..., :QK_NOPE], q[..., QK_NOPE:]
    inv_freq = ROPE_THETA ** (-jnp.arange(0, QK_ROPE, 2, dtype=f32) / QK_ROPE)
    ang = positions.astype(f32)[..., None] * inv_freq
    cos, sin = jnp.cos(ang), jnp.sin(ang)
    q_pe = rope(q_pe, cos[:, :, None, :], sin[:, :, None, :])
    k_pe = rope(k_pe, cos, sin)
    kv = (c_kv @ w_ukv).reshape(B, S, MLA_HEADS, QK_NOPE + V_DIM)
    k_nope, v = kv[..., :QK_NOPE], kv[..., QK_NOPE:]
    scale = 1.0 / math.sqrt(QK_NOPE + QK_ROPE)
    n_blocks = S // Q_BLOCK

    def to_blocks(t):
        return t.reshape(B, n_blocks, Q_BLOCK, *t.shape[2:]).swapaxes(0, 1)

    k_idx = jnp.arange(S, dtype=jnp.int32)

    def attend_block(args):
        qn, qp, start = args
        s = (jnp.einsum('bqhd,bkhd->bhqk', qn, k_nope, preferred_element_type=f32)
             + jnp.einsum('bqhr,bkr->bhqk', qp, k_pe, preferred_element_type=f32))
        q_idx = start + jnp.arange(Q_BLOCK, dtype=jnp.int32)
        causal = k_idx[None, :] <= q_idx[:, None]
        s = jnp.where(causal, s * scale, NEG_INF)
        p = jax.nn.softmax(s, axis=-1).astype(v.dtype)
        return jnp.einsum('bhqk,bkhd->bqhd', p, v)

    starts = jnp.arange(n_blocks, dtype=jnp.int32) * Q_BLOCK
    o = lax.map(attend_block, (to_blocks(q_nope), to_blocks(q_pe), starts))
    o = o.swapaxes(0, 1).reshape(B, S, MLA_HEADS * V_DIM)
    return o @ w_o


def _complex_affine_combine(left, right):
    a1r, a1i, b1r, b1i = left
    a2r, a2i, b2r, b2i = right
    ar = a2r * a1r - a2i * a1i
    ai = a2r * a1i + a2i * a1r
    br = a2r * b1r - a2i * b1i + b2r
    bi = a2r * b1i + a2i * b1r + b2i
    return (ar, ai, br, bi)


def s5_mixer(h, w_in, lam_re, lam_im, log_dt, b_re, b_im, c_re, c_im, d_skip, w_glu):
    B, S, D = h.shape
    f32 = jnp.float32
    u = (h @ w_in).astype(f32).reshape(B, S, S5_GROUPS, S5_GROUP)
    lr, li = lam_re.astype(f32), lam_im.astype(f32)
    dt = jnp.exp(log_dt.astype(f32))[:, None]
    mag = jnp.exp(lr * dt)
    ar = mag * jnp.cos(li * dt)
    ai = mag * jnp.sin(li * dt)
    den = lr * lr + li * li
    gr = ((ar - 1.0) * lr + ai * li) / den
    gi = (ai * lr - (ar - 1.0) * li) / den
    br, bi = b_re.astype(f32), b_im.astype(f32)
    bbr = gr[..., None] * br - gi[..., None] * bi
    bbi = gr[..., None] * bi + gi[..., None] * br
    bu_r = jnp.einsum('bsgh,gph->sbgp', u, bbr)
    bu_i = jnp.einsum('bsgh,gph->sbgp', u, bbi)
    a_r = jnp.broadcast_to(ar[None, None], (S, 1, S5_GROUPS, S5_STATE))
    a_i = jnp.broadcast_to(ai[None, None], (S, 1, S5_GROUPS, S5_STATE))
    _, _, xr, xi = lax.associative_scan(_complex_affine_combine, (a_r, a_i, bu_r, bu_i), axis=0)
    y = (jnp.einsum('sbgp,ghp->bsgh', xr, c_re.astype(f32))
         - jnp.einsum('sbgp,ghp->bsgh', xi, c_im.astype(f32))
         + d_skip.astype(f32) * u)
    y = jax.nn.gelu(y.reshape(B, S, D)).astype(h.dtype)
    z = y @ w_glu
    return z[..., :D] * jax.nn.sigmoid(z[..., D:])


def swiglu(h, w_gate, w_up, w_down):
    return (jax.nn.silu(h @ w_gate) * (h @ w_up)) @ w_down


def moe_swiglu(h, w_router, w_gate, w_up, w_down):
    B, S, D = h.shape
    t = h.reshape(B * S, D)
    logits = (t @ w_router).astype(jnp.float32)
    top_vals, top_idx = lax.top_k(logits, TOP_K)
    gates = jax.nn.softmax(top_vals, axis=-1)
    combine = jnp.sum(jax.nn.one_hot(top_idx, N_EXPERTS, dtype=jnp.float32) * gates[..., None], axis=1)
    out = jnp.zeros_like(t)
    for e in range(N_EXPERTS):
        he = swiglu(t, w_gate[e], w_up[e], w_down[e])
        out = out + combine[:, e:e + 1].astype(t.dtype) * he
    return out.reshape(B, S, D)


def setup_inputs(seed: int = 0) -> dict:
    key = jax.random.key(seed)
    ks = iter(jax.random.split(key, 40))
    f32 = jnp.float32

    def nrm(shape, scale):
        return jax.random.normal(next(ks), shape, f32) * scale

    D = D_MODEL
    x = jax.random.normal(next(ks), (BATCH, SEQ, D), f32)
    offset = jax.random.randint(next(ks), (BATCH, 1), 0, 1024, dtype=jnp.int32)
    positions = offset + jnp.arange(SEQ, dtype=jnp.int32)[None, :]
    mix_norm = 1.0 + nrm((DEPTH, D), 0.05)
    ffn_norm = 1.0 + nrm((DEPTH, D), 0.05)
    final_norm = 1.0 + nrm((D,), 0.05)
    mla_w_in = nrm((N_EVEN, D, MLA_IN), D ** -0.5)
    mla_q_norm = 1.0 + nrm((N_EVEN, Q_LORA), 0.05)
    mla_w_uq = nrm((N_EVEN, Q_LORA, MLA_HEADS * (QK_NOPE + QK_ROPE)), Q_LORA ** -0.5)
    mla_kv_norm = 1.0 + nrm((N_EVEN, KV_LORA), 0.05)
    mla_w_ukv = nrm((N_EVEN, KV_LORA, MLA_HEADS * (QK_NOPE + V_DIM)), KV_LORA ** -0.5)
    mla_w_o = nrm((N_EVEN, MLA_HEADS * V_DIM, D), (MLA_HEADS * V_DIM) ** -0.5)
    ffn_w_gate = nrm((N_EVEN, D, D_FF), D ** -0.5)
    ffn_w_up = nrm((N_EVEN, D, D_FF), D ** -0.5)
    ffn_w_down = nrm((N_EVEN, D_FF, D), D_FF ** -0.5)
    s5_w_in = nrm((N_ODD, D, D), D ** -0.5)
    s5_lambda_re = -0.5 + nrm((N_ODD, S5_GROUPS, S5_STATE), 0.01)
    s5_lambda_im = math.pi * jnp.arange(S5_STATE, dtype=f32)[None, None, :] + nrm((N_ODD, S5_GROUPS, S5_STATE), 0.01)
    s5_log_dt = jax.random.uniform(next(ks), (N_ODD, S5_GROUPS), f32, math.log(DT_MIN), math.log(DT_MAX))
    s5_b_re = nrm((N_ODD, S5_GROUPS, S5_STATE, S5_GROUP), (2 * S5_GROUP) ** -0.5)
    s5_b_im = nrm((N_ODD, S5_GROUPS, S5_STATE, S5_GROUP), (2 * S5_GROUP) ** -0.5)
    s5_c_re = nrm((N_ODD, S5_GROUPS, S5_GROUP, S5_STATE), (2 * S5_STATE) ** -0.5)
    s5_c_im = nrm((N_ODD, S5_GROUPS, S5_GROUP, S5_STATE), (2 * S5_STATE) ** -0.5)
    s5_d = nrm((N_ODD, S5_GROUPS, S5_GROUP), 1.0)
    s5_w_glu = nrm((N_ODD, D, 2 * D), D ** -0.5)
    moe_w_router = nrm((N_ODD, D, N_EXPERTS), D ** -0.5)
    moe_w_gate = nrm((N_ODD, N_EXPERTS, D, D_FF_EXPERT), D ** -0.5)
    moe_w_up = nrm((N_ODD, N_EXPERTS, D, D_FF_EXPERT), D ** -0.5)
    moe_w_down = nrm((N_ODD, N_EXPERTS, D_FF_EXPERT, D), D_FF_EXPERT ** -0.5)
    return {
        "x": x, "positions": positions,
        "mix_norm": mix_norm, "ffn_norm": ffn_norm, "final_norm": final_norm,
        "mla_w_in": mla_w_in, "mla_q_norm": mla_q_norm, "mla_w_uq": mla_w_uq,
        "mla_kv_norm": mla_kv_norm, "mla_w_ukv": mla_w_ukv, "mla_w_o": mla_w_o,
        "ffn_w_gate": ffn_w_gate, "ffn_w_up": ffn_w_up, "ffn_w_down": ffn_w_down,
        "s5_w_in": s5_w_in, "s5_lambda_re": s5_lambda_re, "s5_lambda_im": s5_lambda_im,
        "s5_log_dt": s5_log_dt, "s5_b_re": s5_b_re, "s5_b_im": s5_b_im,
        "s5_c_re": s5_c_re, "s5_c_im": s5_c_im, "s5_d": s5_d, "s5_w_glu": s5_w_glu,
        "moe_w_router": moe_w_router, "moe_w_gate": moe_w_gate,
        "moe_w_up": moe_w_up, "moe_w_down": moe_w_down,
    }


def reference(x, positions, mix_norm, ffn_norm, final_norm,
              mla_w_in, mla_q_norm, mla_w_uq, mla_kv_norm, mla_w_ukv, mla_w_o,
              ffn_w_gate, ffn_w_up, ffn_w_down,
              s5_w_in, s5_lambda_re, s5_lambda_im, s5_log_dt, s5_b_re, s5_b_im,
              s5_c_re, s5_c_im, s5_d, s5_w_glu,
              moe_w_router, moe_w_gate, moe_w_up, moe_w_down):
    h = x
    for i in range(DEPTH):
        j = i // N_MIXERS
        hn = rmsnorm(h, mix_norm[i])
        if i % N_MIXERS == 0:
            h = h + mla_mixer(hn, positions, mla_w_in[j], mla_q_norm[j], mla_w_uq[j],
                              mla_kv_norm[j], mla_w_ukv[j], mla_w_o[j])
        else:
            h = h + s5_mixer(hn, s5_w_in[j], s5_lambda_re[j], s5_lambda_im[j], s5_log_dt[j],
                             s5_b_re[j], s5_b_im[j], s5_c_re[j], s5_c_im[j], s5_d[j], s5_w_glu[j])
        hn = rmsnorm(h, ffn_norm[i])
        if i % 2 == 0:
            h = h + swiglu(hn, ffn_w_gate[j], ffn_w_up[j], ffn_w_down[j])
        else:
            h = h + moe_swiglu(hn, moe_w_router[j], moe_w_gate[j], moe_w_up[j], moe_w_down[j])
    return rmsnorm(h, final_norm)
```

```python
import functools
import math

import jax
import jax.numpy as jnp
import numpy as np
from jax import lax
from jax.experimental import pallas as pl
from jax.experimental.pallas import tpu as pltpu

F32 = jnp.float32
BF16 = jnp.bfloat16

D_MODEL = 1024
MLA_HEADS = 16
QK_NOPE = 64
QK_ROPE = 32
V_DIM = 64
Q_LORA = 384
KV_LORA = 256
ROPE_THETA = 10000.0
HEAD_PAD = 128
S5_GROUP = 16
S5_GROUPS = D_MODEL // S5_GROUP
S5_STATE = 64
S5_CHUNK = 16
S5_SUPER = 8
N_SUPER = S5_GROUPS // S5_SUPER
N_EXPERTS = 8
EPS = 1e-6
NEG_INF = -1e30
LANES = 128
VMEM_LIMIT = 56 * 1024 * 1024


def _cparams(*sem):
    return pltpu.CompilerParams(dimension_semantics=sem, vmem_limit_bytes=VMEM_LIMIT)


def _rms(x, g):
    return x * lax.rsqrt(jnp.mean(x * x, axis=-1, keepdims=True) + EPS) * g


def _mla_proj_kernel(x_ref, pos_ref, g_ref, win_ref, qn_ref, kvn_ref, wq_ref, wkv_ref, tab_ref,
                     q_ref, k_ref, v_ref):
    hn = _rms(x_ref[...], g_ref[...]).astype(BF16)
    proj = jnp.dot(hn, win_ref[...], preferred_element_type=F32)
    cq = _rms(proj[:, :Q_LORA], qn_ref[...]).astype(BF16)
    ckv = _rms(proj[:, Q_LORA:Q_LORA + KV_LORA], kvn_ref[...]).astype(BF16)
    ka = proj[:, 640:768]
    kb = proj[:, 768:896]
    tab = tab_ref[...]
    ang = pos_ref[...] * tab[0:1, :]
    cos = jnp.cos(ang)
    sin = jnp.sin(ang)
    kp = ka * (cos * tab[1:2, :]) + kb * (sin * tab[2:3, :])
    tq = cos * tab[3:4, :] + sin * tab[4:5, :]
    q = jnp.dot(cq, wq_ref[...], preferred_element_type=F32)
    kv = jnp.dot(ckv, wkv_ref[...], preferred_element_type=F32)
    for h in range(MLA_HEADS):
        sl = slice(h * HEAD_PAD, (h + 1) * HEAD_PAD)
        q_ref[:, sl] = (q[:, sl] * tq).astype(BF16)
        k_ref[:, sl] = (kv[:, sl] + kp).astype(BF16)
    v_ref[...] = kv[:, MLA_HEADS * HEAD_PAD:].astype(BF16)


def _mla_tables():
    inv_freq = ROPE_THETA ** (-np.arange(0, QK_ROPE, 2, dtype=np.float32) / QK_ROPE)
    inv_freq = inv_freq.astype(np.float32)
    half = QK_ROPE // 2
    tab = np.zeros((8, LANES), np.float32)
    scale = math.log2(math.e) / math.sqrt(QK_NOPE + QK_ROPE)
    tab[3, :QK_NOPE] = scale
    kc = [1.0, -1.0, 1.0, 1.0]
    ks = [-1.0, 1.0, 1.0, 1.0]
    for grp in range(4):
        sl = slice(QK_NOPE + grp * half, QK_NOPE + (grp + 1) * half)
        tab[0, sl] = inv_freq
        tab[1, sl] = kc[grp]
        tab[2, sl] = ks[grp]
        if grp % 2 == 0:
            tab[3, sl] = scale
        else:
            tab[4, sl] = scale
    return jnp.asarray(tab)


def _mla_weights(w_in, w_uq, w_ukv):
    half = QK_ROPE // 2
    base = Q_LORA + KV_LORA
    k1 = w_in[:, base:base + half]
    k2 = w_in[:, base + half:base + QK_ROPE]
    z64 = jnp.zeros((D_MODEL, QK_NOPE), w_in.dtype)
    win_ext = jnp.concatenate([w_in[:, :base], z64, k1, k1, k2, k2, z64, k2, k2, k1, k1], axis=1)
    wq = w_uq.reshape(Q_LORA, MLA_HEADS, QK_NOPE + QK_ROPE)
    q1 = wq[:, :, QK_NOPE:QK_NOPE + half]
    q2 = wq[:, :, QK_NOPE + half:]
    wq_ext = jnp.concatenate([wq[:, :, :QK_NOPE], q1, q2, q2, q1], axis=2)
    wq_ext = wq_ext.reshape(Q_LORA, MLA_HEADS * HEAD_PAD)
    wkv = w_ukv.reshape(KV_LORA, MLA_HEADS, QK_NOPE + V_DIM)
    zk = jnp.zeros((KV_LORA, MLA_HEADS, HEAD_PAD - QK_NOPE), w_ukv.dtype)
    zv = jnp.zeros((KV_LORA, MLA_HEADS, HEAD_PAD - V_DIM), w_ukv.dtype)
    wk_ext = jnp.concatenate([wkv[:, :, :QK_NOPE], zk], axis=2).reshape(KV_LORA, -1)
    wv_ext = jnp.concatenate([wkv[:, :, QK_NOPE:], zv], axis=2).reshape(KV_LORA, -1)
    wkv_ext = jnp.concatenate([wk_ext, wv_ext], axis=1)
    return win_ext.astype(BF16), wq_ext.astype(BF16), wkv_ext.astype(BF16)


def _mla_proj(x2d, pos, g, win_ext, qn, kvn, wq_ext, wkv_ext, tm=512):
    n = x2d.shape[0]
    hw = MLA_HEADS * HEAD_PAD
    full = lambda a: pl.BlockSpec(a.shape, lambda i: (0,) * a.ndim)
    tab = _mla_tables()
    return pl.pallas_call(
        _mla_proj_kernel,
        grid=(n // tm,),
        in_specs=[pl.BlockSpec((tm, D_MODEL), lambda i: (i, 0)),
                  pl.BlockSpec((tm, 1), lambda i: (i, 0)),
                  full(g), full(win_ext), full(qn), full(kvn), full(wq_ext), full(wkv_ext), full(tab)],
        out_specs=[pl.BlockSpec((tm, hw), lambda i: (i, 0))] * 3,
        out_shape=[jax.ShapeDtypeStruct((n, hw), BF16)] * 3,
        compiler_params=_cparams("parallel"),
        name="mla_proj",
    )(x2d, pos, g, win_ext, qn, kvn, wq_ext, wkv_ext, tab)


def _attn_kernel(q_ref, k_ref, v_ref, o_ref, *, tq, tk):
    i = pl.program_id(2)
    q = q_ref[...]
    n_full = (i * tq) // tk
    n_all = ((i + 1) * tq + tk - 1) // tk

    def step(j, carry, masked):
        m, l, acc = carry
        off = pl.multiple_of(j * tk, tk)
        k = k_ref[pl.ds(off, tk), :]
        v = v_ref[pl.ds(off, tk), :]
        s = lax.dot_general(q, k, (((1,), (1,)), ((), ())), preferred_element_type=F32)
        if masked:
            row = i * tq + lax.broadcasted_iota(jnp.int32, (tq, tk), 0)
            col = j * tk + lax.broadcasted_iota(jnp.int32, (tq, tk), 1)
            s = jnp.where(col <= row, s, NEG_INF)
        m_new = jnp.maximum(m, jnp.max(s, axis=-1, keepdims=True))
        alpha = jnp.exp2(m - m_new)
        p = jnp.exp2(s - m_new)
        l = alpha * l + jnp.sum(p, axis=-1, keepdims=True)
        acc = alpha * acc + jnp.dot(p.astype(BF16), v, preferred_element_type=F32)
        return m_new, l, acc

    init = (jnp.full((tq, 1), NEG_INF, F32), jnp.zeros((tq, 1), F32), jnp.zeros((tq, HEAD_PAD), F32))
    carry = lax.fori_loop(0, n_full, functools.partial(step, masked=False), init)
    m, l, acc = lax.fori_loop(n_full, n_all, functools.partial(step, masked=True), carry)
    o_ref[...] = (acc / l).astype(o_ref.dtype)


def _attention(q, k, v, batch, seq, tq=512, tk=512):
    n, hw = q.shape
    nq = seq // tq
    return pl.pallas_call(
        functools.partial(_attn_kernel, tq=tq, tk=tk),
        grid=(batch, MLA_HEADS, nq),
        in_specs=[pl.BlockSpec((tq, HEAD_PAD), lambda b, h, i: (b * nq + i, h)),
                  pl.BlockSpec((seq, HEAD_PAD), lambda b, h, i: (b, h)),
                  pl.BlockSpec((seq, HEAD_PAD), lambda b, h, i: (b, h))],
        out_specs=pl.BlockSpec((tq, HEAD_PAD), lambda b, h, i: (b * nq + i, h)),
        out_shape=jax.ShapeDtypeStruct((n, hw), BF16),
        compiler_params=_cparams("parallel", "parallel", "arbitrary"),
        name="mla_attention",
    )(q, k, v)


def _mm_res_norm_kernel(a_ref, w_ref, res_ref, g_ref, h_ref, hn_ref):
    h = res_ref[...] + jnp.dot(a_ref[...], w_ref[...], preferred_element_type=F32)
    h_ref[...] = h
    hn_ref[...] = _rms(h, g_ref[...]).astype(hn_ref.dtype)


def _mm_res_norm(a, w, res, g, tm=512):
    n, k = a.shape
    d = w.shape[1]
    return pl.pallas_call(
        _mm_res_norm_kernel,
        grid=(n // tm,),
        in_specs=[pl.BlockSpec((tm, k), lambda i: (i, 0)),
                  pl.BlockSpec((k, d), lambda i: (0, 0)),
                  pl.BlockSpec((tm, d), lambda i: (i, 0)),
                  pl.BlockSpec((1, d), lambda i: (0, 0))],
        out_specs=[pl.BlockSpec((tm, d), lambda i: (i, 0))] * 2,
        out_shape=[jax.ShapeDtypeStruct((n, d), F32), jax.ShapeDtypeStruct((n, d), BF16)],
        compiler_params=_cparams("parallel"),
        name="proj_residual_norm",
    )(a, w, res, g)


def _ffn_kernel(hn_ref, res_ref, comb_ref, wg_ref, wu_ref, wd_ref, g2_ref, out_ref, outn_ref, acc_ref,
                *, weighted):
    e = pl.program_id(1)
    kk = pl.program_id(2)

    @pl.when(jnp.logical_and(e == 0, kk == 0))
    def _():
        acc_ref[...] = res_ref[...]

    hn = hn_ref[...]
    gate = jnp.dot(hn, wg_ref[...], preferred_element_type=F32)
    up = jnp.dot(hn, wu_ref[...], preferred_element_type=F32)
    act = jax.nn.silu(gate) * up
    if weighted:
        lane = lax.broadcasted_iota(jnp.int32, comb_ref.shape, 1)
        cw = jnp.sum(jnp.where(lane == e, comb_ref[...], 0.0), axis=-1, keepdims=True)
        act = act * cw
    acc_ref[...] += jnp.dot(act.astype(BF16), wd_ref[...], preferred_element_type=F32)

    @pl.when(jnp.logical_and(e == pl.num_programs(1) - 1, kk == pl.num_programs(2) - 1))
    def _():
        out = acc_ref[...]
        out_ref[...] = out
        outn_ref[...] = _rms(out, g2_ref[...]).astype(outn_ref.dtype)


def _ffn(hn, res, comb, wg, wu, wd, g2, norm_dtype, tm, tf):
    n, d = hn.shape
    ne, _, f = wg.shape
    weighted = comb is not None
    if comb is None:
        comb = jnp.ones((8, LANES), F32)
        comb_spec = pl.BlockSpec((8, LANES), lambda i, e, k: (0, 0))
    else:
        comb_spec = pl.BlockSpec((tm, LANES), lambda i, e, k: (i, 0))
    return pl.pallas_call(
        functools.partial(_ffn_kernel, weighted=weighted),
        grid=(n // tm, ne, f // tf),
        in_specs=[pl.BlockSpec((tm, d), lambda i, e, k: (i, 0)),
                  pl.BlockSpec((tm, d), lambda i, e, k: (i, 0)),
                  comb_spec,
                  pl.BlockSpec((None, d, tf), lambda i, e, k: (e, 0, k)),
                  pl.BlockSpec((None, d, tf), lambda i, e, k: (e, 0, k)),
                  pl.BlockSpec((None, tf, d), lambda i, e, k: (e, k, 0)),
                  pl.BlockSpec((1, d), lambda i, e, k: (0, 0))],
        out_specs=[pl.BlockSpec((tm, d), lambda i, e, k: (i, 0))] * 2,
        out_shape=[jax.ShapeDtypeStruct((n, d), F32), jax.ShapeDtypeStruct((n, d), norm_dtype)],
        scratch_shapes=[pltpu.VMEM((tm, d), F32)],
        compiler_params=_cparams("parallel", "arbitrary", "arbitrary"),
        name="swiglu_moe" if weighted else "swiglu_dense",
    )(hn, res, comb, wg, wu, wd, g2)


def _mm_split_kernel(a_ref, w_ref, o_ref):
    r = jnp.dot(a_ref[...], w_ref[...], preferred_element_type=F32)
    for g in range(N_SUPER):
        o_ref[g] = r[:, g * LANES:(g + 1) * LANES].astype(o_ref.dtype)


def _mm_split(a, w, tm=512):
    n, k = a.shape
    return pl.pallas_call(
        _mm_split_kernel,
        grid=(n // tm,),
        in_specs=[pl.BlockSpec((tm, k), lambda i: (i, 0)),
                  pl.BlockSpec(w.shape, lambda i: (0, 0))],
        out_specs=pl.BlockSpec((N_SUPER, tm, LANES), lambda i: (0, i, 0)),
        out_shape=jax.ShapeDtypeStruct((N_SUPER, n, LANES), BF16),
        compiler_params=_cparams("parallel"),
        name="s5_in_proj",
    )(a, w)


def _s5_param_kernel(lr_ref, li_ref, ldt_ref, btr_ref, bti_ref, cr_ref, ci_ref, d_ref,
                     kt_ref, er_ref, ei_ref, wr_ref, wi_ref, ac_ref):
    L = S5_CHUNK
    lr = lr_ref[...]
    li = li_ref[...]
    dt = jnp.exp(ldt_ref[...])
    mag = jnp.exp(lr * dt)
    ar = mag * jnp.cos(li * dt)
    ai = mag * jnp.sin(li * dt)
    den = lr * lr + li * li
    gr = ((ar - 1.0) * lr + ai * li) / den
    gi = (ai * lr - (ar - 1.0) * li) / den
    btr = btr_ref[...]
    bti = bti_ref[...]
    bbr = gr * btr - gi * bti
    bbi = gr * bti + gi * btr
    kf = lax.broadcasted_iota(jnp.int32, (L, 1), 0).astype(F32)
    pmag = jnp.exp(lr * dt * kf)
    pr = pmag * jnp.cos(li * dt * kf)
    pi = pmag * jnp.sin(li * dt * kf)
    cr = cr_ref[...]
    ci = ci_ref[...]
    pr3 = pr[:, None, :]
    pi3 = pi[:, None, :]
    wr = cr[None] * pr3 - ci[None] * pi3
    wi = cr[None] * pi3 + ci[None] * pr3
    dn = (((1,), (1,)), ((), ()))
    hp = lax.Precision.HIGHEST
    kt = (lax.dot_general(wr.reshape(L * S5_GROUP, S5_STATE), bbr, dn, precision=hp,
                          preferred_element_type=F32)
          - lax.dot_general(wi.reshape(L * S5_GROUP, S5_STATE), bbi, dn, precision=hp,
                            preferred_element_type=F32))
    row = lax.broadcasted_iota(jnp.int32, kt.shape, 0)
    col = lax.broadcasted_iota(jnp.int32, kt.shape, 1)
    kt_ref[...] = kt + jnp.where(row == col, d_ref[...], 0.0)
    er_ref[...] = pr3 * bbr[None] - pi3 * bbi[None]
    ei_ref[...] = pr3 * bbi[None] + pi3 * bbr[None]
    wr_ref[...] = wr * ar - wi * ai
    wi_ref[...] = wr * ai + wi * ar
    lastr = pr[L - 1:L, :]
    lasti = pi[L - 1:L, :]
    ac_ref[0:1, :] = lastr * ar - lasti * ai
    ac_ref[1:2, :] = lastr * ai + lasti * ar


def _s5_operators(lam_re, lam_im, log_dt, b_re, b_im, c_re, c_im, d_skip):
    G, P, H, L = S5_GROUPS, S5_STATE, S5_GROUP, S5_CHUNK
    per_g = lambda *shape: pl.BlockSpec((None,) + shape, lambda g: (g,) + (0,) * len(shape))
    kt, er, ei, wr, wi, ac = pl.pallas_call(
        _s5_param_kernel,
        grid=(G,),
        in_specs=[per_g(1, P), per_g(1, P), per_g(1, 1), per_g(H, P), per_g(H, P),
                  per_g(H, P), per_g(H, P), per_g(1, H)],
        out_specs=[per_g(L * H, H), per_g(L, H, P), per_g(L, H, P), per_g(L, H, P), per_g(L, H, P),
                   per_g(2, P)],
        out_shape=[jax.ShapeDtypeStruct((G, L * H, H), F32)]
        + [jax.ShapeDtypeStruct((G, L, H, P), F32)] * 4
        + [jax.ShapeDtypeStruct((G, 2, P), F32)],
        compiler_params=_cparams("parallel"),
        name="s5_operators",
    )(lam_re.reshape(G, 1, P), lam_im.reshape(G, 1, P), log_dt.reshape(G, 1, 1),
      b_re.transpose(0, 2, 1), b_im.transpose(0, 2, 1), c_re, c_im, d_skip.reshape(G, 1, H))

    ns, sg = N_SUPER, S5_SUPER
    eye = jnp.eye(sg, dtype=F32)
    ktg = kt.reshape(ns, sg, L, H, H)
    lag = np.arange(L)[None, :] - np.arange(L)[:, None]
    toe = ktg[:, :, np.clip(lag, 0, None)] * jnp.asarray(lag >= 0, F32)[:, :, None, None]
    tmat = jnp.einsum('SgabOI,gk->SagIbkO', toe, eye).reshape(ns, L * LANES, L * LANES)
    ecat = jnp.stack([er, ei], axis=3)[:, ::-1]
    ecat = ecat.reshape(ns, sg, L, H, 2, P)
    bmat = jnp.einsum('SgLhrp,gk->SLghkrp', ecat, eye).reshape(ns, L * LANES, 2 * sg * P)
    wcat = jnp.stack([wr, -wi], axis=3).reshape(ns, sg, L, H, 2, P)
    cmat = jnp.einsum('SgLhrp,gk->SgrpLkh', wcat, eye).reshape(ns, 2 * sg * P, L * LANES)
    acr = ac[:, 0, :].reshape(ns, sg, P)
    aci = ac[:, 1, :].reshape(ns, sg, P)
    a_same = jnp.concatenate([acr, acr], axis=-1)
    a_swap = jnp.concatenate([-aci, aci], axis=-1)
    return tmat.astype(BF16), bmat.astype(BF16), cmat.astype(BF16), a_same, a_swap


def _s5_inc_kernel(u_ref, b_ref, o_ref):
    o_ref[...] = jnp.dot(u_ref[...], b_ref[...], preferred_element_type=F32)


def _s5_scan_kernel(inc_ref, asame_ref, aswap_ref, x_ref, state_ref, *, cb):
    @pl.when(pl.program_id(0) == 0)
    def _():
        state_ref[...] = jnp.zeros_like(state_ref)

    nseq = state_ref.shape[0]
    a_same = asame_ref[...].reshape(nseq * S5_SUPER, LANES)
    a_swap = aswap_ref[...].reshape(nseq * S5_SUPER, LANES)

    def body(c, x):
        x_ref[:, c] = x.reshape(nseq, S5_SUPER, LANES)
        inc = inc_ref[:, c].reshape(nseq * S5_SUPER, LANES)
        return x * a_same + pltpu.roll(x, LANES // 2, 1) * a_swap + inc

    x0 = state_ref[...].reshape(nseq * S5_SUPER, LANES)
    state_ref[...] = lax.fori_loop(0, cb, body, x0).reshape(nseq, S5_SUPER, LANES)


def _s5_out_kernel(u_ref, x_ref, t_ref, c_ref, y_ref):
    y = jnp.dot(u_ref[...], t_ref[...], preferred_element_type=F32)
    y += jnp.dot(x_ref[...].astype(BF16), c_ref[...], preferred_element_type=F32)
    y_ref[...] = jax.nn.gelu(y).astype(y_ref.dtype)


def _s5_core(u, tmat, bmat, cmat, a_same, a_swap, batch, seq, tr=512):
    ns, n, _ = u.shape
    L = S5_CHUNK
    rows = n // L
    n_chunks = seq // L
    width = L * LANES
    sw = 2 * S5_SUPER * S5_STATE
    uc = u.reshape(ns, rows, width)
    tr = min(tr, rows)
    inc = pl.pallas_call(
        _s5_inc_kernel,
        grid=(ns, rows // tr),
        in_specs=[pl.BlockSpec((None, tr, width), lambda s, r: (s, r, 0)),
                  pl.BlockSpec((None, width, sw), lambda s, r: (s, 0, 0))],
        out_specs=pl.BlockSpec((None, tr, sw), lambda s, r: (s, r, 0)),
        out_shape=jax.ShapeDtypeStruct((ns, rows, sw), F32),
        compiler_params=_cparams("parallel", "parallel"),
        name="s5_chunk_state",
    )(uc, bmat)
    nseq = ns * batch
    cb = min(32, n_chunks)
    tile = (S5_SUPER, LANES)
    bcast = lambda a: jnp.broadcast_to(a[:, None], (ns, batch) + tile).reshape((nseq,) + tile)
    seq_spec = pl.BlockSpec((nseq, cb) + tile, lambda c: (0, c, 0, 0))
    tab_spec = pl.BlockSpec((nseq,) + tile, lambda c: (0, 0, 0))
    xprev = pl.pallas_call(
        functools.partial(_s5_scan_kernel, cb=cb),
        grid=(n_chunks // cb,),
        in_specs=[seq_spec, tab_spec, tab_spec],
        out_specs=seq_spec,
        out_shape=jax.ShapeDtypeStruct((nseq, n_chunks) + tile, F32),
        scratch_shapes=[pltpu.VMEM((nseq,) + tile, F32)],
        compiler_params=_cparams("arbitrary"),
        name="s5_chunk_scan",
    )(inc.reshape((nseq, n_chunks) + tile), bcast(a_same), bcast(a_swap))
    xprev = xprev.reshape(ns, rows, sw)
    y = pl.pallas_call(
        _s5_out_kernel,
        grid=(ns, rows // tr),
        in_specs=[pl.BlockSpec((None, tr, width), lambda s, r: (s, r, 0)),
                  pl.BlockSpec((None, tr, sw), lambda s, r: (s, r, 0)),
                  pl.BlockSpec((None, width, width), lambda s, r: (s, 0, 0)),
                  pl.BlockSpec((None, sw, width), lambda s, r: (s, 0, 0))],
        out_specs=pl.BlockSpec((None, tr, width), lambda s, r: (s, r, 0)),
        out_shape=jax.ShapeDtypeStruct((ns, rows, width), BF16),
        compiler_params=_cparams("parallel", "parallel"),
        name="s5_chunk_out",
    )(uc, xprev, tmat, cmat)
    return y.reshape(ns, n, LANES)


def _glu_router_kernel(y_ref, w_ref, res_ref, g_ref, wr_ref, h_ref, hn_ref, comb_ref):
    d = res_ref.shape[1]
    y = jnp.concatenate([y_ref[g] for g in range(N_SUPER)], axis=-1)
    z = jnp.dot(y, w_ref[...], preferred_element_type=F32)
    h = res_ref[...] + z[:, :d] * jax.nn.sigmoid(z[:, d:])
    h_ref[...] = h
    hn = _rms(h, g_ref[...])
    hn_ref[...] = hn.astype(hn_ref.dtype)
    logits = jnp.dot(hn, wr_ref[...], preferred_element_type=F32, precision=lax.Precision.HIGHEST)
    lane = lax.broadcasted_iota(jnp.int32, logits.shape, 1)
    logits = jnp.where(lane < N_EXPERTS, logits, -jnp.inf)
    m1 = jnp.max(logits, axis=-1, keepdims=True)
    i1 = jnp.min(jnp.where(logits == m1, lane, LANES), axis=-1, keepdims=True)
    rest = jnp.where(lane == i1, -jnp.inf, logits)
    m2 = jnp.max(rest, axis=-1, keepdims=True)
    i2 = jnp.min(jnp.where(rest == m2, lane, LANES), axis=-1, keepdims=True)
    e2 = jnp.exp(m2 - m1)
    g1 = 1.0 / (1.0 + e2)
    comb_ref[...] = jnp.where(lane == i1, g1, 0.0) + jnp.where(lane == i2, e2 * g1, 0.0)


def _glu_router(y, w_glu, res, g, w_router_pad, tm=512):
    ns, n, _ = y.shape
    d = res.shape[1]
    return pl.pallas_call(
        _glu_router_kernel,
        grid=(n // tm,),
        in_specs=[pl.BlockSpec((ns, tm, LANES), lambda i: (0, i, 0)),
                  pl.BlockSpec(w_glu.shape, lambda i: (0, 0)),
                  pl.BlockSpec((tm, d), lambda i: (i, 0)),
                  pl.BlockSpec((1, d), lambda i: (0, 0)),
                  pl.BlockSpec(w_router_pad.shape, lambda i: (0, 0))],
        out_specs=[pl.BlockSpec((tm, d), lambda i: (i, 0)),
                   pl.BlockSpec((tm, d), lambda i: (i, 0)),
                   pl.BlockSpec((tm, LANES), lambda i: (i, 0))],
        out_shape=[jax.ShapeDtypeStruct((n, d), F32), jax.ShapeDtypeStruct((n, d), BF16),
                   jax.ShapeDtypeStruct((n, LANES), F32)],
        compiler_params=_cparams("parallel"),
        name="s5_glu_router",
    )(y, w_glu, res, g, w_router_pad)


def _mla_layer(h, positions, mix_g, w_in, q_norm, w_uq, kv_norm, w_ukv, w_o, ffn_g):
    batch, seq = positions.shape
    win_ext, wq_ext, wkv_ext = _mla_weights(w_in, w_uq, w_ukv)
    pos = positions.astype(F32).reshape(batch * seq, 1)
    q, k, v = _mla_proj(h, pos, mix_g.reshape(1, -1), win_ext, q_norm.reshape(1, -1),
                        kv_norm.reshape(1, -1), wq_ext, wkv_ext)
    o = _attention(q, k, v, batch, seq)
    wo = w_o.reshape(MLA_HEADS, V_DIM, D_MODEL)
    wo_ext = jnp.concatenate([wo, jnp.zeros((MLA_HEADS, HEAD_PAD - V_DIM, D_MODEL), wo.dtype)], axis=1)
    wo_ext = wo_ext.reshape(MLA_HEADS * HEAD_PAD, D_MODEL).astype(BF16)
    return _mm_res_norm(o, wo_ext, h, ffn_g.reshape(1, -1))


def _s5_layer(h, hn, batch, seq, w_in, lam_re, lam_im, log_dt, b_re, b_im, c_re, c_im, d_skip, w_glu,
              ffn_g, w_router):
    u = _mm_split(hn, w_in.astype(BF16))
    tmat, bmat, cmat, a_same, a_swap = _s5_operators(lam_re, lam_im, log_dt, b_re, b_im, c_re, c_im, d_skip)
    y = _s5_core(u, tmat, bmat, cmat, a_same, a_swap, batch, seq)
    wr_pad = jnp.concatenate([w_router, jnp.zeros((D_MODEL, LANES - N_EXPERTS), w_router.dtype)], axis=1)
    return _glu_router(y, w_glu.astype(BF16), h, ffn_g.reshape(1, -1), wr_pad)


def kernel(x, positions, mix_norm, ffn_norm, final_norm, mla_w_in, mla_q_norm, mla_w_uq, mla_kv_norm, mla_w_ukv, mla_w_o, ffn_w_gate, ffn_w_up, ffn_w_down, s5_w_in, s5_lambda_re, s5_lambda_im, s5_log_dt, s5_b_re, s5_b_im, s5_c_re, s5_c_im, s5_d, s5_w_glu, moe_w_router, moe_w_gate, moe_w_up, moe_w_down):
    batch, seq, d = x.shape
    h0 = x.reshape(batch * seq, d)
    h1, hn1 = _mla_layer(h0, positions, mix_norm[0], mla_w_in[0], mla_q_norm[0], mla_w_uq[0],
                         mla_kv_norm[0], mla_w_ukv[0], mla_w_o[0], ffn_norm[0])
    h2, hn2 = _ffn(hn1, h1, None, ffn_w_gate.astype(BF16), ffn_w_up.astype(BF16),
                   ffn_w_down.astype(BF16), mix_norm[1].reshape(1, -1), BF16, tm=512, tf=1408)
    h3, hn3, comb = _s5_layer(h2, hn2, batch, seq, s5_w_in[0], s5_lambda_re[0], s5_lambda_im[0],
                              s5_log_dt[0], s5_b_re[0], s5_b_im[0], s5_c_re[0], s5_c_im[0], s5_d[0],
                              s5_w_glu[0], ffn_norm[1], moe_w_router[0])
    _, out = _ffn(hn3, h3, comb, moe_w_gate[0].astype(BF16), moe_w_up[0].astype(BF16),
                  moe_w_down[0].astype(BF16), final_norm.reshape(1, -1), F32, tm=1024, tf=896)
    return out.reshape(batch, seq, d)
```

```python
import functools
import math

import jax
import jax.numpy as jnp
import numpy as np
from jax import lax
from jax.experimental import pallas as pl
from jax.experimental.pallas import tpu as pltpu

F32 = jnp.float32
BF16 = jnp.bfloat16

D_MODEL = 1024
MLA_HEADS = 16
QK_NOPE = 64
QK_ROPE = 32
V_DIM = 64
Q_LORA = 384
KV_LORA = 256
ROPE_THETA = 10000.0
HEAD_PAD = 128
S5_GROUP = 16
S5_GROUPS = D_MODEL // S5_GROUP
S5_STATE = 64
S5_CHUNK = 16
S5_SUPER = 8
N_SUPER = S5_GROUPS // S5_SUPER
N_EXPERTS = 8
EPS = 1e-6
NEG_INF = -1e30
LANES = 128
VMEM_LIMIT = 56 * 1024 * 1024


def _cparams(*sem):
    return pltpu.CompilerParams(dimension_semantics=sem, vmem_limit_bytes=VMEM_LIMIT)


def _rms(x, g):
    return x * lax.rsqrt(jnp.mean(x * x, axis=-1, keepdims=True) + EPS) * g


def _mla_proj_kernel(x_ref, pos_ref, g_ref, win_ref, qn_ref, kvn_ref, wq_ref, wkv_ref, tab_ref,
                     q_ref, k_ref, v_ref):
    hn = _rms(x_ref[...], g_ref[...]).astype(BF16)
    proj = jnp.dot(hn, win_ref[...], preferred_element_type=F32)
    cq = _rms(proj[:, :Q_LORA], qn_ref[...]).astype(BF16)
    ckv = _rms(proj[:, Q_LORA:Q_LORA + KV_LORA], kvn_ref[...]).astype(BF16)
    ka = proj[:, 640:768]
    kb = proj[:, 768:896]
    tab = tab_ref[...]
    ang = pos_ref[...] * tab[0:1, :]
    cos = jnp.cos(ang)
    sin = jnp.sin(ang)
    kp = ka * (cos * tab[1:2, :]) + kb * (sin * tab[2:3, :])
    tq = cos * tab[3:4, :] + sin * tab[4:5, :]
    q = jnp.dot(cq, wq_ref[...], preferred_element_type=F32)
    kv = jnp.dot(ckv, wkv_ref[...], preferred_element_type=F32)
    for h in range(MLA_HEADS):
        sl = slice(h * HEAD_PAD, (h + 1) * HEAD_PAD)
        q_ref[:, sl] = (q[:, sl] * tq).astype(BF16)
        k_ref[:, sl] = (kv[:, sl] + kp).astype(BF16)
    v_ref[...] = kv[:, MLA_HEADS * HEAD_PAD:].astype(BF16)


def _mla_tables():
    inv_freq = ROPE_THETA ** (-np.arange(0, QK_ROPE, 2, dtype=np.float32) / QK_ROPE)
    inv_freq = inv_freq.astype(np.float32)
    half = QK_ROPE // 2
    tab = np.zeros((8, LANES), np.float32)
    scale = math.log2(math.e) / math.sqrt(QK_NOPE + QK_ROPE)
    tab[3, :QK_NOPE] = scale
    kc = [1.0, -1.0, 1.0, 1.0]
    ks = [-1.0, 1.0, 1.0, 1.0]
    for grp in range(4):
        sl = slice(QK_NOPE + grp * half, QK_NOPE + (grp + 1) * half)
        tab[0, sl] = inv_freq
        tab[1, sl] = kc[grp]
        tab[2, sl] = ks[grp]
        if grp % 2 == 0:
            tab[3, sl] = scale
        else:
            tab[4, sl] = scale
    return jnp.asarray(tab)


def _mla_weights(w_in, w_uq, w_ukv):
    half = QK_ROPE // 2
    base = Q_LORA + KV_LORA
    k1 = w_in[:, base:base + half]
    k2 = w_in[:, base + half:base + QK_ROPE]
    z64 = jnp.zeros((D_MODEL, QK_NOPE), w_in.dtype)
    win_ext = jnp.concatenate([w_in[:, :base], z64, k1, k1, k2, k2, z64, k2, k2, k1, k1], axis=1)
    wq = w_uq.reshape(Q_LORA, MLA_HEADS, QK_NOPE + QK_ROPE)
    q1 = wq[:, :, QK_NOPE:QK_NOPE + half]
    q2 = wq[:, :, QK_NOPE + half:]
    wq_ext = jnp.concatenate([wq[:, :, :QK_NOPE], q1, q2, q2, q1], axis=2)
    wq_ext = wq_ext.reshape(Q_LORA, MLA_HEADS * HEAD_PAD)
    wkv = w_ukv.reshape(KV_LORA, MLA_HEADS, QK_NOPE + V_DIM)
    zk = jnp.zeros((KV_LORA, MLA_HEADS, HEAD_PAD - QK_NOPE), w_ukv.dtype)
    zv = jnp.zeros((KV_LORA, MLA_HEADS, HEAD_PAD - V_DIM), w_ukv.dtype)
    wk_ext = jnp.concatenate([wkv[:, :, :QK_NOPE], zk], axis=2).reshape(KV_LORA, -1)
    wv_ext = jnp.concatenate([wkv[:, :, QK_NOPE:], zv], axis=2).reshape(KV_LORA, -1)
    wkv_ext = jnp.concatenate([wk_ext, wv_ext], axis=1)
    return win_ext.astype(BF16), wq_ext.astype(BF16), wkv_ext.astype(BF16)


def _mla_proj(x2d, pos, g, win_ext, qn, kvn, wq_ext, wkv_ext, tm=512):
    n = x2d.shape[0]
    hw = MLA_HEADS * HEAD_PAD
    full = lambda a: pl.BlockSpec(a.shape, lambda i: (0,) * a.ndim)
    tab = _mla_tables()
    return pl.pallas_call(
        _mla_proj_kernel,
        grid=(n // tm,),
        in_specs=[pl.BlockSpec((tm, D_MODEL), lambda i: (i, 0)),
                  pl.BlockSpec((tm, 1), lambda i: (i, 0)),
                  full(g), full(win_ext), full(qn), full(kvn), full(wq_ext), full(wkv_ext), full(tab)],
        out_specs=[pl.BlockSpec((tm, hw), lambda i: (i, 0))] * 3,
        out_shape=[jax.ShapeDtypeStruct((n, hw), BF16)] * 3,
        compiler_params=_cparams("parallel"),
        name="mla_proj",
    )(x2d, pos, g, win_ext, qn, kvn, wq_ext, wkv_ext, tab)


def _attn_kernel(q_ref, k_ref, v_ref, o_ref, *, tq, tk):
    i = pl.program_id(2)
    q = q_ref[...]
    n_full = (i * tq) // tk
    n_all = ((i + 1) * tq + tk - 1) // tk

    def step(j, carry, masked):
        m, l, acc = carry
        off = pl.multiple_of(j * tk, tk)
        k = k_ref[pl.ds(off, tk), :]
        v = v_ref[pl.ds(off, tk), :]
        s = lax.dot_general(q, k, (((1,), (1,)), ((), ())), preferred_element_type=F32)
        if masked:
            row = i * tq + lax.broadcasted_iota(jnp.int32, (tq, tk), 0)
            col = j * tk + lax.broadcasted_iota(jnp.int32, (tq, tk), 1)
            s = jnp.where(col <= row, s, NEG_INF)
        m_new = jnp.maximum(m, jnp.max(s, axis=-1, keepdims=True))
        alpha = jnp.exp2(m - m_new)
        p = jnp.exp2(s - m_new)
        l = alpha * l + jnp.sum(p, axis=-1, keepdims=True)
        acc = alpha * acc + jnp.dot(p.astype(BF16), v, preferred_element_type=F32)
        return m_new, l, acc

    init = (jnp.full((tq, 1), NEG_INF, F32), jnp.zeros((tq, 1), F32), jnp.zeros((tq, HEAD_PAD), F32))
    carry = lax.fori_loop(0, n_full, functools.partial(step, masked=False), init)
    m, l, acc = lax.fori_loop(n_full, n_all, functools.partial(step, masked=True), carry)
    o_ref[...] = (acc / l).astype(o_ref.dtype)


def _attention(q, k, v, batch, seq, tq=512, tk=512):
    n, hw = q.shape
    nq = seq // tq
    return pl.pallas_call(
        functools.partial(_attn_kernel, tq=tq, tk=tk),
        grid=(batch, MLA_HEADS, nq),
        in_specs=[pl.BlockSpec((tq, HEAD_PAD), lambda b, h, i: (b * nq + i, h)),
                  pl.BlockSpec((seq, HEAD_PAD), lambda b, h, i: (b, h)),
                  pl.BlockSpec((seq, HEAD_PAD), lambda b, h, i: (b, h))],
        out_specs=pl.BlockSpec((tq, HEAD_PAD), lambda b, h, i: (b * nq + i, h)),
        out_shape=jax.ShapeDtypeStruct((n, hw), BF16),
        compiler_params=_cparams("parallel", "parallel", "arbitrary"),
        name="mla_attention",
    )(q, k, v)


def _mm_res_norm_kernel(a_ref, w_ref, res_ref, g_ref, h_ref, hn_ref):
    h = res_ref[...] + jnp.dot(a_ref[...], w_ref[...], preferred_element_type=F32)
    h_ref[...] = h
    hn_ref[...] = _rms(h, g_ref[...]).astype(hn_ref.dtype)


def _mm_res_norm(a, w, res, g, tm=512):
    n, k = a.shape
    d = w.shape[1]
    return pl.pallas_call(
        _mm_res_norm_kernel,
        grid=(n // tm,),
        in_specs=[pl.BlockSpec((tm, k), lambda i: (i, 0)),
                  pl.BlockSpec((k, d), lambda i: (0, 0)),
                  pl.BlockSpec((tm, d), lambda i: (i, 0)),
                  pl.BlockSpec((1, d), lambda i: (0, 0))],
        out_specs=[pl.BlockSpec((tm, d), lambda i: (i, 0))] * 2,
        out_shape=[jax.ShapeDtypeStruct((n, d), F32), jax.ShapeDtypeStruct((n, d), BF16)],
        compiler_params=_cparams("parallel"),
        name="proj_residual_norm",
    )(a, w, res, g)


def _ffn_kernel(hn_ref, res_ref, wg_ref, wu_ref, wd_ref, g2_ref, out_ref, outn_ref):
    kk = pl.program_id(1)

    @pl.when(kk == 0)
    def _():
        out_ref[...] = res_ref[...]

    hn = hn_ref[...]
    gate = jnp.dot(hn, wg_ref[...], preferred_element_type=F32)
    up = jnp.dot(hn, wu_ref[...], preferred_element_type=F32)
    act = (jax.nn.silu(gate) * up).astype(BF16)
    out_ref[...] += jnp.dot(act, wd_ref[...], preferred_element_type=F32)

    @pl.when(kk == pl.num_programs(1) - 1)
    def _():
        outn_ref[...] = _rms(out_ref[...], g2_ref[...]).astype(outn_ref.dtype)


def _ffn(hn, res, wg, wu, wd, g2, tm=512, tf=1408):
    n, d = hn.shape
    f = wg.shape[1]
    return pl.pallas_call(
        _ffn_kernel,
        grid=(n // tm, f // tf),
        in_specs=[pl.BlockSpec((tm, d), lambda i, k: (i, 0)),
                  pl.BlockSpec((tm, d), lambda i, k: (i, 0)),
                  pl.BlockSpec((d, tf), lambda i, k: (0, k)),
                  pl.BlockSpec((d, tf), lambda i, k: (0, k)),
                  pl.BlockSpec((tf, d), lambda i, k: (k, 0)),
                  pl.BlockSpec((1, d), lambda i, k: (0, 0))],
        out_specs=[pl.BlockSpec((tm, d), lambda i, k: (i, 0))] * 2,
        out_shape=[jax.ShapeDtypeStruct((n, d), F32), jax.ShapeDtypeStruct((n, d), BF16)],
        compiler_params=_cparams("parallel", "arbitrary"),
        name="swiglu_dense",
    )(hn, res, wg, wu, wd, g2)


def _mm_split_kernel(a_ref, w_ref, o_ref):
    r = jnp.dot(a_ref[...], w_ref[...], preferred_element_type=F32)
    for g in range(N_SUPER):
        o_ref[g] = r[:, g * LANES:(g + 1) * LANES].astype(o_ref.dtype)


def _mm_split(a, w, tm=512):
    n, k = a.shape
    return pl.pallas_call(
        _mm_split_kernel,
        grid=(n // tm,),
        in_specs=[pl.BlockSpec((tm, k), lambda i: (i, 0)),
                  pl.BlockSpec(w.shape, lambda i: (0, 0))],
        out_specs=pl.BlockSpec((N_SUPER, tm, LANES), lambda i: (0, i, 0)),
        out_shape=jax.ShapeDtypeStruct((N_SUPER, n, LANES), BF16),
        compiler_params=_cparams("parallel"),
        name="s5_in_proj",
    )(a, w)


def _s5_param_kernel(lr_ref, li_ref, ldt_ref, btr_ref, bti_ref, cr_ref, ci_ref, d_ref,
                     kt_ref, er_ref, ei_ref, wr_ref, wi_ref, ac_ref):
    L = S5_CHUNK
    lr = lr_ref[...]
    li = li_ref[...]
    dt = jnp.exp(ldt_ref[...])
    mag = jnp.exp(lr * dt)
    ar = mag * jnp.cos(li * dt)
    ai = mag * jnp.sin(li * dt)
    den = lr * lr + li * li
    gr = ((ar - 1.0) * lr + ai * li) / den
    gi = (ai * lr - (ar - 1.0) * li) / den
    btr = btr_ref[...]
    bti = bti_ref[...]
    bbr = gr * btr - gi * bti
    bbi = gr * bti + gi * btr
    kf = lax.broadcasted_iota(jnp.int32, (L, 1), 0).astype(F32)
    pmag = jnp.exp(lr * dt * kf)
    pr = pmag * jnp.cos(li * dt * kf)
    pi = pmag * jnp.sin(li * dt * kf)
    cr = cr_ref[...]
    ci = ci_ref[...]
    pr3 = pr[:, None, :]
    pi3 = pi[:, None, :]
    wr = cr[None] * pr3 - ci[None] * pi3
    wi = cr[None] * pi3 + ci[None] * pr3
    dn = (((1,), (1,)), ((), ()))
    hp = lax.Precision.HIGHEST
    kt = (lax.dot_general(wr.reshape(L * S5_GROUP, S5_STATE), bbr, dn, precision=hp,
                          preferred_element_type=F32)
          - lax.dot_general(wi.reshape(L * S5_GROUP, S5_STATE), bbi, dn, precision=hp,
                            preferred_element_type=F32))
    row = lax.broadcasted_iota(jnp.int32, kt.shape, 0)
    col = lax.broadcasted_iota(jnp.int32, kt.shape, 1)
    kt_ref[...] = kt + jnp.where(row == col, d_ref[...], 0.0)
    er_ref[...] = pr3 * bbr[None] - pi3 * bbi[None]
    ei_ref[...] = pr3 * bbi[None] + pi3 * bbr[None]
    wr_ref[...] = wr * ar - wi * ai
    wi_ref[...] = wr * ai + wi * ar
    lastr = pr[L - 1:L, :]
    lasti = pi[L - 1:L, :]
    ac_ref[0:1, :] = lastr * ar - lasti * ai
    ac_ref[1:2, :] = lastr * ai + lasti * ar


def _s5_operators(lam_re, lam_im, log_dt, b_re, b_im, c_re, c_im, d_skip):
    G, P, H, L = S5_GROUPS, S5_STATE, S5_GROUP, S5_CHUNK
    per_g = lambda *shape: pl.BlockSpec((None,) + shape, lambda g: (g,) + (0,) * len(shape))
    kt, er, ei, wr, wi, ac = pl.pallas_call(
        _s5_param_kernel,
        grid=(G,),
        in_specs=[per_g(1, P), per_g(1, P), per_g(1, 1), per_g(H, P), per_g(H, P),
                  per_g(H, P), per_g(H, P), per_g(1, H)],
        out_specs=[per_g(L * H, H), per_g(L, H, P), per_g(L, H, P), per_g(L, H, P), per_g(L, H, P),
                   per_g(2, P)],
        out_shape=[jax.ShapeDtypeStruct((G, L * H, H), F32)]
        + [jax.ShapeDtypeStruct((G, L, H, P), F32)] * 4
        + [jax.ShapeDtypeStruct((G, 2, P), F32)],
        compiler_params=_cparams("parallel"),
        name="s5_operators",
    )(lam_re.reshape(G, 1, P), lam_im.reshape(G, 1, P), log_dt.reshape(G, 1, 1),
      b_re.transpose(0, 2, 1), b_im.transpose(0, 2, 1), c_re, c_im, d_skip.reshape(G, 1, H))

    ns, sg = N_SUPER, S5_SUPER
    eye = jnp.eye(sg, dtype=F32)
    ktg = kt.reshape(ns, sg, L, H, H)
    lag = np.arange(L)[None, :] - np.arange(L)[:, None]
    toe = ktg[:, :, np.clip(lag, 0, None)] * jnp.asarray(lag >= 0, F32)[:, :, None, None]
    tmat = jnp.einsum('SgabOI,gk->SagIbkO', toe, eye).reshape(ns, L * LANES, L * LANES)
    ecat = jnp.stack([er, ei], axis=3)[:, ::-1]
    ecat = ecat.reshape(ns, sg, L, H, 2, P)
    bmat = jnp.einsum('SgLhrp,gk->SLghkrp', ecat, eye).reshape(ns, L * LANES, 2 * sg * P)
    wcat = jnp.stack([wr, -wi], axis=3).reshape(ns, sg, L, H, 2, P)
    cmat = jnp.einsum('SgLhrp,gk->SgrpLkh', wcat, eye).reshape(ns, 2 * sg * P, L * LANES)
    acr = ac[:, 0, :].reshape(ns, sg, P)
    aci = ac[:, 1, :].reshape(ns, sg, P)
    a_same = jnp.concatenate([acr, acr], axis=-1)
    a_swap = jnp.concatenate([-aci, aci], axis=-1)
    return tmat.astype(BF16), bmat.astype(BF16), cmat.astype(BF16), a_same, a_swap


def _s5_inc_kernel(u_ref, b_ref, o_ref):
    o_ref[...] = jnp.dot(u_ref[...], b_ref[...], preferred_element_type=F32)


def _s5_scan_kernel(inc_ref, asame_ref, aswap_ref, x_ref, state_ref, *, cb):
    @pl.when(pl.program_id(0) == 0)
    def _():
        state_ref[...] = jnp.zeros_like(state_ref)

    nseq = state_ref.shape[0]
    a_same = asame_ref[...].reshape(nseq * S5_SUPER, LANES)
    a_swap = aswap_ref[...].reshape(nseq * S5_SUPER, LANES)

    def body(c, x):
        x_ref[:, c] = x.reshape(nseq, S5_SUPER, LANES)
        inc = inc_ref[:, c].reshape(nseq * S5_SUPER, LANES)
        return x * a_same + pltpu.roll(x, LANES // 2, 1) * a_swap + inc

    x0 = state_ref[...].reshape(nseq * S5_SUPER, LANES)
    state_ref[...] = lax.fori_loop(0, cb, body, x0).reshape(nseq, S5_SUPER, LANES)


def _s5_out_kernel(u_ref, x_ref, t_ref, c_ref, y_ref):
    y = jnp.dot(u_ref[...], t_ref[...], preferred_element_type=F32)
    y += jnp.dot(x_ref[...].astype(BF16), c_ref[...], preferred_element_type=F32)
    y_ref[...] = jax.nn.gelu(y).astype(y_ref.dtype)


def _s5_core(u, tmat, bmat, cmat, a_same, a_swap, batch, seq, tr=512):
    ns, n, _ = u.shape
    L = S5_CHUNK
    rows = n // L
    n_chunks = seq // L
    width = L * LANES
    sw = 2 * S5_SUPER * S5_STATE
    uc = u.reshape(ns, rows, width)
    tr = min(tr, rows)
    inc = pl.pallas_call(
        _s5_inc_kernel,
        grid=(ns, rows // tr),
        in_specs=[pl.BlockSpec((None, tr, width), lambda s, r: (s, r, 0)),
                  pl.BlockSpec((None, width, sw), lambda s, r: (s, 0, 0))],
        out_specs=pl.BlockSpec((None, tr, sw), lambda s, r: (s, r, 0)),
        out_shape=jax.ShapeDtypeStruct((ns, rows, sw), F32),
        compiler_params=_cparams("parallel", "parallel"),
        name="s5_chunk_state",
    )(uc, bmat)
    nseq = ns * batch
    cb = min(32, n_chunks)
    tile = (S5_SUPER, LANES)
    bcast = lambda a: jnp.broadcast_to(a[:, None], (ns, batch) + tile).reshape((nseq,) + tile)
    seq_spec = pl.BlockSpec((nseq, cb) + tile, lambda c: (0, c, 0, 0))
    tab_spec = pl.BlockSpec((nseq,) + tile, lambda c: (0, 0, 0))
    xprev = pl.pallas_call(
        functools.partial(_s5_scan_kernel, cb=cb),
        grid=(n_chunks // cb,),
        in_specs=[seq_spec, tab_spec, tab_spec],
        out_specs=seq_spec,
        out_shape=jax.ShapeDtypeStruct((nseq, n_chunks) + tile, F32),
        scratch_shapes=[pltpu.VMEM((nseq,) + tile, F32)],
        compiler_params=_cparams("arbitrary"),
        name="s5_chunk_scan",
    )(inc.reshape((nseq, n_chunks) + tile), bcast(a_same), bcast(a_swap))
    xprev = xprev.reshape(ns, rows, sw)
    y = pl.pallas_call(
        _s5_out_kernel,
        grid=(ns, rows // tr),
        in_specs=[pl.BlockSpec((None, tr, width), lambda s, r: (s, r, 0)),
                  pl.BlockSpec((None, tr, sw), lambda s, r: (s, r, 0)),
                  pl.BlockSpec((None, width, width), lambda s, r: (s, 0, 0)),
                  pl.BlockSpec((None, sw, width), lambda s, r: (s, 0, 0))],
        out_specs=pl.BlockSpec((None, tr, width), lambda s, r: (s, r, 0)),
        out_shape=jax.ShapeDtypeStruct((ns, rows, width), BF16),
        compiler_params=_cparams("parallel", "parallel"),
        name="s5_chunk_out",
    )(uc, xprev, tmat, cmat)
    return y.reshape(ns, n, LANES)


def _glu_router_kernel(y_ref, w_ref, res_ref, g_ref, wr_ref, h_ref, hn_ref, route_ref):
    d = res_ref.shape[1]
    y = jnp.concatenate([y_ref[g] for g in range(N_SUPER)], axis=-1)
    z = jnp.dot(y, w_ref[...], preferred_element_type=F32)
    h = res_ref[...] + z[:, :d] * jax.nn.sigmoid(z[:, d:])
    h_ref[...] = h
    hn = _rms(h, g_ref[...])
    hn_ref[...] = hn.astype(hn_ref.dtype)
    logits = jnp.dot(hn, wr_ref[...], preferred_element_type=F32, precision=lax.Precision.HIGHEST)
    lane = lax.broadcasted_iota(jnp.int32, logits.shape, 1)
    logits = jnp.where(lane < N_EXPERTS, logits, -jnp.inf)
    m1 = jnp.max(logits, axis=-1, keepdims=True)
    i1 = jnp.min(jnp.where(logits == m1, lane, LANES), axis=-1, keepdims=True)
    rest = jnp.where(lane == i1, -jnp.inf, logits)
    m2 = jnp.max(rest, axis=-1, keepdims=True)
    i2 = jnp.min(jnp.where(rest == m2, lane, LANES), axis=-1, keepdims=True)
    e2 = jnp.exp(m2 - m1)
    g1 = 1.0 / (1.0 + e2)
    route_ref[...] = (jnp.where(lane == 0, i1.astype(F32), 0.0) + jnp.where(lane == 1, i2.astype(F32), 0.0)
                      + jnp.where(lane == 2, g1, 0.0) + jnp.where(lane == 3, e2 * g1, 0.0))


def _glu_router(y, w_glu, res, g, w_router_pad, tm=512):
    ns, n, _ = y.shape
    d = res.shape[1]
    return pl.pallas_call(
        _glu_router_kernel,
        grid=(n // tm,),
        in_specs=[pl.BlockSpec((ns, tm, LANES), lambda i: (0, i, 0)),
                  pl.BlockSpec(w_glu.shape, lambda i: (0, 0)),
                  pl.BlockSpec((tm, d), lambda i: (i, 0)),
                  pl.BlockSpec((1, d), lambda i: (0, 0)),
                  pl.BlockSpec(w_router_pad.shape, lambda i: (0, 0))],
        out_specs=[pl.BlockSpec((tm, d), lambda i: (i, 0)),
                   pl.BlockSpec((tm, d), lambda i: (i, 0)),
                   pl.BlockSpec((tm, LANES), lambda i: (i, 0))],
        out_shape=[jax.ShapeDtypeStruct((n, d), F32), jax.ShapeDtypeStruct((n, d), F32),
                   jax.ShapeDtypeStruct((n, LANES), F32)],
        compiler_params=_cparams("parallel"),
        name="s5_glu_router",
    )(y, w_glu, res, g, w_router_pad)


MOE_TM = 1024
MOE_TF = 896
DISPATCH_BLOCK = 256
COMBINE_TM = 512


def _moe_plan(route, n_tok):
    ne, tm = N_EXPERTS, MOE_TM
    n_tiles = (2 * n_tok) // tm + ne
    eid = route[:, :2].astype(jnp.int32).reshape(-1)
    onehot = (eid[:, None] == jnp.arange(ne, dtype=jnp.int32)[None, :]).astype(jnp.int32)
    csum = jnp.cumsum(onehot, axis=0)
    rank = jnp.sum((csum - onehot) * onehot, axis=1)
    cnt = csum[-1]
    padded = ((cnt + tm - 1) // tm) * tm
    ends = jnp.cumsum(padded)
    off = ends - padded
    pos = (jnp.sum(off[None, :] * onehot, axis=1) + rank).astype(jnp.int32)
    tile_end = ends // tm
    n_valid = tile_end[-1:].astype(jnp.int32)
    tiles = jnp.arange(n_tiles, dtype=jnp.int32)
    tile_expert = jnp.minimum(jnp.sum(tiles[:, None] >= tile_end[None, :], axis=1), ne - 1).astype(jnp.int32)
    pad_start = (off + cnt).astype(jnp.int32)
    pad_len = (padded - cnt).astype(jnp.int32)
    return pos, tile_expert, n_valid, pad_start, pad_len, n_tiles


def _dispatch_kernel(pos_ref, pstart_ref, plen_ref, hn_ref, xs_ref, zero_ref, sem, *, n_tok):
    nb = n_tok // DISPATCH_BLOCK
    b = pl.program_id(0)

    def row_copy(t, k):
        return pltpu.make_async_copy(hn_ref.at[pl.ds(t, 1)], xs_ref.at[pl.ds(pos_ref[2 * t + k], 1)], sem)

    @pl.when(b < nb)
    def _():
        def body(r, _):
            t = b * DISPATCH_BLOCK + r
            row_copy(t, 0).start()
            row_copy(t, 1).start()
            return 0
        lax.fori_loop(0, DISPATCH_BLOCK, body, 0, unroll=8)

    @pl.when(b > 0)
    def _():
        def body(r, _):
            t = (b - 1) * DISPATCH_BLOCK + r
            row_copy(t, 0).wait()
            row_copy(t, 1).wait()
            return 0
        lax.fori_loop(0, DISPATCH_BLOCK, body, 0, unroll=8)

    @pl.when(b == nb)
    def _():
        zero_ref[...] = jnp.zeros_like(zero_ref)

        def pad_copy(e, r):
            return pltpu.make_async_copy(zero_ref.at[pl.ds(0, 1)], xs_ref.at[pl.ds(pstart_ref[e] + r, 1)], sem)

        for e in range(N_EXPERTS):
            def pbody(r, _, e=e):
                pad_copy(e, r).start()
                return 0
            lax.fori_loop(0, plen_ref[e], pbody, 0)
        for e in range(N_EXPERTS):
            def wbody(r, _, e=e):
                pad_copy(e, r).wait()
                return 0
            lax.fori_loop(0, plen_ref[e], wbody, 0)


def _dispatch(hn, pos, pad_start, pad_len, n_rows):
    n_tok, d = hn.shape
    return pl.pallas_call(
        functools.partial(_dispatch_kernel, n_tok=n_tok),
        grid_spec=pltpu.PrefetchScalarGridSpec(
            num_scalar_prefetch=3,
            grid=(n_tok // DISPATCH_BLOCK + 1,),
            in_specs=[pl.BlockSpec(memory_space=pl.ANY)],
            out_specs=pl.BlockSpec(memory_space=pl.ANY),
            scratch_shapes=[pltpu.VMEM((8, d), F32), pltpu.SemaphoreType.DMA]),
        out_shape=jax.ShapeDtypeStruct((n_rows, d), F32),
        compiler_params=_cparams("arbitrary"),
        name="moe_dispatch",
    )(pos, pad_start, pad_len, hn)


def _moe_kernel(te_ref, nv_ref, x_ref, wg_ref, wu_ref, wd_ref, y_ref, xb_ref):
    i = pl.program_id(0)
    k = pl.program_id(1)

    @pl.when(i < nv_ref[0])
    def _():
        @pl.when(k == 0)
        def _():
            xb_ref[...] = x_ref[...].astype(BF16)

        hn = xb_ref[...]
        gate = jnp.dot(hn, wg_ref[...], preferred_element_type=F32)
        up = jnp.dot(hn, wu_ref[...], preferred_element_type=F32)
        act = (jax.nn.silu(gate) * up).astype(BF16)
        part = jnp.dot(act, wd_ref[...], preferred_element_type=F32)

        @pl.when(k == 0)
        def _():
            y_ref[...] = part

        @pl.when(k > 0)
        def _():
            y_ref[...] += part


def _moe_experts(xs, tile_expert, n_valid, wg, wu, wd, n_tiles):
    tm, tf = MOE_TM, MOE_TF
    d = xs.shape[1]
    f = wg.shape[2]
    nk = f // tf

    def row_map(i, k, te, nv):
        return (jnp.minimum(i, nv[0] - 1), 0)

    def kk(i, k, nv):
        return jnp.where(i < nv[0], k, nk - 1)

    def w_in_map(i, k, te, nv):
        return (te[jnp.minimum(i, nv[0] - 1)], 0, kk(i, k, nv))

    def w_out_map(i, k, te, nv):
        return (te[jnp.minimum(i, nv[0] - 1)], kk(i, k, nv), 0)

    return pl.pallas_call(
        _moe_kernel,
        grid_spec=pltpu.PrefetchScalarGridSpec(
            num_scalar_prefetch=2,
            grid=(n_tiles, nk),
            in_specs=[pl.BlockSpec((tm, d), row_map),
                      pl.BlockSpec((None, d, tf), w_in_map),
                      pl.BlockSpec((None, d, tf), w_in_map),
                      pl.BlockSpec((None, tf, d), w_out_map)],
            out_specs=pl.BlockSpec((tm, d), row_map),
            scratch_shapes=[pltpu.VMEM((tm, d), BF16)]),
        out_shape=jax.ShapeDtypeStruct(xs.shape, F32),
        compiler_params=_cparams("arbitrary", "arbitrary"),
        name="moe_experts",
    )(tile_expert, n_valid, xs, wg, wu, wd)


def _combine_kernel(pos_ref, y_ref, h_ref, route_ref, g_ref, out_ref, buf_ref, sem, *, tm, n_steps):
    i = pl.program_id(0)
    slot = i % 2

    def row_copy(step, sl, r, k):
        t = step * tm + r
        return pltpu.make_async_copy(y_ref.at[pl.ds(pos_ref[2 * t + k], 1)],
                                     buf_ref.at[sl, k, pl.ds(r, 1)], sem.at[sl])

    def issue(step, sl):
        def body(r, _):
            row_copy(step, sl, r, 0).start()
            row_copy(step, sl, r, 1).start()
            return 0
        lax.fori_loop(0, tm, body, 0, unroll=8)

    @pl.when(i == 0)
    def _():
        issue(0, 0)

    @pl.when(i + 1 < n_steps)
    def _():
        issue(i + 1, 1 - slot)

    def wbody(r, _):
        row_copy(i, slot, r, 0).wait()
        row_copy(i, slot, r, 1).wait()
        return 0
    lax.fori_loop(0, tm, wbody, 0, unroll=8)

    route = route_ref[...]
    out = h_ref[...] + route[:, 2:3] * buf_ref[slot, 0] + route[:, 3:4] * buf_ref[slot, 1]
    out_ref[...] = _rms(out, g_ref[...])


def _combine(y, pos, h, route, g):
    n_tok, d = h.shape
    tm = COMBINE_TM
    n_steps = n_tok // tm
    return pl.pallas_call(
        functools.partial(_combine_kernel, tm=tm, n_steps=n_steps),
        grid_spec=pltpu.PrefetchScalarGridSpec(
            num_scalar_prefetch=1,
            grid=(n_steps,),
            in_specs=[pl.BlockSpec(memory_space=pl.ANY),
                      pl.BlockSpec((tm, d), lambda i, pos: (i, 0)),
                      pl.BlockSpec((tm, LANES), lambda i, pos: (i, 0)),
                      pl.BlockSpec((1, d), lambda i, pos: (0, 0))],
            out_specs=pl.BlockSpec((tm, d), lambda i, pos: (i, 0)),
            scratch_shapes=[pltpu.VMEM((2, 2, tm, d), F32), pltpu.SemaphoreType.DMA((2,))]),
        out_shape=jax.ShapeDtypeStruct((n_tok, d), F32),
        compiler_params=_cparams("arbitrary"),
        name="moe_combine",
    )(pos, y, h, route, g)


def _moe_layer(h, hn, route, wg, wu, wd, g):
    n_tok = h.shape[0]
    pos, tile_expert, n_valid, pad_start, pad_len, n_tiles = _moe_plan(route, n_tok)
    xs = _dispatch(hn, pos, pad_start, pad_len, n_tiles * MOE_TM)
    y = _moe_experts(xs, tile_expert, n_valid, wg, wu, wd, n_tiles)
    return _combine(y, pos, h, route, g)


def _mla_layer(h, positions, mix_g, w_in, q_norm, w_uq, kv_norm, w_ukv, w_o, ffn_g):
    batch, seq = positions.shape
    win_ext, wq_ext, wkv_ext = _mla_weights(w_in, w_uq, w_ukv)
    pos = positions.astype(F32).reshape(batch * seq, 1)
    q, k, v = _mla_proj(h, pos, mix_g.reshape(1, -1), win_ext, q_norm.reshape(1, -1),
                        kv_norm.reshape(1, -1), wq_ext, wkv_ext)
    o = _attention(q, k, v, batch, seq)
    wo = w_o.reshape(MLA_HEADS, V_DIM, D_MODEL)
    wo_ext = jnp.concatenate([wo, jnp.zeros((MLA_HEADS, HEAD_PAD - V_DIM, D_MODEL), wo.dtype)], axis=1)
    wo_ext = wo_ext.reshape(MLA_HEADS * HEAD_PAD, D_MODEL).astype(BF16)
    return _mm_res_norm(o, wo_ext, h, ffn_g.reshape(1, -1))


def _s5_layer(h, hn, batch, seq, w_in, lam_re, lam_im, log_dt, b_re, b_im, c_re, c_im, d_skip, w_glu,
              ffn_g, w_router):
    u = _mm_split(hn, w_in.astype(BF16))
    tmat, bmat, cmat, a_same, a_swap = _s5_operators(lam_re, lam_im, log_dt, b_re, b_im, c_re, c_im, d_skip)
    y = _s5_core(u, tmat, bmat, cmat, a_same, a_swap, batch, seq)
    wr_pad = jnp.concatenate([w_router, jnp.zeros((D_MODEL, LANES - N_EXPERTS), w_router.dtype)], axis=1)
    return _glu_router(y, w_glu.astype(BF16), h, ffn_g.reshape(1, -1), wr_pad)


def kernel(x, positions, mix_norm, ffn_norm, final_norm, mla_w_in, mla_q_norm, mla_w_uq, mla_kv_norm, mla_w_ukv, mla_w_o, ffn_w_gate, ffn_w_up, ffn_w_down, s5_w_in, s5_lambda_re, s5_lambda_im, s5_log_dt, s5_b_re, s5_b_im, s5_c_re, s5_c_im, s5_d, s5_w_glu, moe_w_router, moe_w_gate, moe_w_up, moe_w_down):
    batch, seq, d = x.shape
    h0 = x.reshape(batch * seq, d)
    h1, hn1 = _mla_layer(h0, positions, mix_norm[0], mla_w_in[0], mla_q_norm[0], mla_w_uq[0],
                         mla_kv_norm[0], mla_w_ukv[0], mla_w_o[0], ffn_norm[0])
    h2, hn2 = _ffn(hn1, h1, ffn_w_gate[0].astype(BF16), ffn_w_up[0].astype(BF16),
                   ffn_w_down[0].astype(BF16), mix_norm[1].reshape(1, -1))
    h3, hn3, route = _s5_layer(h2, hn2, batch, seq, s5_w_in[0], s5_lambda_re[0], s5_lambda_im[0],
                              s5_log_dt[0], s5_b_re[0], s5_b_im[0], s5_c_re[0], s5_c_im[0], s5_d[0],
                              s5_w_glu[0], ffn_norm[1], moe_w_router[0])
    out = _moe_layer(h3, hn3, route, moe_w_gate[0].astype(BF16), moe_w_up[0].astype(BF16),
                     moe_w_down[0].astype(BF16), final_norm.reshape(1, -1))
    return out.reshape(batch, seq, d)
```

```python
import functools
import math

import jax
import jax.numpy as jnp
import numpy as np
from jax import lax
from jax.experimental import pallas as pl
from jax.experimental.pallas import tpu as pltpu

F32 = jnp.float32
BF16 = jnp.bfloat16

D_MODEL = 1024
MLA_HEADS = 16
QK_NOPE = 64
QK_ROPE = 32
V_DIM = 64
Q_LORA = 384
KV_LORA = 256
ROPE_THETA = 10000.0
HEAD_PAD = 128
S5_GROUP = 16
S5_GROUPS = D_MODEL // S5_GROUP
S5_STATE = 64
S5_CHUNK = 16
S5_SUPER = 8
N_SUPER = S5_GROUPS // S5_SUPER
N_EXPERTS = 8
EPS = 1e-6
NEG_INF = -1e30
LANES = 128
VMEM_LIMIT = 56 * 1024 * 1024


def _cparams(*sem):
    return pltpu.CompilerParams(dimension_semantics=sem, vmem_limit_bytes=VMEM_LIMIT)


def _rms(x, g):
    return x * lax.rsqrt(jnp.mean(x * x, axis=-1, keepdims=True) + EPS) * g


def _mla_proj_kernel(x_ref, pos_ref, g_ref, win_ref, qn_ref, kvn_ref, wq_ref, wkv_ref, tab_ref,
                     q_ref, k_ref, v_ref):
    hn = _rms(x_ref[...], g_ref[...]).astype(BF16)
    proj = jnp.dot(hn, win_ref[...], preferred_element_type=F32)
    cq = _rms(proj[:, :Q_LORA], qn_ref[...]).astype(BF16)
    ckv = _rms(proj[:, Q_LORA:Q_LORA + KV_LORA], kvn_ref[...]).astype(BF16)
    ka = proj[:, 640:768]
    kb = proj[:, 768:896]
    tab = tab_ref[...]
    ang = pos_ref[...] * tab[0:1, :]
    cos = jnp.cos(ang)
    sin = jnp.sin(ang)
    kp = ka * (cos * tab[1:2, :]) + kb * (sin * tab[2:3, :])
    tq = cos * tab[3:4, :] + sin * tab[4:5, :]
    q = jnp.dot(cq, wq_ref[...], preferred_element_type=F32)
    kv = jnp.dot(ckv, wkv_ref[...], preferred_element_type=F32)
    for h in range(MLA_HEADS):
        sl = slice(h * HEAD_PAD, (h + 1) * HEAD_PAD)
        q_ref[:, sl] = (q[:, sl] * tq).astype(BF16)
        k_ref[:, sl] = (kv[:, sl] + kp).astype(BF16)
    v = kv[:, MLA_HEADS * HEAD_PAD:]
    vlane = lax.broadcasted_iota(jnp.int32, v.shape, 1) % HEAD_PAD
    v_ref[...] = jnp.where(vlane == V_DIM, 1.0, v).astype(BF16)


def _mla_tables():
    inv_freq = ROPE_THETA ** (-np.arange(0, QK_ROPE, 2, dtype=np.float32) / QK_ROPE)
    inv_freq = inv_freq.astype(np.float32)
    half = QK_ROPE // 2
    tab = np.zeros((8, LANES), np.float32)
    scale = math.log2(math.e) / math.sqrt(QK_NOPE + QK_ROPE)
    tab[3, :QK_NOPE] = scale
    kc = [1.0, -1.0, 1.0, 1.0]
    ks = [-1.0, 1.0, 1.0, 1.0]
    for grp in range(4):
        sl = slice(QK_NOPE + grp * half, QK_NOPE + (grp + 1) * half)
        tab[0, sl] = inv_freq
        tab[1, sl] = kc[grp]
        tab[2, sl] = ks[grp]
        if grp % 2 == 0:
            tab[3, sl] = scale
        else:
            tab[4, sl] = scale
    return jnp.asarray(tab)


def _mla_weights(w_in, w_uq, w_ukv):
    half = QK_ROPE // 2
    base = Q_LORA + KV_LORA
    k1 = w_in[:, base:base + half]
    k2 = w_in[:, base + half:base + QK_ROPE]
    z64 = jnp.zeros((D_MODEL, QK_NOPE), w_in.dtype)
    win_ext = jnp.concatenate([w_in[:, :base], z64, k1, k1, k2, k2, z64, k2, k2, k1, k1], axis=1)
    wq = w_uq.reshape(Q_LORA, MLA_HEADS, QK_NOPE + QK_ROPE)
    q1 = wq[:, :, QK_NOPE:QK_NOPE + half]
    q2 = wq[:, :, QK_NOPE + half:]
    wq_ext = jnp.concatenate([wq[:, :, :QK_NOPE], q1, q2, q2, q1], axis=2)
    wq_ext = wq_ext.reshape(Q_LORA, MLA_HEADS * HEAD_PAD)
    wkv = w_ukv.reshape(KV_LORA, MLA_HEADS, QK_NOPE + V_DIM)
    zk = jnp.zeros((KV_LORA, MLA_HEADS, HEAD_PAD - QK_NOPE), w_ukv.dtype)
    zv = jnp.zeros((KV_LORA, MLA_HEADS, HEAD_PAD - V_DIM), w_ukv.dtype)
    wk_ext = jnp.concatenate([wkv[:, :, :QK_NOPE], zk], axis=2).reshape(KV_LORA, -1)
    wv_ext = jnp.concatenate([wkv[:, :, QK_NOPE:], zv], axis=2).reshape(KV_LORA, -1)
    wkv_ext = jnp.concatenate([wk_ext, wv_ext], axis=1)
    return win_ext.astype(BF16), wq_ext.astype(BF16), wkv_ext.astype(BF16)


def _mla_proj(x2d, pos, g, win_ext, qn, kvn, wq_ext, wkv_ext, tm=512):
    n = x2d.shape[0]
    hw = MLA_HEADS * HEAD_PAD
    full = lambda a: pl.BlockSpec(a.shape, lambda i: (0,) * a.ndim)
    tab = _mla_tables()
    return pl.pallas_call(
        _mla_proj_kernel,
        grid=(n // tm,),
        in_specs=[pl.BlockSpec((tm, D_MODEL), lambda i: (i, 0)),
                  pl.BlockSpec((tm, 1), lambda i: (i, 0)),
                  full(g), full(win_ext), full(qn), full(kvn), full(wq_ext), full(wkv_ext), full(tab)],
        out_specs=[pl.BlockSpec((tm, hw), lambda i: (i, 0))] * 3,
        out_shape=[jax.ShapeDtypeStruct((n, hw), BF16)] * 3,
        compiler_params=_cparams("parallel"),
        name="mla_proj",
    )(x2d, pos, g, win_ext, qn, kvn, wq_ext, wkv_ext, tab)


def _attn_kernel(q_ref, k_ref, v_ref, o_ref, sa_ref, sb_ref, *, blk):
    i = pl.program_id(2)
    q = q_ref[...]

    def scores(j, s_ref):
        off = pl.multiple_of(j * blk, blk)
        s_ref[...] = lax.dot_general(q, k_ref[pl.ds(off, blk), :], (((1,), (1,)), ((), ())),
                                     preferred_element_type=F32)

    def update(j, m, acc, s):
        off = pl.multiple_of(j * blk, blk)
        m_new = jnp.maximum(m, jnp.max(s, axis=-1, keepdims=True))
        alpha = jnp.exp2(m - m_new)
        p = jnp.exp2(s - m_new).astype(BF16)
        acc = alpha * acc + jnp.dot(p, v_ref[pl.ds(off, blk), :], preferred_element_type=F32)
        return m_new, acc

    def pair(jj, carry):
        m, acc = carry
        j = 2 * jj
        scores(j + 1, sb_ref)
        m, acc = update(j, m, acc, sa_ref[...])
        scores(j + 2, sa_ref)
        return update(j + 1, m, acc, sb_ref[...])

    def finish(m, acc, s):
        row = lax.broadcasted_iota(jnp.int32, (blk, blk), 0)
        col = lax.broadcasted_iota(jnp.int32, (blk, blk), 1)
        _, acc = update(i, m, acc, jnp.where(col <= row, s, NEG_INF))
        o_ref[...] = (acc / acc[:, V_DIM:V_DIM + 1]).astype(o_ref.dtype)

    scores(0, sa_ref)
    init = (jnp.full((blk, 1), NEG_INF, F32), jnp.zeros((blk, HEAD_PAD), F32))
    m, acc = lax.fori_loop(0, i // 2, pair, init)

    @pl.when(i % 2 == 0)
    def _():
        finish(m, acc, sa_ref[...])

    @pl.when(i % 2 == 1)
    def _():
        scores(i, sb_ref)
        m1, acc1 = update(i - 1, m, acc, sa_ref[...])
        finish(m1, acc1, sb_ref[...])


def _attention(q, k, v, batch, seq, blk=512):
    n, hw = q.shape
    nq = seq // blk
    return pl.pallas_call(
        functools.partial(_attn_kernel, blk=blk),
        grid=(batch, MLA_HEADS, nq),
        in_specs=[pl.BlockSpec((blk, HEAD_PAD), lambda b, h, i: (b * nq + i, h)),
                  pl.BlockSpec((seq, HEAD_PAD), lambda b, h, i: (b, h)),
                  pl.BlockSpec((seq, HEAD_PAD), lambda b, h, i: (b, h))],
        out_specs=pl.BlockSpec((blk, HEAD_PAD), lambda b, h, i: (b * nq + i, h)),
        out_shape=jax.ShapeDtypeStruct((n, hw), BF16),
        scratch_shapes=[pltpu.VMEM((blk, blk), F32), pltpu.VMEM((blk, blk), F32)],
        compiler_params=_cparams("parallel", "parallel", "arbitrary"),
        name="mla_attention",
    )(q, k, v)


def _mm_res_norm_kernel(a_ref, w_ref, res_ref, g_ref, h_ref, hn_ref):
    h = res_ref[...] + jnp.dot(a_ref[...], w_ref[...], preferred_element_type=F32)
    h_ref[...] = h
    hn_ref[...] = _rms(h, g_ref[...]).astype(hn_ref.dtype)


def _mm_res_norm(a, w, res, g, tm=512):
    n, k = a.shape
    d = w.shape[1]
    return pl.pallas_call(
        _mm_res_norm_kernel,
        grid=(n // tm,),
        in_specs=[pl.BlockSpec((tm, k), lambda i: (i, 0)),
                  pl.BlockSpec((k, d), lambda i: (0, 0)),
                  pl.BlockSpec((tm, d), lambda i: (i, 0)),
                  pl.BlockSpec((1, d), lambda i: (0, 0))],
        out_specs=[pl.BlockSpec((tm, d), lambda i: (i, 0))] * 2,
        out_shape=[jax.ShapeDtypeStruct((n, d), F32), jax.ShapeDtypeStruct((n, d), BF16)],
        compiler_params=_cparams("parallel"),
        name="proj_residual_norm",
    )(a, w, res, g)


def _ffn_kernel(hn_ref, res_ref, wg_ref, wu_ref, wd_ref, g2_ref, out_ref, outn_ref):
    kk = pl.program_id(1)

    @pl.when(kk == 0)
    def _():
        out_ref[...] = res_ref[...]

    hn = hn_ref[...]
    gate = jnp.dot(hn, wg_ref[...], preferred_element_type=F32)
    up = jnp.dot(hn, wu_ref[...], preferred_element_type=F32)
    act = (jax.nn.silu(gate) * up).astype(BF16)
    out_ref[...] += jnp.dot(act, wd_ref[...], preferred_element_type=F32)

    @pl.when(kk == pl.num_programs(1) - 1)
    def _():
        outn_ref[...] = _rms(out_ref[...], g2_ref[...]).astype(outn_ref.dtype)


def _ffn(hn, res, wg, wu, wd, g2, tm=512, tf=1408):
    n, d = hn.shape
    f = wg.shape[1]
    return pl.pallas_call(
        _ffn_kernel,
        grid=(n // tm, f // tf),
        in_specs=[pl.BlockSpec((tm, d), lambda i, k: (i, 0)),
                  pl.BlockSpec((tm, d), lambda i, k: (i, 0)),
                  pl.BlockSpec((d, tf), lambda i, k: (0, k)),
                  pl.BlockSpec((d, tf), lambda i, k: (0, k)),
                  pl.BlockSpec((tf, d), lambda i, k: (k, 0)),
                  pl.BlockSpec((1, d), lambda i, k: (0, 0))],
        out_specs=[pl.BlockSpec((tm, d), lambda i, k: (i, 0))] * 2,
        out_shape=[jax.ShapeDtypeStruct((n, d), F32), jax.ShapeDtypeStruct((n, d), BF16)],
        compiler_params=_cparams("parallel", "arbitrary"),
        name="swiglu_dense",
    )(hn, res, wg, wu, wd, g2)


def _mm_split_kernel(a_ref, w_ref, o_ref):
    r = jnp.dot(a_ref[...], w_ref[...], preferred_element_type=F32)
    for g in range(N_SUPER):
        o_ref[g] = r[:, g * LANES:(g + 1) * LANES].astype(o_ref.dtype)


def _mm_split(a, w, tm=512):
    n, k = a.shape
    return pl.pallas_call(
        _mm_split_kernel,
        grid=(n // tm,),
        in_specs=[pl.BlockSpec((tm, k), lambda i: (i, 0)),
                  pl.BlockSpec(w.shape, lambda i: (0, 0))],
        out_specs=pl.BlockSpec((N_SUPER, tm, LANES), lambda i: (0, i, 0)),
        out_shape=jax.ShapeDtypeStruct((N_SUPER, n, LANES), BF16),
        compiler_params=_cparams("parallel"),
        name="s5_in_proj",
    )(a, w)


def _s5_param_kernel(lr_ref, li_ref, ldt_ref, btr_ref, bti_ref, cr_ref, ci_ref, d_ref,
                     kt_ref, er_ref, ei_ref, wr_ref, wi_ref, ac_ref):
    L = S5_CHUNK
    lr = lr_ref[...]
    li = li_ref[...]
    dt = jnp.exp(ldt_ref[...])
    mag = jnp.exp(lr * dt)
    ar = mag * jnp.cos(li * dt)
    ai = mag * jnp.sin(li * dt)
    den = lr * lr + li * li
    gr = ((ar - 1.0) * lr + ai * li) / den
    gi = (ai * lr - (ar - 1.0) * li) / den
    btr = btr_ref[...]
    bti = bti_ref[...]
    bbr = gr * btr - gi * bti
    bbi = gr * bti + gi * btr
    kf = lax.broadcasted_iota(jnp.int32, (L, 1), 0).astype(F32)
    pmag = jnp.exp(lr * dt * kf)
    pr = pmag * jnp.cos(li * dt * kf)
    pi = pmag * jnp.sin(li * dt * kf)
    cr = cr_ref[...]
    ci = ci_ref[...]
    pr3 = pr[:, None, :]
    pi3 = pi[:, None, :]
    wr = cr[None] * pr3 - ci[None] * pi3
    wi = cr[None] * pi3 + ci[None] * pr3
    dn = (((1,), (1,)), ((), ()))
    hp = lax.Precision.HIGHEST
    kt = (lax.dot_general(wr.reshape(L * S5_GROUP, S5_STATE), bbr, dn, precision=hp,
                          preferred_element_type=F32)
          - lax.dot_general(wi.reshape(L * S5_GROUP, S5_STATE), bbi, dn, precision=hp,
                            preferred_element_type=F32))
    row = lax.broadcasted_iota(jnp.int32, kt.shape, 0)
    col = lax.broadcasted_iota(jnp.int32, kt.shape, 1)
    kt_ref[...] = kt + jnp.where(row == col, d_ref[...], 0.0)
    er_ref[...] = pr3 * bbr[None] - pi3 * bbi[None]
    ei_ref[...] = pr3 * bbi[None] + pi3 * bbr[None]
    wr_ref[...] = wr * ar - wi * ai
    wi_ref[...] = wr * ai + wi * ar
    lastr = pr[L - 1:L, :]
    lasti = pi[L - 1:L, :]
    ac_ref[0:1, :] = lastr * ar - lasti * ai
    ac_ref[1:2, :] = lastr * ai + lasti * ar


def _s5_operators(lam_re, lam_im, log_dt, b_re, b_im, c_re, c_im, d_skip):
    G, P, H, L = S5_GROUPS, S5_STATE, S5_GROUP, S5_CHUNK
    per_g = lambda *shape: pl.BlockSpec((None,) + shape, lambda g: (g,) + (0,) * len(shape))
    kt, er, ei, wr, wi, ac = pl.pallas_call(
        _s5_param_kernel,
        grid=(G,),
        in_specs=[per_g(1, P), per_g(1, P), per_g(1, 1), per_g(H, P), per_g(H, P),
                  per_g(H, P), per_g(H, P), per_g(1, H)],
        out_specs=[per_g(L * H, H), per_g(L, H, P), per_g(L, H, P), per_g(L, H, P), per_g(L, H, P),
                   per_g(2, P)],
        out_shape=[jax.ShapeDtypeStruct((G, L * H, H), F32)]
        + [jax.ShapeDtypeStruct((G, L, H, P), F32)] * 4
        + [jax.ShapeDtypeStruct((G, 2, P), F32)],
        compiler_params=_cparams("parallel"),
        name="s5_operators",
    )(lam_re.reshape(G, 1, P), lam_im.reshape(G, 1, P), log_dt.reshape(G, 1, 1),
      b_re.transpose(0, 2, 1), b_im.transpose(0, 2, 1), c_re, c_im, d_skip.reshape(G, 1, H))

    ns, sg = N_SUPER, S5_SUPER
    eye = jnp.eye(sg, dtype=F32)
    ktg = kt.reshape(ns, sg, L, H, H)
    kd = jnp.einsum('SgtOI,gk->StgIkO', ktg, eye).reshape(ns, L, LANES, LANES).astype(BF16)
    lag = np.arange(L)[None, :] - np.arange(L)[:, None]
    causal = jnp.asarray(lag >= 0, BF16)[None, :, :, None, None]
    toe = kd[:, np.clip(lag, 0, None)] * causal
    tmat = toe.transpose(0, 1, 3, 2, 4).reshape(ns, L * LANES, L * LANES)
    grp_of_lane = np.arange(LANES) // H
    grp_mask = jnp.asarray(grp_of_lane[:, None] == np.arange(sg)[None, :], BF16)
    ecat = jnp.stack([er, ei], axis=3)[:, ::-1]
    eb = ecat.reshape(ns, sg, L, H, 2, P).transpose(0, 2, 1, 3, 4, 5).reshape(ns, L, LANES, LANES)
    bmat = eb.astype(BF16)[:, :, :, None, :] * grp_mask[None, None, :, :, None]
    bmat = bmat.reshape(ns, L * LANES, 2 * sg * P)
    wcat = jnp.stack([wr, -wi], axis=3).reshape(ns, sg, L, H, 2, P)
    wc = wcat.transpose(0, 4, 5, 2, 1, 3).reshape(ns, LANES, L, LANES)
    cmat = wc.astype(BF16)[:, None, :, :, :] * grp_mask.T[None, :, None, None, :]
    cmat = cmat.reshape(ns, 2 * sg * P, L * LANES)
    acr = ac[:, 0, :].reshape(ns, sg, P)
    aci = ac[:, 1, :].reshape(ns, sg, P)
    a_same = jnp.concatenate([acr, acr], axis=-1)
    a_swap = jnp.concatenate([-aci, aci], axis=-1)
    return tmat.astype(BF16), bmat.astype(BF16), cmat.astype(BF16), a_same, a_swap


def _s5_inc_kernel(u_ref, b_ref, o_ref):
    o_ref[...] = jnp.dot(u_ref[...], b_ref[...], preferred_element_type=F32)


def _s5_scan_kernel(inc_ref, asame_ref, aswap_ref, x_ref, state_ref, *, cb):
    @pl.when(pl.program_id(0) == 0)
    def _():
        state_ref[...] = jnp.zeros_like(state_ref)

    nseq = state_ref.shape[0]
    a_same = asame_ref[...].reshape(nseq * S5_SUPER, LANES)
    a_swap = aswap_ref[...].reshape(nseq * S5_SUPER, LANES)

    def body(c, x):
        x_ref[:, c] = x.reshape(nseq, S5_SUPER, LANES)
        inc = inc_ref[:, c].reshape(nseq * S5_SUPER, LANES)
        return x * a_same + pltpu.roll(x, LANES // 2, 1) * a_swap + inc

    x0 = state_ref[...].reshape(nseq * S5_SUPER, LANES)
    state_ref[...] = lax.fori_loop(0, cb, body, x0).reshape(nseq, S5_SUPER, LANES)


def _s5_out_kernel(u_ref, x_ref, t_ref, c_ref, y_ref):
    y = jnp.dot(u_ref[...], t_ref[...], preferred_element_type=F32)
    y += jnp.dot(x_ref[...].astype(BF16), c_ref[...], preferred_element_type=F32)
    y_ref[...] = jax.nn.gelu(y).astype(y_ref.dtype)


def _s5_core(u, tmat, bmat, cmat, a_same, a_swap, batch, seq, tr=512):
    ns, n, _ = u.shape
    L = S5_CHUNK
    rows = n // L
    n_chunks = seq // L
    width = L * LANES
    sw = 2 * S5_SUPER * S5_STATE
    uc = u.reshape(ns, rows, width)
    tr = min(tr, rows)
    inc = pl.pallas_call(
        _s5_inc_kernel,
        grid=(ns, rows // tr),
        in_specs=[pl.BlockSpec((None, tr, width), lambda s, r: (s, r, 0)),
                  pl.BlockSpec((None, width, sw), lambda s, r: (s, 0, 0))],
        out_specs=pl.BlockSpec((None, tr, sw), lambda s, r: (s, r, 0)),
        out_shape=jax.ShapeDtypeStruct((ns, rows, sw), F32),
        compiler_params=_cparams("parallel", "parallel"),
        name="s5_chunk_state",
    )(uc, bmat)
    nseq = ns * batch
    cb = min(32, n_chunks)
    tile = (S5_SUPER, LANES)
    bcast = lambda a: jnp.broadcast_to(a[:, None], (ns, batch) + tile).reshape((nseq,) + tile)
    seq_spec = pl.BlockSpec((nseq, cb) + tile, lambda c: (0, c, 0, 0))
    tab_spec = pl.BlockSpec((nseq,) + tile, lambda c: (0, 0, 0))
    xprev = pl.pallas_call(
        functools.partial(_s5_scan_kernel, cb=cb),
        grid=(n_chunks // cb,),
        in_specs=[seq_spec, tab_spec, tab_spec],
        out_specs=seq_spec,
        out_shape=jax.ShapeDtypeStruct((nseq, n_chunks) + tile, F32),
        scratch_shapes=[pltpu.VMEM((nseq,) + tile, F32)],
        compiler_params=_cparams("arbitrary"),
        name="s5_chunk_scan",
    )(inc.reshape((nseq, n_chunks) + tile), bcast(a_same), bcast(a_swap))
    xprev = xprev.reshape(ns, rows, sw)
    y = pl.pallas_call(
        _s5_out_kernel,
        grid=(ns, rows // tr),
        in_specs=[pl.BlockSpec((None, tr, width), lambda s, r: (s, r, 0)),
                  pl.BlockSpec((None, tr, sw), lambda s, r: (s, r, 0)),
                  pl.BlockSpec((None, width, width), lambda s, r: (s, 0, 0)),
                  pl.BlockSpec((None, sw, width), lambda s, r: (s, 0, 0))],
        out_specs=pl.BlockSpec((None, tr, width), lambda s, r: (s, r, 0)),
        out_shape=jax.ShapeDtypeStruct((ns, rows, width), BF16),
        compiler_params=_cparams("parallel", "parallel"),
        name="s5_chunk_out",
    )(uc, xprev, tmat, cmat)
    return y.reshape(ns, n, LANES)


def _glu_router_kernel(y_ref, w_ref, res_ref, g_ref, wr_ref, h_ref, hn_ref, route_ref):
    d = res_ref.shape[1]
    y = jnp.concatenate([y_ref[g] for g in range(N_SUPER)], axis=-1)
    z = jnp.dot(y, w_ref[...], preferred_element_type=F32)
    h = res_ref[...] + z[:, :d] * jax.nn.sigmoid(z[:, d:])
    h_ref[...] = h
    hn = _rms(h, g_ref[...])
    hn_ref[...] = hn.astype(hn_ref.dtype)
    logits = jnp.dot(hn, wr_ref[...], preferred_element_type=F32, precision=lax.Precision.HIGHEST)
    lane = lax.broadcasted_iota(jnp.int32, logits.shape, 1)
    logits = jnp.where(lane < N_EXPERTS, logits, -jnp.inf)
    m1 = jnp.max(logits, axis=-1, keepdims=True)
    i1 = jnp.min(jnp.where(logits == m1, lane, LANES), axis=-1, keepdims=True)
    rest = jnp.where(lane == i1, -jnp.inf, logits)
    m2 = jnp.max(rest, axis=-1, keepdims=True)
    i2 = jnp.min(jnp.where(rest == m2, lane, LANES), axis=-1, keepdims=True)
    e2 = jnp.exp(m2 - m1)
    g1 = 1.0 / (1.0 + e2)
    route_ref[...] = (jnp.where(lane == 0, i1.astype(F32), 0.0) + jnp.where(lane == 1, i2.astype(F32), 0.0)
                      + jnp.where(lane == 2, g1, 0.0) + jnp.where(lane == 3, e2 * g1, 0.0))


def _glu_router(y, w_glu, res, g, w_router_pad, tm=512):
    ns, n, _ = y.shape
    d = res.shape[1]
    return pl.pallas_call(
        _glu_router_kernel,
        grid=(n // tm,),
        in_specs=[pl.BlockSpec((ns, tm, LANES), lambda i: (0, i, 0)),
                  pl.BlockSpec(w_glu.shape, lambda i: (0, 0)),
                  pl.BlockSpec((tm, d), lambda i: (i, 0)),
                  pl.BlockSpec((1, d), lambda i: (0, 0)),
                  pl.BlockSpec(w_router_pad.shape, lambda i: (0, 0))],
        out_specs=[pl.BlockSpec((tm, d), lambda i: (i, 0)),
                   pl.BlockSpec((tm, d), lambda i: (i, 0)),
                   pl.BlockSpec((tm, LANES), lambda i: (i, 0))],
        out_shape=[jax.ShapeDtypeStruct((n, d), F32), jax.ShapeDtypeStruct((n, d), F32),
                   jax.ShapeDtypeStruct((n, LANES), F32)],
        compiler_params=_cparams("parallel"),
        name="s5_glu_router",
    )(y, w_glu, res, g, w_router_pad)


MOE_TM = 1024
MOE_TF = 896
DISPATCH_BLOCK = 512
COMBINE_TM = 512


def _moe_plan(route, n_tok):
    ne, tm = N_EXPERTS, MOE_TM
    n_tiles = (2 * n_tok) // tm + ne
    eid = route[:, :2].astype(jnp.int32).reshape(-1)
    onehot = (eid[:, None] == jnp.arange(ne, dtype=jnp.int32)[None, :]).astype(jnp.int32)
    csum = jnp.cumsum(onehot, axis=0)
    rank = jnp.sum((csum - onehot) * onehot, axis=1)
    cnt = csum[-1]
    padded = ((cnt + tm - 1) // tm) * tm
    ends = jnp.cumsum(padded)
    off = ends - padded
    pos = (jnp.sum(off[None, :] * onehot, axis=1) + rank).astype(jnp.int32)
    tile_end = ends // tm
    n_valid = tile_end[-1:].astype(jnp.int32)
    tiles = jnp.arange(n_tiles, dtype=jnp.int32)
    tile_expert = jnp.minimum(jnp.sum(tiles[:, None] >= tile_end[None, :], axis=1), ne - 1).astype(jnp.int32)
    pad_start = (off + cnt).astype(jnp.int32)
    pad_len = (padded - cnt).astype(jnp.int32)
    return pos, tile_expert, n_valid, pad_start, pad_len, n_tiles


def _dispatch_kernel(pos_ref, pstart_ref, plen_ref, hn_ref, xs_ref, zero_ref, sem, *, n_tok):
    nb = n_tok // DISPATCH_BLOCK
    b = pl.program_id(0)

    def row_copy(r, k):
        dst = pos_ref[2 * (b * DISPATCH_BLOCK + r) + k]
        return pltpu.make_async_copy(hn_ref.at[pl.ds(r, 1)], xs_ref.at[pl.ds(dst, 1)], sem)

    def issue(r, _):
        row_copy(r, 0).start()
        row_copy(r, 1).start()
        return 0
    lax.fori_loop(0, DISPATCH_BLOCK, issue, 0, unroll=8)

    def drain(r, _):
        row_copy(r, 0).wait()
        row_copy(r, 1).wait()
        return 0
    lax.fori_loop(0, DISPATCH_BLOCK, drain, 0, unroll=8)

    @pl.when(b == nb - 1)
    def _():
        zero_ref[...] = jnp.zeros_like(zero_ref)

        def pad_copy(e, r):
            return pltpu.make_async_copy(zero_ref.at[pl.ds(0, 1)], xs_ref.at[pl.ds(pstart_ref[e] + r, 1)], sem)

        for e in range(N_EXPERTS):
            def pbody(r, _, e=e):
                pad_copy(e, r).start()
                return 0
            lax.fori_loop(0, plen_ref[e], pbody, 0)
        for e in range(N_EXPERTS):
            def wbody(r, _, e=e):
                pad_copy(e, r).wait()
                return 0
            lax.fori_loop(0, plen_ref[e], wbody, 0)


def _dispatch(hn, pos, pad_start, pad_len, n_rows):
    n_tok, d = hn.shape
    return pl.pallas_call(
        functools.partial(_dispatch_kernel, n_tok=n_tok),
        grid_spec=pltpu.PrefetchScalarGridSpec(
            num_scalar_prefetch=3,
            grid=(n_tok // DISPATCH_BLOCK,),
            in_specs=[pl.BlockSpec((DISPATCH_BLOCK, d), lambda b, pos, ps, plen: (b, 0))],
            out_specs=pl.BlockSpec(memory_space=pl.ANY),
            scratch_shapes=[pltpu.VMEM((8, d), F32), pltpu.SemaphoreType.DMA]),
        out_shape=jax.ShapeDtypeStruct((n_rows, d), F32),
        compiler_params=_cparams("arbitrary"),
        name="moe_dispatch",
    )(pos, pad_start, pad_len, hn)


def _moe_kernel(te_ref, nv_ref, x_ref, wg_ref, wu_ref, wd_ref, y_ref, xb_ref):
    i = pl.program_id(0)
    k = pl.program_id(1)

    @pl.when(i < nv_ref[0])
    def _():
        @pl.when(k == 0)
        def _():
            xb_ref[...] = x_ref[...].astype(BF16)

        hn = xb_ref[...]
        gate = jnp.dot(hn, wg_ref[...], preferred_element_type=F32)
        up = jnp.dot(hn, wu_ref[...], preferred_element_type=F32)
        act = (jax.nn.silu(gate) * up).astype(BF16)
        part = jnp.dot(act, wd_ref[...], preferred_element_type=F32)

        @pl.when(k == 0)
        def _():
            y_ref[...] = part

        @pl.when(k > 0)
        def _():
            y_ref[...] += part


def _moe_experts(xs, tile_expert, n_valid, wg, wu, wd, n_tiles):
    tm, tf = MOE_TM, MOE_TF
    d = xs.shape[1]
    f = wg.shape[2]
    nk = f // tf

    def row_map(i, k, te, nv):
        return (jnp.minimum(i, nv[0] - 1), 0)

    def kk(i, k, nv):
        return jnp.where(i < nv[0], k, nk - 1)

    def w_in_map(i, k, te, nv):
        return (te[jnp.minimum(i, nv[0] - 1)], 0, kk(i, k, nv))

    def w_out_map(i, k, te, nv):
        return (te[jnp.minimum(i, nv[0] - 1)], kk(i, k, nv), 0)

    return pl.pallas_call(
        _moe_kernel,
        grid_spec=pltpu.PrefetchScalarGridSpec(
            num_scalar_prefetch=2,
            grid=(n_tiles, nk),
            in_specs=[pl.BlockSpec((tm, d), row_map),
                      pl.BlockSpec((None, d, tf), w_in_map),
                      pl.BlockSpec((None, d, tf), w_in_map),
                      pl.BlockSpec((None, tf, d), w_out_map)],
            out_specs=pl.BlockSpec((tm, d), row_map),
            scratch_shapes=[pltpu.VMEM((tm, d), BF16)]),
        out_shape=jax.ShapeDtypeStruct(xs.shape, F32),
        compiler_params=_cparams("arbitrary", "arbitrary"),
        name="moe_experts",
    )(tile_expert, n_valid, xs, wg, wu, wd)


def _combine_kernel(pos_ref, y_ref, h_ref, route_ref, g_ref, out_ref, buf_ref, sem, *, tm, n_steps):
    i = pl.program_id(0)
    slot = i % 2

    def row_copy(step, sl, r, k):
        t = step * tm + r
        return pltpu.make_async_copy(y_ref.at[pl.ds(pos_ref[2 * t + k], 1)],
                                     buf_ref.at[sl, k, pl.ds(r, 1)], sem.at[sl])

    def issue(step, sl):
        def body(r, _):
            row_copy(step, sl, r, 0).start()
            row_copy(step, sl, r, 1).start()
            return 0
        lax.fori_loop(0, tm, body, 0, unroll=8)

    @pl.when(i == 0)
    def _():
        issue(0, 0)

    @pl.when(i + 1 < n_steps)
    def _():
        issue(i + 1, 1 - slot)

    def wbody(r, _):
        row_copy(i, slot, r, 0).wait()
        row_copy(i, slot, r, 1).wait()
        return 0
    lax.fori_loop(0, tm, wbody, 0, unroll=8)

    route = route_ref[...]
    out = h_ref[...] + route[:, 2:3] * buf_ref[slot, 0] + route[:, 3:4] * buf_ref[slot, 1]
    out_ref[...] = _rms(out, g_ref[...])


def _combine(y, pos, h, route, g):
    n_tok, d = h.shape
    tm = COMBINE_TM
    n_steps = n_tok // tm
    return pl.pallas_call(
        functools.partial(_combine_kernel, tm=tm, n_steps=n_steps),
        grid_spec=pltpu.PrefetchScalarGridSpec(
            num_scalar_prefetch=1,
            grid=(n_steps,),
            in_specs=[pl.BlockSpec(memory_space=pl.ANY),
                      pl.BlockSpec((tm, d), lambda i, pos: (i, 0)),
                      pl.BlockSpec((tm, LANES), lambda i, pos: (i, 0)),
                      pl.BlockSpec((1, d), lambda i, pos: (0, 0))],
            out_specs=pl.BlockSpec((tm, d), lambda i, pos: (i, 0)),
            scratch_shapes=[pltpu.VMEM((2, 2, tm, d), F32), pltpu.SemaphoreType.DMA((2,))]),
        out_shape=jax.ShapeDtypeStruct((n_tok, d), F32),
        compiler_params=_cparams("arbitrary"),
        name="moe_combine",
    )(pos, y, h, route, g)


def _moe_layer(h, hn, route, wg, wu, wd, g):
    n_tok = h.shape[0]
    pos, tile_expert, n_valid, pad_start, pad_len, n_tiles = _moe_plan(route, n_tok)
    xs = _dispatch(hn, pos, pad_start, pad_len, n_tiles * MOE_TM)
    y = _moe_experts(xs, tile_expert, n_valid, wg, wu, wd, n_tiles)
    return _combine(y, pos, h, route, g)


def _mla_layer(h, positions, mix_g, w_in, q_norm, w_uq, kv_norm, w_ukv, w_o, ffn_g):
    batch, seq = positions.shape
    win_ext, wq_ext, wkv_ext = _mla_weights(w_in, w_uq, w_ukv)
    pos = positions.astype(F32).reshape(batch * seq, 1)
    q, k, v = _mla_proj(h, pos, mix_g.reshape(1, -1), win_ext, q_norm.reshape(1, -1),
                        kv_norm.reshape(1, -1), wq_ext, wkv_ext)
    o = _attention(q, k, v, batch, seq)
    wo = w_o.reshape(MLA_HEADS, V_DIM, D_MODEL)
    wo_ext = jnp.concatenate([wo, jnp.zeros((MLA_HEADS, HEAD_PAD - V_DIM, D_MODEL), wo.dtype)], axis=1)
    wo_ext = wo_ext.reshape(MLA_HEADS * HEAD_PAD, D_MODEL).astype(BF16)
    return _mm_res_norm(o, wo_ext, h, ffn_g.reshape(1, -1))


def _s5_layer(h, hn, batch, seq, w_in, lam_re, lam_im, log_dt, b_re, b_im, c_re, c_im, d_skip, w_glu,
              ffn_g, w_router):
    u = _mm_split(hn, w_in.astype(BF16))
    tmat, bmat, cmat, a_same, a_swap = _s5_operators(lam_re, lam_im, log_dt, b_re, b_im, c_re, c_im, d_skip)
    y = _s5_core(u, tmat, bmat, cmat, a_same, a_swap, batch, seq)
    wr_pad = jnp.concatenate([w_router, jnp.zeros((D_MODEL, LANES - N_EXPERTS), w_router.dtype)], axis=1)
    return _glu_router(y, w_glu.astype(BF16), h, ffn_g.reshape(1, -1), wr_pad)


def kernel(x, positions, mix_norm, ffn_norm, final_norm, mla_w_in, mla_q_norm, mla_w_uq, mla_kv_norm, mla_w_ukv, mla_w_o, ffn_w_gate, ffn_w_up, ffn_w_down, s5_w_in, s5_lambda_re, s5_lambda_im, s5_log_dt, s5_b_re, s5_b_im, s5_c_re, s5_c_im, s5_d, s5_w_glu, moe_w_router, moe_w_gate, moe_w_up, moe_w_down):
    batch, seq, d = x.shape
    h0 = x.reshape(batch * seq, d)
    h1, hn1 = _mla_layer(h0, positions, mix_norm[0], mla_w_in[0], mla_q_norm[0], mla_w_uq[0],
                         mla_kv_norm[0], mla_w_ukv[0], mla_w_o[0], ffn_norm[0])
    h2, hn2 = _ffn(hn1, h1, ffn_w_gate[0].astype(BF16), ffn_w_up[0].astype(BF16),
                   ffn_w_down[0].astype(BF16), mix_norm[1].reshape(1, -1))
    h3, hn3, route = _s5_layer(h2, hn2, batch, seq, s5_w_in[0], s5_lambda_re[0], s5_lambda_im[0],
                              s5_log_dt[0], s5_b_re[0], s5_b_im[0], s5_c_re[0], s5_c_im[0], s5_d[0],
                              s5_w_glu[0], ffn_norm[1], moe_w_router[0])
    out = _moe_layer(h3, hn3, route, moe_w_gate[0].astype(BF16), moe_w_up[0].astype(BF16),
                     moe_w_down[0].astype(BF16), final_norm.reshape(1, -1))
    return out.reshape(batch, seq, d)
```

```python
import functools
import math

import jax
import jax.numpy as jnp
import numpy as np
from jax import lax
from jax.experimental import pallas as pl
from jax.experimental.pallas import tpu as pltpu

F32 = jnp.float32
BF16 = jnp.bfloat16

D_MODEL = 1024
MLA_HEADS = 16
QK_NOPE = 64
QK_ROPE = 32
V_DIM = 64
Q_LORA = 384
KV_LORA = 256
ROPE_THETA = 10000.0
HEAD_PAD = 128
ATTN_HEADS_PER_STEP = 2
S5_GROUP = 16
S5_GROUPS = D_MODEL // S5_GROUP
S5_STATE = 64
S5_CHUNK = 16
S5_SUPER = 8
N_SUPER = S5_GROUPS // S5_SUPER
N_EXPERTS = 8
EPS = 1e-6
NEG_INF = -1e30
LANES = 128
VMEM_LIMIT = 56 * 1024 * 1024


def _cparams(*sem):
    return pltpu.CompilerParams(dimension_semantics=sem, vmem_limit_bytes=VMEM_LIMIT)


def _rms(x, g):
    return x * lax.rsqrt(jnp.mean(x * x, axis=-1, keepdims=True) + EPS) * g


def _mla_proj_kernel(x_ref, pos_ref, g_ref, win_ref, qn_ref, kvn_ref, wq_ref, wkv_ref, tab_ref,
                     q_ref, k_ref, v_ref):
    hn = _rms(x_ref[...], g_ref[...]).astype(BF16)
    proj = jnp.dot(hn, win_ref[...], preferred_element_type=F32)
    cq = _rms(proj[:, :Q_LORA], qn_ref[...]).astype(BF16)
    ckv = _rms(proj[:, Q_LORA:Q_LORA + KV_LORA], kvn_ref[...]).astype(BF16)
    ka = proj[:, 640:768]
    kb = proj[:, 768:896]
    tab = tab_ref[...]
    ang = pos_ref[...] * tab[0:1, :]
    cos = jnp.cos(ang)
    sin = jnp.sin(ang)
    kp = ka * (cos * tab[1:2, :]) + kb * (sin * tab[2:3, :])
    tq = cos * tab[3:4, :] + sin * tab[4:5, :]
    q = jnp.dot(cq, wq_ref[...], preferred_element_type=F32)
    kv = jnp.dot(ckv, wkv_ref[...], preferred_element_type=F32)
    for h in range(MLA_HEADS):
        sl = slice(h * HEAD_PAD, (h + 1) * HEAD_PAD)
        q_ref[:, sl] = (q[:, sl] * tq).astype(BF16)
        k_ref[:, sl] = (kv[:, sl] + kp).astype(BF16)
    v = kv[:, MLA_HEADS * HEAD_PAD:]
    vlane = lax.broadcasted_iota(jnp.int32, v.shape, 1) % HEAD_PAD
    v_ref[...] = jnp.where(vlane == V_DIM, 1.0, v).astype(BF16)


def _mla_tables():
    inv_freq = ROPE_THETA ** (-np.arange(0, QK_ROPE, 2, dtype=np.float32) / QK_ROPE)
    inv_freq = inv_freq.astype(np.float32)
    half = QK_ROPE // 2
    tab = np.zeros((8, LANES), np.float32)
    scale = math.log2(math.e) / math.sqrt(QK_NOPE + QK_ROPE)
    tab[3, :QK_NOPE] = scale
    kc = [1.0, -1.0, 1.0, 1.0]
    ks = [-1.0, 1.0, 1.0, 1.0]
    for grp in range(4):
        sl = slice(QK_NOPE + grp * half, QK_NOPE + (grp + 1) * half)
        tab[0, sl] = inv_freq
        tab[1, sl] = kc[grp]
        tab[2, sl] = ks[grp]
        if grp % 2 == 0:
            tab[3, sl] = scale
        else:
            tab[4, sl] = scale
    return jnp.asarray(tab)


def _mla_weights(w_in, w_uq, w_ukv):
    half = QK_ROPE // 2
    base = Q_LORA + KV_LORA
    k1 = w_in[:, base:base + half]
    k2 = w_in[:, base + half:base + QK_ROPE]
    z64 = jnp.zeros((D_MODEL, QK_NOPE), w_in.dtype)
    win_ext = jnp.concatenate([w_in[:, :base], z64, k1, k1, k2, k2, z64, k2, k2, k1, k1], axis=1)
    wq = w_uq.reshape(Q_LORA, MLA_HEADS, QK_NOPE + QK_ROPE)
    q1 = wq[:, :, QK_NOPE:QK_NOPE + half]
    q2 = wq[:, :, QK_NOPE + half:]
    wq_ext = jnp.concatenate([wq[:, :, :QK_NOPE], q1, q2, q2, q1], axis=2)
    wq_ext = wq_ext.reshape(Q_LORA, MLA_HEADS * HEAD_PAD)
    wkv = w_ukv.reshape(KV_LORA, MLA_HEADS, QK_NOPE + V_DIM)
    zk = jnp.zeros((KV_LORA, MLA_HEADS, HEAD_PAD - QK_NOPE), w_ukv.dtype)
    zv = jnp.zeros((KV_LORA, MLA_HEADS, HEAD_PAD - V_DIM), w_ukv.dtype)
    wk_ext = jnp.concatenate([wkv[:, :, :QK_NOPE], zk], axis=2).reshape(KV_LORA, -1)
    wv_ext = jnp.concatenate([wkv[:, :, QK_NOPE:], zv], axis=2).reshape(KV_LORA, -1)
    wkv_ext = jnp.concatenate([wk_ext, wv_ext], axis=1)
    return win_ext.astype(BF16), wq_ext.astype(BF16), wkv_ext.astype(BF16)


def _mla_proj(x2d, pos, g, win_ext, qn, kvn, wq_ext, wkv_ext, tm=512):
    n = x2d.shape[0]
    hw = MLA_HEADS * HEAD_PAD
    full = lambda a: pl.BlockSpec(a.shape, lambda i: (0,) * a.ndim)
    tab = _mla_tables()
    return pl.pallas_call(
        _mla_proj_kernel,
        grid=(n // tm,),
        in_specs=[pl.BlockSpec((tm, D_MODEL), lambda i: (i, 0)),
                  pl.BlockSpec((tm, 1), lambda i: (i, 0)),
                  full(g), full(win_ext), full(qn), full(kvn), full(wq_ext), full(wkv_ext), full(tab)],
        out_specs=[pl.BlockSpec((tm, hw), lambda i: (i, 0))] * 3,
        out_shape=[jax.ShapeDtypeStruct((n, hw), BF16)] * 3,
        compiler_params=_cparams("parallel"),
        name="mla_proj",
    )(x2d, pos, g, win_ext, qn, kvn, wq_ext, wkv_ext, tab)


def _attn_kernel(q_ref, k_ref, v_ref, o_ref, s0_ref, s1_ref, p0_ref, p1_ref, *, blk):
    i = pl.program_id(2)
    heads = range(ATTN_HEADS_PER_STEP)
    lanes = [slice(h * HEAD_PAD, (h + 1) * HEAD_PAD) for h in heads]
    q = [q_ref[:, lanes[h]] for h in heads]

    def stage_a(j, s_ref):
        off = pl.multiple_of(j * blk, blk)
        for h in heads:
            s_ref[h] = lax.dot_general(q[h], k_ref[pl.ds(off, blk), lanes[h]], (((1,), (1,)), ((), ())),
                                       preferred_element_type=F32)

    def stage_b(state, s_ref, p_ref, masked=False):
        out = []
        for h in heads:
            m, _, acc = state[h]
            s = s_ref[h]
            if masked:
                row = lax.broadcasted_iota(jnp.int32, (blk, blk), 0)
                col = lax.broadcasted_iota(jnp.int32, (blk, blk), 1)
                s = jnp.where(col <= row, s, NEG_INF)
            m_new = jnp.maximum(m, jnp.max(s, axis=-1, keepdims=True))
            p_ref[h] = jnp.exp2(s - m_new).astype(BF16)
            out.append((m_new, jnp.exp2(m - m_new), acc))
        return tuple(out)

    def stage_c(j, state, p_ref):
        off = pl.multiple_of(j * blk, blk)
        out = []
        for h in heads:
            m, alpha, acc = state[h]
            pv = jnp.dot(p_ref[h], v_ref[pl.ds(off, blk), lanes[h]], preferred_element_type=F32)
            out.append((m, alpha, alpha * acc + pv))
        return tuple(out)

    def steady(j, state, even):
        s_j, s_n, p_j, p_n = (s0_ref, s1_ref, p0_ref, p1_ref) if even else (s1_ref, s0_ref, p1_ref, p0_ref)
        state = stage_c(j, state, p_j)
        stage_a(j + 2, s_j)
        return stage_b(state, s_n, p_n)

    def drain(state, even_last):
        s_i, p_i, p_prev = (s0_ref, p0_ref, p1_ref) if even_last else (s1_ref, p1_ref, p0_ref)
        state = stage_c(i - 1, state, p_prev)
        state = stage_b(state, s_i, p_i, masked=True)
        finish(stage_c(i, state, p_i))

    def finish(state):
        for h in heads:
            acc = state[h][2]
            o_ref[:, lanes[h]] = (acc / acc[:, V_DIM:V_DIM + 1]).astype(o_ref.dtype)

    init = tuple((jnp.full((blk, 1), NEG_INF, F32), jnp.ones((blk, 1), F32), jnp.zeros((blk, HEAD_PAD), F32))
                 for _ in heads)
    stage_a(0, s0_ref)

    @pl.when(i == 0)
    def _():
        finish(stage_c(0, stage_b(init, s0_ref, p0_ref, masked=True), p0_ref))

    @pl.when(i > 0)
    def _():
        stage_a(1, s1_ref)
        state = stage_b(init, s0_ref, p0_ref)

        def pair(jj, st):
            return steady(2 * jj + 1, steady(2 * jj, st, even=True), even=False)
        state = lax.fori_loop(0, (i - 1) // 2, pair, state)

        @pl.when(i % 2 == 1)
        def _():
            drain(state, even_last=False)

        @pl.when(i % 2 == 0)
        def _():
            drain(steady(i - 2, state, even=True), even_last=True)


def _attention(q, k, v, batch, seq, blk=512):
    n, hw = q.shape
    nq = seq // blk
    nh = ATTN_HEADS_PER_STEP
    width = nh * HEAD_PAD
    return pl.pallas_call(
        functools.partial(_attn_kernel, blk=blk),
        grid=(batch, MLA_HEADS // nh, nq),
        in_specs=[pl.BlockSpec((blk, width), lambda b, h, i: (b * nq + i, h)),
                  pl.BlockSpec((seq, width), lambda b, h, i: (b, h)),
                  pl.BlockSpec((seq, width), lambda b, h, i: (b, h))],
        out_specs=pl.BlockSpec((blk, width), lambda b, h, i: (b * nq + i, h)),
        out_shape=jax.ShapeDtypeStruct((n, hw), BF16),
        scratch_shapes=[pltpu.VMEM((nh, blk, blk), F32), pltpu.VMEM((nh, blk, blk), F32),
                        pltpu.VMEM((nh, blk, blk), BF16), pltpu.VMEM((nh, blk, blk), BF16)],
        compiler_params=_cparams("parallel", "parallel", "arbitrary"),
        name="mla_attention",
    )(q, k, v)


def _mm_res_norm_kernel(a_ref, w_ref, res_ref, g_ref, h_ref, hn_ref):
    h = res_ref[...] + jnp.dot(a_ref[...], w_ref[...], preferred_element_type=F32)
    h_ref[...] = h
    hn_ref[...] = _rms(h, g_ref[...]).astype(hn_ref.dtype)


def _mm_res_norm(a, w, res, g, tm=512):
    n, k = a.shape
    d = w.shape[1]
    return pl.pallas_call(
        _mm_res_norm_kernel,
        grid=(n // tm,),
        in_specs=[pl.BlockSpec((tm, k), lambda i: (i, 0)),
                  pl.BlockSpec((k, d), lambda i: (0, 0)),
                  pl.BlockSpec((tm, d), lambda i: (i, 0)),
                  pl.BlockSpec((1, d), lambda i: (0, 0))],
        out_specs=[pl.BlockSpec((tm, d), lambda i: (i, 0))] * 2,
        out_shape=[jax.ShapeDtypeStruct((n, d), F32), jax.ShapeDtypeStruct((n, d), BF16)],
        compiler_params=_cparams("parallel"),
        name="proj_residual_norm",
    )(a, w, res, g)


def _ffn_kernel(hn_ref, res_ref, wg_ref, wu_ref, wd_ref, g2_ref, out_ref, outn_ref):
    kk = pl.program_id(1)

    @pl.when(kk == 0)
    def _():
        out_ref[...] = res_ref[...]

    hn = hn_ref[...]
    gate = jnp.dot(hn, wg_ref[...], preferred_element_type=F32)
    up = jnp.dot(hn, wu_ref[...], preferred_element_type=F32)
    act = (jax.nn.silu(gate) * up).astype(BF16)
    out_ref[...] += jnp.dot(act, wd_ref[...], preferred_element_type=F32)

    @pl.when(kk == pl.num_programs(1) - 1)
    def _():
        outn_ref[...] = _rms(out_ref[...], g2_ref[...]).astype(outn_ref.dtype)


def _ffn(hn, res, wg, wu, wd, g2, tm=512, tf=1408):
    n, d = hn.shape
    f = wg.shape[1]
    return pl.pallas_call(
        _ffn_kernel,
        grid=(n // tm, f // tf),
        in_specs=[pl.BlockSpec((tm, d), lambda i, k: (i, 0)),
                  pl.BlockSpec((tm, d), lambda i, k: (i, 0)),
                  pl.BlockSpec((d, tf), lambda i, k: (0, k)),
                  pl.BlockSpec((d, tf), lambda i, k: (0, k)),
                  pl.BlockSpec((tf, d), lambda i, k: (k, 0)),
                  pl.BlockSpec((1, d), lambda i, k: (0, 0))],
        out_specs=[pl.BlockSpec((tm, d), lambda i, k: (i, 0))] * 2,
        out_shape=[jax.ShapeDtypeStruct((n, d), F32), jax.ShapeDtypeStruct((n, d), BF16)],
        compiler_params=_cparams("parallel", "arbitrary"),
        name="swiglu_dense",
    )(hn, res, wg, wu, wd, g2)


def _mm_split_kernel(a_ref, w_ref, o_ref, r_ref):
    tm = a_ref.shape[0]
    r = jnp.dot(a_ref[...], w_ref[...], preferred_element_type=F32)
    for g in range(N_SUPER):
        r_ref[g] = r[:, g * LANES:(g + 1) * LANES]
    for g in range(N_SUPER):
        for l in range(S5_CHUNK):
            piece = r_ref[g, pl.ds(l, tm // S5_CHUNK, stride=S5_CHUNK), :]
            o_ref[g, :, l * LANES:(l + 1) * LANES] = piece.astype(o_ref.dtype)


def _mm_split(a, w, tm=512):
    n, k = a.shape
    return pl.pallas_call(
        _mm_split_kernel,
        grid=(n // tm,),
        in_specs=[pl.BlockSpec((tm, k), lambda i: (i, 0)),
                  pl.BlockSpec(w.shape, lambda i: (0, 0))],
        out_specs=pl.BlockSpec((N_SUPER, tm // S5_CHUNK, S5_CHUNK * LANES), lambda i: (0, i, 0)),
        out_shape=jax.ShapeDtypeStruct((N_SUPER, n // S5_CHUNK, S5_CHUNK * LANES), BF16),
        scratch_shapes=[pltpu.VMEM((N_SUPER, tm, LANES), F32)],
        compiler_params=_cparams("parallel"),
        name="s5_in_proj",
    )(a, w)


def _s5_param_kernel(lr_ref, li_ref, ldt_ref, btr_ref, bti_ref, cr_ref, ci_ref, d_ref,
                     kt_ref, er_ref, ei_ref, wr_ref, wi_ref, ac_ref):
    L = S5_CHUNK
    lr = lr_ref[...]
    li = li_ref[...]
    dt = jnp.exp(ldt_ref[...])
    mag = jnp.exp(lr * dt)
    ar = mag * jnp.cos(li * dt)
    ai = mag * jnp.sin(li * dt)
    den = lr * lr + li * li
    gr = ((ar - 1.0) * lr + ai * li) / den
    gi = (ai * lr - (ar - 1.0) * li) / den
    btr = btr_ref[...]
    bti = bti_ref[...]
    bbr = gr * btr - gi * bti
    bbi = gr * bti + gi * btr
    kf = lax.broadcasted_iota(jnp.int32, (L, 1), 0).astype(F32)
    pmag = jnp.exp(lr * dt * kf)
    pr = pmag * jnp.cos(li * dt * kf)
    pi = pmag * jnp.sin(li * dt * kf)
    cr = cr_ref[...]
    ci = ci_ref[...]
    pr3 = pr[:, None, :]
    pi3 = pi[:, None, :]
    wr = cr[None] * pr3 - ci[None] * pi3
    wi = cr[None] * pi3 + ci[None] * pr3
    dn = (((1,), (1,)), ((), ()))
    hp = lax.Precision.HIGHEST
    kt = (lax.dot_general(wr.reshape(L * S5_GROUP, S5_STATE), bbr, dn, precision=hp,
                          preferred_element_type=F32)
          - lax.dot_general(wi.reshape(L * S5_GROUP, S5_STATE), bbi, dn, precision=hp,
                            preferred_element_type=F32))
    row = lax.broadcasted_iota(jnp.int32, kt.shape, 0)
    col = lax.broadcasted_iota(jnp.int32, kt.shape, 1)
    kt_ref[...] = kt + jnp.where(row == col, d_ref[...], 0.0)
    er_ref[...] = pr3 * bbr[None] - pi3 * bbi[None]
    ei_ref[...] = pr3 * bbi[None] + pi3 * bbr[None]
    wr_ref[...] = wr * ar - wi * ai
    wi_ref[...] = wr * ai + wi * ar
    lastr = pr[L - 1:L, :]
    lasti = pi[L - 1:L, :]
    ac_ref[0:1, :] = lastr * ar - lasti * ai
    ac_ref[1:2, :] = lastr * ai + lasti * ar


def _s5_operators(lam_re, lam_im, log_dt, b_re, b_im, c_re, c_im, d_skip):
    G, P, H, L = S5_GROUPS, S5_STATE, S5_GROUP, S5_CHUNK
    per_g = lambda *shape: pl.BlockSpec((None,) + shape, lambda g: (g,) + (0,) * len(shape))
    kt, er, ei, wr, wi, ac = pl.pallas_call(
        _s5_param_kernel,
        grid=(G,),
        in_specs=[per_g(1, P), per_g(1, P), per_g(1, 1), per_g(H, P), per_g(H, P),
                  per_g(H, P), per_g(H, P), per_g(1, H)],
        out_specs=[per_g(L * H, H), per_g(L, H, P), per_g(L, H, P), per_g(L, H, P), per_g(L, H, P),
                   per_g(2, P)],
        out_shape=[jax.ShapeDtypeStruct((G, L * H, H), F32)]
        + [jax.ShapeDtypeStruct((G, L, H, P), F32)] * 4
        + [jax.ShapeDtypeStruct((G, 2, P), F32)],
        compiler_params=_cparams("parallel"),
        name="s5_operators",
    )(lam_re.reshape(G, 1, P), lam_im.reshape(G, 1, P), log_dt.reshape(G, 1, 1),
      b_re.transpose(0, 2, 1), b_im.transpose(0, 2, 1), c_re, c_im, d_skip.reshape(G, 1, H))

    ns, sg = N_SUPER, S5_SUPER
    eye = jnp.eye(sg, dtype=F32)
    ktg = kt.reshape(ns, sg, L, H, H)
    kd = jnp.einsum('SgtOI,gk->StgIkO', ktg, eye).reshape(ns, L, LANES, LANES).astype(BF16)
    lag = np.arange(L)[None, :] - np.arange(L)[:, None]
    causal = jnp.asarray(lag >= 0, BF16)[None, :, :, None, None]
    toe = kd[:, np.clip(lag, 0, None)] * causal
    tmat = toe.transpose(0, 1, 3, 2, 4).reshape(ns, L * LANES, L * LANES)
    grp_of_lane = np.arange(LANES) // H
    grp_mask = jnp.asarray(grp_of_lane[:, None] == np.arange(sg)[None, :], BF16)
    ecat = jnp.stack([er, ei], axis=3)[:, ::-1]
    eb = ecat.reshape(ns, sg, L, H, 2, P).transpose(0, 2, 1, 3, 4, 5).reshape(ns, L, LANES, LANES)
    bmat = eb.astype(BF16)[:, :, :, None, :] * grp_mask[None, None, :, :, None]
    bmat = bmat.reshape(ns, L * LANES, 2 * sg * P)
    wcat = jnp.stack([wr, -wi], axis=3).reshape(ns, sg, L, H, 2, P)
    wc = wcat.transpose(0, 4, 5, 2, 1, 3).reshape(ns, LANES, L, LANES)
    cmat = wc.astype(BF16)[:, None, :, :, :] * grp_mask.T[None, :, None, None, :]
    cmat = cmat.reshape(ns, 2 * sg * P, L * LANES)
    acr = ac[:, 0, :].reshape(ns, sg, P)
    aci = ac[:, 1, :].reshape(ns, sg, P)
    a_same = jnp.concatenate([acr, acr], axis=-1)
    a_swap = jnp.concatenate([-aci, aci], axis=-1)
    return tmat.astype(BF16), bmat.astype(BF16), cmat.astype(BF16), a_same, a_swap


def _s5_inc_kernel(u_ref, b_ref, o_ref):
    tr = u_ref.shape[0]
    inc = jnp.dot(u_ref[...], b_ref[...], preferred_element_type=F32)
    for g in range(S5_SUPER):
        o_ref[pl.ds(g, tr, stride=S5_SUPER), :] = inc[:, g * LANES:(g + 1) * LANES]


def _s5_scan_kernel(inc_ref, asame_ref, aswap_ref, x_ref, state_ref, *, cb):
    @pl.when(pl.program_id(0) == 0)
    def _():
        state_ref[...] = jnp.zeros_like(state_ref)

    nseq = state_ref.shape[0]
    a_same = asame_ref[...].reshape(nseq * S5_SUPER, LANES)
    a_swap = aswap_ref[...].reshape(nseq * S5_SUPER, LANES)

    def body(c, x):
        x_ref[:, c] = x.reshape(nseq, S5_SUPER, LANES)
        inc = inc_ref[:, c].reshape(nseq * S5_SUPER, LANES)
        return x * a_same + pltpu.roll(x, LANES // 2, 1) * a_swap + inc

    x0 = state_ref[...].reshape(nseq * S5_SUPER, LANES)
    state_ref[...] = lax.fori_loop(0, cb, body, x0).reshape(nseq, S5_SUPER, LANES)


def _s5_out_kernel(u_ref, x_ref, t_ref, c_ref, y_ref, ys_ref):
    tr = u_ref.shape[0]
    x = jnp.concatenate([x_ref[pl.ds(g, tr, stride=S5_SUPER), :] for g in range(S5_SUPER)], axis=-1)
    y = jnp.dot(u_ref[...], t_ref[...], preferred_element_type=F32)
    y += jnp.dot(x.astype(BF16), c_ref[...], preferred_element_type=F32)
    y = jax.nn.gelu(y)
    for l in range(S5_CHUNK):
        ys_ref[pl.ds(l, tr, stride=S5_CHUNK), :] = y[:, l * LANES:(l + 1) * LANES]
    y_ref[...] = ys_ref[...].astype(y_ref.dtype)


def _s5_core(uc, tmat, bmat, cmat, a_same, a_swap, batch, seq, tr=512):
    ns, rows, width = uc.shape
    L = S5_CHUNK
    n = rows * L
    n_chunks = seq // L
    sw = 2 * S5_SUPER * S5_STATE
    tr = min(tr, rows)
    inc = pl.pallas_call(
        _s5_inc_kernel,
        grid=(ns, rows // tr),
        in_specs=[pl.BlockSpec((None, tr, width), lambda s, r: (s, r, 0)),
                  pl.BlockSpec((None, width, sw), lambda s, r: (s, 0, 0))],
        out_specs=pl.BlockSpec((None, tr * S5_SUPER, LANES), lambda s, r: (s, r, 0)),
        out_shape=jax.ShapeDtypeStruct((ns, rows * S5_SUPER, LANES), F32),
        compiler_params=_cparams("parallel", "parallel"),
        name="s5_chunk_state",
    )(uc, bmat)
    nseq = ns * batch
    cb = min(32, n_chunks)
    tile = (S5_SUPER, LANES)
    bcast = lambda a: jnp.broadcast_to(a[:, None], (ns, batch) + tile).reshape((nseq,) + tile)
    seq_spec = pl.BlockSpec((nseq, cb) + tile, lambda c: (0, c, 0, 0))
    tab_spec = pl.BlockSpec((nseq,) + tile, lambda c: (0, 0, 0))
    xprev = pl.pallas_call(
        functools.partial(_s5_scan_kernel, cb=cb),
        grid=(n_chunks // cb,),
        in_specs=[seq_spec, tab_spec, tab_spec],
        out_specs=seq_spec,
        out_shape=jax.ShapeDtypeStruct((nseq, n_chunks) + tile, F32),
        scratch_shapes=[pltpu.VMEM((nseq,) + tile, F32)],
        compiler_params=_cparams("arbitrary"),
        name="s5_chunk_scan",
    )(inc.reshape((nseq, n_chunks) + tile), bcast(a_same), bcast(a_swap))
    xprev = xprev.reshape(ns, rows * S5_SUPER, LANES)
    return pl.pallas_call(
        _s5_out_kernel,
        grid=(ns, rows // tr),
        in_specs=[pl.BlockSpec((None, tr, width), lambda s, r: (s, r, 0)),
                  pl.BlockSpec((None, tr * S5_SUPER, LANES), lambda s, r: (s, r, 0)),
                  pl.BlockSpec((None, width, width), lambda s, r: (s, 0, 0)),
                  pl.BlockSpec((None, sw, width), lambda s, r: (s, 0, 0))],
        out_specs=pl.BlockSpec((None, tr * L, LANES), lambda s, r: (s, r, 0)),
        out_shape=jax.ShapeDtypeStruct((ns, n, LANES), BF16),
        scratch_shapes=[pltpu.VMEM((tr * L, LANES), F32)],
        compiler_params=_cparams("parallel", "parallel"),
        name="s5_chunk_out",
    )(uc, xprev, tmat, cmat)


def _glu_router_kernel(y_ref, w_ref, res_ref, g_ref, wr_ref, h_ref, hn_ref, route_ref):
    d = res_ref.shape[1]
    y = jnp.concatenate([y_ref[g] for g in range(N_SUPER)], axis=-1)
    z = jnp.dot(y, w_ref[...], preferred_element_type=F32)
    h = res_ref[...] + z[:, :d] * jax.nn.sigmoid(z[:, d:])
    h_ref[...] = h
    hn = _rms(h, g_ref[...])
    hn_ref[...] = hn.astype(hn_ref.dtype)
    hi = hn.astype(BF16)
    lo = (hn - hi.astype(F32)).astype(BF16)
    wr = wr_ref[...]
    first = jnp.dot(hi, wr, preferred_element_type=F32)
    logits = (first[:, :LANES] + first[:, LANES:]) + jnp.dot(lo, wr[:, :LANES], preferred_element_type=F32)
    lane = lax.broadcasted_iota(jnp.int32, logits.shape, 1)
    logits = jnp.where(lane < N_EXPERTS, logits, -jnp.inf)
    m1 = jnp.max(logits, axis=-1, keepdims=True)
    i1 = jnp.min(jnp.where(logits == m1, lane, LANES), axis=-1, keepdims=True)
    rest = jnp.where(lane == i1, -jnp.inf, logits)
    m2 = jnp.max(rest, axis=-1, keepdims=True)
    i2 = jnp.min(jnp.where(rest == m2, lane, LANES), axis=-1, keepdims=True)
    e2 = jnp.exp(m2 - m1)
    g1 = 1.0 / (1.0 + e2)
    route_ref[...] = (jnp.where(lane == 0, i1.astype(F32), 0.0) + jnp.where(lane == 1, i2.astype(F32), 0.0)
                      + jnp.where(lane == 2, g1, 0.0) + jnp.where(lane == 3, e2 * g1, 0.0))


def _glu_router(y, w_glu, res, g, w_router_pad, tm=512):
    ns, n, _ = y.shape
    d = res.shape[1]
    return pl.pallas_call(
        _glu_router_kernel,
        grid=(n // tm,),
        in_specs=[pl.BlockSpec((ns, tm, LANES), lambda i: (0, i, 0)),
                  pl.BlockSpec(w_glu.shape, lambda i: (0, 0)),
                  pl.BlockSpec((tm, d), lambda i: (i, 0)),
                  pl.BlockSpec((1, d), lambda i: (0, 0)),
                  pl.BlockSpec(w_router_pad.shape, lambda i: (0, 0))],
        out_specs=[pl.BlockSpec((tm, d), lambda i: (i, 0)),
                   pl.BlockSpec((tm, d), lambda i: (i, 0)),
                   pl.BlockSpec((tm, LANES), lambda i: (i, 0))],
        out_shape=[jax.ShapeDtypeStruct((n, d), F32), jax.ShapeDtypeStruct((n, d), F32),
                   jax.ShapeDtypeStruct((n, LANES), F32)],
        compiler_params=_cparams("parallel"),
        name="s5_glu_router",
    )(y, w_glu, res, g, w_router_pad)


MOE_TM = 512
MOE_TF = 1792
DISPATCH_BLOCK = 512
COMBINE_TM = 512


def _moe_plan(route, n_tok):
    ne, tm = N_EXPERTS, MOE_TM
    n_tiles = (2 * n_tok) // tm + ne
    eid = route[:, :2].astype(jnp.int32).reshape(-1)
    onehot = (eid[:, None] == jnp.arange(ne, dtype=jnp.int32)[None, :]).astype(jnp.int32)
    csum = jnp.cumsum(onehot, axis=0)
    rank = jnp.sum((csum - onehot) * onehot, axis=1)
    cnt = csum[-1]
    padded = ((cnt + tm - 1) // tm) * tm
    ends = jnp.cumsum(padded)
    off = ends - padded
    pos = (jnp.sum(off[None, :] * onehot, axis=1) + rank).astype(jnp.int32)
    tile_end = ends // tm
    n_valid = tile_end[-1:].astype(jnp.int32)
    tiles = jnp.arange(n_tiles, dtype=jnp.int32)
    tile_expert = jnp.minimum(jnp.sum(tiles[:, None] >= tile_end[None, :], axis=1), ne - 1).astype(jnp.int32)
    pad_start = (off + cnt).astype(jnp.int32)
    pad_len = (padded - cnt).astype(jnp.int32)
    return pos, tile_expert, n_valid, pad_start, pad_len, n_tiles


def _dispatch_kernel(pos_ref, pstart_ref, plen_ref, hn_ref, xs_ref, zero_ref, sem, *, n_tok):
    nb = n_tok // DISPATCH_BLOCK
    b = pl.program_id(0)

    def row_copy(r, k):
        dst = pos_ref[2 * (b * DISPATCH_BLOCK + r) + k]
        return pltpu.make_async_copy(hn_ref.at[pl.ds(r, 1)], xs_ref.at[pl.ds(dst, 1)], sem)

    def issue(r, _):
        row_copy(r, 0).start()
        row_copy(r, 1).start()
        return 0
    lax.fori_loop(0, DISPATCH_BLOCK, issue, 0, unroll=8)

    def drain(r, _):
        row_copy(r, 0).wait()
        row_copy(r, 1).wait()
        return 0
    lax.fori_loop(0, DISPATCH_BLOCK, drain, 0, unroll=8)

    @pl.when(b == nb - 1)
    def _():
        zero_ref[...] = jnp.zeros_like(zero_ref)

        def pad_copy(e, r):
            return pltpu.make_async_copy(zero_ref.at[pl.ds(0, 1)], xs_ref.at[pl.ds(pstart_ref[e] + r, 1)], sem)

        for e in range(N_EXPERTS):
            def pbody(r, _, e=e):
                pad_copy(e, r).start()
                return 0
            lax.fori_loop(0, plen_ref[e], pbody, 0)
        for e in range(N_EXPERTS):
            def wbody(r, _, e=e):
                pad_copy(e, r).wait()
                return 0
            lax.fori_loop(0, plen_ref[e], wbody, 0)


def _dispatch(hn, pos, pad_start, pad_len, n_rows):
    n_tok, d = hn.shape
    return pl.pallas_call(
        functools.partial(_dispatch_kernel, n_tok=n_tok),
        grid_spec=pltpu.PrefetchScalarGridSpec(
            num_scalar_prefetch=3,
            grid=(n_tok // DISPATCH_BLOCK,),
            in_specs=[pl.BlockSpec((DISPATCH_BLOCK, d), lambda b, pos, ps, plen: (b, 0))],
            out_specs=pl.BlockSpec(memory_space=pl.ANY),
            scratch_shapes=[pltpu.VMEM((8, d), F32), pltpu.SemaphoreType.DMA]),
        out_shape=jax.ShapeDtypeStruct((n_rows, d), F32),
        compiler_params=_cparams("arbitrary"),
        name="moe_dispatch",
    )(pos, pad_start, pad_len, hn)


def _moe_kernel(te_ref, nv_ref, x_ref, wg_ref, wu_ref, wd_ref, y_ref, xb_ref):
    i = pl.program_id(0)
    k = pl.program_id(1)

    @pl.when(i < nv_ref[0])
    def _():
        @pl.when(k == 0)
        def _():
            xb_ref[...] = x_ref[...].astype(BF16)

        hn = xb_ref[...]
        gate = jnp.dot(hn, wg_ref[...], preferred_element_type=F32)
        up = jnp.dot(hn, wu_ref[...], preferred_element_type=F32)
        act = (jax.nn.silu(gate) * up).astype(BF16)
        part = jnp.dot(act, wd_ref[...], preferred_element_type=F32)

        @pl.when(k == 0)
        def _():
            y_ref[...] = part

        @pl.when(k > 0)
        def _():
            y_ref[...] += part


def _moe_experts(xs, tile_expert, n_valid, wg, wu, wd, n_tiles):
    tm, tf = MOE_TM, MOE_TF
    d = xs.shape[1]
    f = wg.shape[2]
    nk = f // tf

    def row_map(i, k, te, nv):
        return (jnp.minimum(i, nv[0] - 1), 0)

    def kk(i, k, nv):
        return jnp.where(i < nv[0], k, nk - 1)

    def w_in_map(i, k, te, nv):
        return (te[jnp.minimum(i, nv[0] - 1)], 0, kk(i, k, nv))

    def w_out_map(i, k, te, nv):
        return (te[jnp.minimum(i, nv[0] - 1)], kk(i, k, nv), 0)

    return pl.pallas_call(
        _moe_kernel,
        grid_spec=pltpu.PrefetchScalarGridSpec(
            num_scalar_prefetch=2,
            grid=(n_tiles, nk),
            in_specs=[pl.BlockSpec((tm, d), row_map),
                      pl.BlockSpec((None, d, tf), w_in_map),
                      pl.BlockSpec((None, d, tf), w_in_map),
                      pl.BlockSpec((None, tf, d), w_out_map)],
            out_specs=pl.BlockSpec((tm, d), row_map),
            scratch_shapes=[pltpu.VMEM((tm, d), BF16)]),
        out_shape=jax.ShapeDtypeStruct(xs.shape, F32),
        compiler_params=_cparams("arbitrary", "arbitrary"),
        name="moe_experts",
    )(tile_expert, n_valid, xs, wg, wu, wd)


def _combine_kernel(pos_ref, y_ref, h_ref, route_ref, g_ref, out_ref, buf_ref, sem, *, tm, n_steps):
    i = pl.program_id(0)
    slot = i % 2

    def row_copy(step, sl, r, k):
        t = step * tm + r
        return pltpu.make_async_copy(y_ref.at[pl.ds(pos_ref[2 * t + k], 1)],
                                     buf_ref.at[sl, k, pl.ds(r, 1)], sem.at[sl])

    def issue(step, sl):
        def body(r, _):
            row_copy(step, sl, r, 0).start()
            row_copy(step, sl, r, 1).start()
            return 0
        lax.fori_loop(0, tm, body, 0, unroll=8)

    @pl.when(i == 0)
    def _():
        issue(0, 0)

    @pl.when(i + 1 < n_steps)
    def _():
        issue(i + 1, 1 - slot)

    def wbody(r, _):
        row_copy(i, slot, r, 0).wait()
        row_copy(i, slot, r, 1).wait()
        return 0
    lax.fori_loop(0, tm, wbody, 0, unroll=8)

    route = route_ref[...]
    out = h_ref[...] + route[:, 2:3] * buf_ref[slot, 0] + route[:, 3:4] * buf_ref[slot, 1]
    out_ref[...] = _rms(out, g_ref[...])


def _combine(y, pos, h, route, g):
    n_tok, d = h.shape
    tm = COMBINE_TM
    n_steps = n_tok // tm
    return pl.pallas_call(
        functools.partial(_combine_kernel, tm=tm, n_steps=n_steps),
        grid_spec=pltpu.PrefetchScalarGridSpec(
            num_scalar_prefetch=1,
            grid=(n_steps,),
            in_specs=[pl.BlockSpec(memory_space=pl.ANY),
                      pl.BlockSpec((tm, d), lambda i, pos: (i, 0)),
                      pl.BlockSpec((tm, LANES), lambda i, pos: (i, 0)),
                      pl.BlockSpec((1, d), lambda i, pos: (0, 0))],
            out_specs=pl.BlockSpec((tm, d), lambda i, pos: (i, 0)),
            scratch_shapes=[pltpu.VMEM((2, 2, tm, d), F32), pltpu.SemaphoreType.DMA((2,))]),
        out_shape=jax.ShapeDtypeStruct((n_tok, d), F32),
        compiler_params=_cparams("arbitrary"),
        name="moe_combine",
    )(pos, y, h, route, g)


def _moe_layer(h, hn, route, wg, wu, wd, g):
    n_tok = h.shape[0]
    pos, tile_expert, n_valid, pad_start, pad_len, n_tiles = _moe_plan(route, n_tok)
    xs = _dispatch(hn, pos, pad_start, pad_len, n_tiles * MOE_TM)
    y = _moe_experts(xs, tile_expert, n_valid, wg, wu, wd, n_tiles)
    return _combine(y, pos, h, route, g)


def _mla_layer(h, positions, mix_g, w_in, q_norm, w_uq, kv_norm, w_ukv, w_o, ffn_g):
    batch, seq = positions.shape
    win_ext, wq_ext, wkv_ext = _mla_weights(w_in, w_uq, w_ukv)
    pos = positions.astype(F32).reshape(batch * seq, 1)
    q, k, v = _mla_proj(h, pos, mix_g.reshape(1, -1), win_ext, q_norm.reshape(1, -1),
                        kv_norm.reshape(1, -1), wq_ext, wkv_ext)
    o = _attention(q, k, v, batch, seq)
    wo = w_o.reshape(MLA_HEADS, V_DIM, D_MODEL)
    wo_ext = jnp.concatenate([wo, jnp.zeros((MLA_HEADS, HEAD_PAD - V_DIM, D_MODEL), wo.dtype)], axis=1)
    wo_ext = wo_ext.reshape(MLA_HEADS * HEAD_PAD, D_MODEL).astype(BF16)
    return _mm_res_norm(o, wo_ext, h, ffn_g.reshape(1, -1))


def _s5_layer(h, hn, batch, seq, w_in, lam_re, lam_im, log_dt, b_re, b_im, c_re, c_im, d_skip, w_glu,
              ffn_g, w_router):
    u = _mm_split(hn, w_in.astype(BF16))
    tmat, bmat, cmat, a_same, a_swap = _s5_operators(lam_re, lam_im, log_dt, b_re, b_im, c_re, c_im, d_skip)
    y = _s5_core(u, tmat, bmat, cmat, a_same, a_swap, batch, seq)
    wr_pad = jnp.concatenate([w_router, jnp.zeros((D_MODEL, LANES - N_EXPERTS), w_router.dtype)], axis=1)
    wr_hi = wr_pad.astype(BF16)
    wr_lo = (wr_pad - wr_hi.astype(F32)).astype(BF16)
    return _glu_router(y, w_glu.astype(BF16), h, ffn_g.reshape(1, -1), jnp.concatenate([wr_hi, wr_lo], axis=1))


def kernel(x, positions, mix_norm, ffn_norm, final_norm, mla_w_in, mla_q_norm, mla_w_uq, mla_kv_norm, mla_w_ukv, mla_w_o, ffn_w_gate, ffn_w_up, ffn_w_down, s5_w_in, s5_lambda_re, s5_lambda_im, s5_log_dt, s5_b_re, s5_b_im, s5_c_re, s5_c_im, s5_d, s5_w_glu, moe_w_router, moe_w_gate, moe_w_up, moe_w_down):
    batch, seq, d = x.shape
    h0 = x.reshape(batch * seq, d)
    h1, hn1 = _mla_layer(h0, positions, mix_norm[0], mla_w_in[0], mla_q_norm[0], mla_w_uq[0],
                         mla_kv_norm[0], mla_w_ukv[0], mla_w_o[0], ffn_norm[0])
    h2, hn2 = _ffn(hn1, h1, ffn_w_gate[0].astype(BF16), ffn_w_up[0].astype(BF16),
                   ffn_w_down[0].astype(BF16), mix_norm[1].reshape(1, -1))
    h3, hn3, route = _s5_layer(h2, hn2, batch, seq, s5_w_in[0], s5_lambda_re[0], s5_lambda_im[0],
                              s5_log_dt[0], s5_b_re[0], s5_b_im[0], s5_c_re[0], s5_c_im[0], s5_d[0],
                              s5_w_glu[0], ffn_norm[1], moe_w_router[0])
    out = _moe_layer(h3, hn3, route, moe_w_gate[0].astype(BF16), moe_w_up[0].astype(BF16),
                     moe_w_down[0].astype(BF16), final_norm.reshape(1, -1))
    return out.reshape(batch, seq, d)
```

```python
import functools
import math

import jax
import jax.numpy as jnp
import numpy as np
from jax import lax
from jax.experimental import pallas as pl
from jax.experimental.pallas import tpu as pltpu

F32 = jnp.float32
BF16 = jnp.bfloat16

D_MODEL = 1024
MLA_HEADS = 16
QK_NOPE = 64
QK_ROPE = 32
V_DIM = 64
Q_LORA = 384
KV_LORA = 256
ROPE_THETA = 10000.0
HEAD_PAD = 128
ATTN_HEADS_PER_STEP = 4
S5_GROUP = 16
S5_GROUPS = D_MODEL // S5_GROUP
S5_STATE = 64
S5_CHUNK = 16
S5_SUPER = 8
N_SUPER = S5_GROUPS // S5_SUPER
N_EXPERTS = 8
EPS = 1e-6
NEG_INF = -1e30
LANES = 128
VMEM_LIMIT = 56 * 1024 * 1024


def _cparams(*sem):
    return pltpu.CompilerParams(dimension_semantics=sem, vmem_limit_bytes=VMEM_LIMIT)


def _rms(x, g):
    return x * lax.rsqrt(jnp.mean(x * x, axis=-1, keepdims=True) + EPS) * g


def _mla_proj_kernel(x_ref, pos_ref, g_ref, win_ref, qn_ref, kvn_ref, wq_ref, wkv_ref, tab_ref,
                     q_ref, k_ref, v_ref):
    hn = _rms(x_ref[...], g_ref[...]).astype(BF16)
    proj = jnp.dot(hn, win_ref[...], preferred_element_type=F32)
    cq = _rms(proj[:, :Q_LORA], qn_ref[...]).astype(BF16)
    ckv = _rms(proj[:, Q_LORA:Q_LORA + KV_LORA], kvn_ref[...]).astype(BF16)
    ka = proj[:, 640:768]
    kb = proj[:, 768:896]
    tab = tab_ref[...]
    ang = pos_ref[...] * tab[0:1, :]
    cos = jnp.cos(ang)
    sin = jnp.sin(ang)
    kp = ka * (cos * tab[1:2, :]) + kb * (sin * tab[2:3, :])
    tq = cos * tab[3:4, :] + sin * tab[4:5, :]
    q = jnp.dot(cq, wq_ref[...], preferred_element_type=F32)
    kv = jnp.dot(ckv, wkv_ref[...], preferred_element_type=F32)
    for h in range(MLA_HEADS):
        sl = slice(h * HEAD_PAD, (h + 1) * HEAD_PAD)
        q_ref[:, sl] = (q[:, sl] * tq).astype(BF16)
        k_ref[:, sl] = (kv[:, sl] + kp).astype(BF16)
    v = kv[:, MLA_HEADS * HEAD_PAD:]
    vlane = lax.broadcasted_iota(jnp.int32, v.shape, 1) % HEAD_PAD
    v_ref[...] = jnp.where(vlane == V_DIM, 1.0, v).astype(BF16)


def _mla_tables():
    inv_freq = ROPE_THETA ** (-np.arange(0, QK_ROPE, 2, dtype=np.float32) / QK_ROPE)
    inv_freq = inv_freq.astype(np.float32)
    half = QK_ROPE // 2
    tab = np.zeros((8, LANES), np.float32)
    scale = math.log2(math.e) / math.sqrt(QK_NOPE + QK_ROPE)
    tab[3, :QK_NOPE] = scale
    kc = [1.0, -1.0, 1.0, 1.0]
    ks = [-1.0, 1.0, 1.0, 1.0]
    for grp in range(4):
        sl = slice(QK_NOPE + grp * half, QK_NOPE + (grp + 1) * half)
        tab[0, sl] = inv_freq
        tab[1, sl] = kc[grp]
        tab[2, sl] = ks[grp]
        if grp % 2 == 0:
            tab[3, sl] = scale
        else:
            tab[4, sl] = scale
    return jnp.asarray(tab)


def _mla_weights(w_in, w_uq, w_ukv):
    half = QK_ROPE // 2
    base = Q_LORA + KV_LORA
    k1 = w_in[:, base:base + half]
    k2 = w_in[:, base + half:base + QK_ROPE]
    z64 = jnp.zeros((D_MODEL, QK_NOPE), w_in.dtype)
    win_ext = jnp.concatenate([w_in[:, :base], z64, k1, k1, k2, k2, z64, k2, k2, k1, k1], axis=1)
    wq = w_uq.reshape(Q_LORA, MLA_HEADS, QK_NOPE + QK_ROPE)
    q1 = wq[:, :, QK_NOPE:QK_NOPE + half]
    q2 = wq[:, :, QK_NOPE + half:]
    wq_ext = jnp.concatenate([wq[:, :, :QK_NOPE], q1, q2, q2, q1], axis=2)
    wq_ext = wq_ext.reshape(Q_LORA, MLA_HEADS * HEAD_PAD)
    wkv = w_ukv.reshape(KV_LORA, MLA_HEADS, QK_NOPE + V_DIM)
    zk = jnp.zeros((KV_LORA, MLA_HEADS, HEAD_PAD - QK_NOPE), w_ukv.dtype)
    zv = jnp.zeros((KV_LORA, MLA_HEADS, HEAD_PAD - V_DIM), w_ukv.dtype)
    wk_ext = jnp.concatenate([wkv[:, :, :QK_NOPE], zk], axis=2).reshape(KV_LORA, -1)
    wv_ext = jnp.concatenate([wkv[:, :, QK_NOPE:], zv], axis=2).reshape(KV_LORA, -1)
    wkv_ext = jnp.concatenate([wk_ext, wv_ext], axis=1)
    return win_ext.astype(BF16), wq_ext.astype(BF16), wkv_ext.astype(BF16)


def _mla_proj(x2d, pos, g, win_ext, qn, kvn, wq_ext, wkv_ext, tm=512):
    n = x2d.shape[0]
    hw = MLA_HEADS * HEAD_PAD
    full = lambda a: pl.BlockSpec(a.shape, lambda i: (0,) * a.ndim)
    tab = _mla_tables()
    return pl.pallas_call(
        _mla_proj_kernel,
        grid=(n // tm,),
        in_specs=[pl.BlockSpec((tm, D_MODEL), lambda i: (i, 0)),
                  pl.BlockSpec((tm, 1), lambda i: (i, 0)),
                  full(g), full(win_ext), full(qn), full(kvn), full(wq_ext), full(wkv_ext), full(tab)],
        out_specs=[pl.BlockSpec((tm, hw), lambda i: (i, 0))] * 3,
        out_shape=[jax.ShapeDtypeStruct((n, hw), BF16)] * 3,
        compiler_params=_cparams("parallel"),
        name="mla_proj",
    )(x2d, pos, g, win_ext, qn, kvn, wq_ext, wkv_ext, tab)


def _attn_kernel(q_ref, k_ref, v_ref, o_ref, s0_ref, s1_ref, p0_ref, p1_ref, *, blk):
    i = pl.program_id(2)
    heads = range(ATTN_HEADS_PER_STEP)
    lanes = [slice(h * HEAD_PAD, (h + 1) * HEAD_PAD) for h in heads]
    q = [q_ref[:, lanes[h]] for h in heads]

    def stage_a(j, s_ref):
        off = pl.multiple_of(j * blk, blk)
        for h in heads:
            s_ref[h] = lax.dot_general(q[h], k_ref[pl.ds(off, blk), lanes[h]], (((1,), (1,)), ((), ())),
                                       preferred_element_type=F32)

    def stage_b(state, s_ref, p_ref, masked=False):
        out = []
        for h in heads:
            m, _, acc = state[h]
            s = s_ref[h]
            if masked:
                row = lax.broadcasted_iota(jnp.int32, (blk, blk), 0)
                col = lax.broadcasted_iota(jnp.int32, (blk, blk), 1)
                s = jnp.where(col <= row, s, NEG_INF)
            m_new = jnp.maximum(m, jnp.max(s, axis=-1, keepdims=True))
            p_ref[h] = jnp.exp2((s - m_new).astype(BF16))
            out.append((m_new, jnp.exp2(m - m_new), acc))
        return tuple(out)

    def stage_c(j, state, p_ref):
        off = pl.multiple_of(j * blk, blk)
        out = []
        for h in heads:
            m, alpha, acc = state[h]
            pv = jnp.dot(p_ref[h], v_ref[pl.ds(off, blk), lanes[h]], preferred_element_type=F32)
            out.append((m, alpha, alpha * acc + pv))
        return tuple(out)

    def steady(j, state, even):
        s_j, s_n, p_j, p_n = (s0_ref, s1_ref, p0_ref, p1_ref) if even else (s1_ref, s0_ref, p1_ref, p0_ref)
        state = stage_c(j, state, p_j)
        stage_a(j + 2, s_j)
        return stage_b(state, s_n, p_n)

    def drain(state, even_last):
        s_i, p_i, p_prev = (s0_ref, p0_ref, p1_ref) if even_last else (s1_ref, p1_ref, p0_ref)
        state = stage_c(i - 1, state, p_prev)
        state = stage_b(state, s_i, p_i, masked=True)
        finish(stage_c(i, state, p_i))

    def finish(state):
        for h in heads:
            acc = state[h][2]
            o_ref[:, lanes[h]] = (acc / acc[:, V_DIM:V_DIM + 1]).astype(o_ref.dtype)

    init = tuple((jnp.full((blk, 1), NEG_INF, F32), jnp.ones((blk, 1), F32), jnp.zeros((blk, HEAD_PAD), F32))
                 for _ in heads)
    stage_a(0, s0_ref)

    @pl.when(i == 0)
    def _():
        finish(stage_c(0, stage_b(init, s0_ref, p0_ref, masked=True), p0_ref))

    @pl.when(i > 0)
    def _():
        stage_a(1, s1_ref)
        state = stage_b(init, s0_ref, p0_ref)

        def pair(jj, st):
            return steady(2 * jj + 1, steady(2 * jj, st, even=True), even=False)
        state = lax.fori_loop(0, (i - 1) // 2, pair, state)

        @pl.when(i % 2 == 1)
        def _():
            drain(state, even_last=False)

        @pl.when(i % 2 == 0)
        def _():
            drain(steady(i - 2, state, even=True), even_last=True)


def _attention(q, k, v, batch, seq, blk=512):
    n, hw = q.shape
    nq = seq // blk
    nh = ATTN_HEADS_PER_STEP
    width = nh * HEAD_PAD
    return pl.pallas_call(
        functools.partial(_attn_kernel, blk=blk),
        grid=(batch, MLA_HEADS // nh, nq),
        in_specs=[pl.BlockSpec((blk, width), lambda b, h, i: (b * nq + i, h)),
                  pl.BlockSpec((seq, width), lambda b, h, i: (b, h), pipeline_mode=pl.Buffered(1)),
                  pl.BlockSpec((seq, width), lambda b, h, i: (b, h), pipeline_mode=pl.Buffered(1))],
        out_specs=pl.BlockSpec((blk, width), lambda b, h, i: (b * nq + i, h)),
        out_shape=jax.ShapeDtypeStruct((n, hw), BF16),
        scratch_shapes=[pltpu.VMEM((nh, blk, blk), F32), pltpu.VMEM((nh, blk, blk), F32),
                        pltpu.VMEM((nh, blk, blk), BF16), pltpu.VMEM((nh, blk, blk), BF16)],
        compiler_params=_cparams("parallel", "parallel", "arbitrary"),
        name="mla_attention",
    )(q, k, v)


def _mm_res_norm_kernel(a_ref, w_ref, res_ref, g_ref, h_ref, hn_ref):
    h = res_ref[...] + jnp.dot(a_ref[...], w_ref[...], preferred_element_type=F32)
    h_ref[...] = h
    hn_ref[...] = _rms(h, g_ref[...]).astype(hn_ref.dtype)


def _mm_res_norm(a, w, res, g, tm=512):
    n, k = a.shape
    d = w.shape[1]
    return pl.pallas_call(
        _mm_res_norm_kernel,
        grid=(n // tm,),
        in_specs=[pl.BlockSpec((tm, k), lambda i: (i, 0)),
                  pl.BlockSpec((k, d), lambda i: (0, 0)),
                  pl.BlockSpec((tm, d), lambda i: (i, 0)),
                  pl.BlockSpec((1, d), lambda i: (0, 0))],
        out_specs=[pl.BlockSpec((tm, d), lambda i: (i, 0))] * 2,
        out_shape=[jax.ShapeDtypeStruct((n, d), F32), jax.ShapeDtypeStruct((n, d), BF16)],
        compiler_params=_cparams("parallel"),
        name="proj_residual_norm",
    )(a, w, res, g)


def _ffn_kernel(hn_ref, res_ref, wg_ref, wu_ref, wd_ref, g2_ref, out_ref, outn_ref):
    kk = pl.program_id(1)

    @pl.when(kk == 0)
    def _():
        out_ref[...] = res_ref[...]

    hn = hn_ref[...]
    gate = jnp.dot(hn, wg_ref[...], preferred_element_type=F32)
    up = jnp.dot(hn, wu_ref[...], preferred_element_type=F32)
    act = (jax.nn.silu(gate) * up).astype(BF16)
    out_ref[...] += jnp.dot(act, wd_ref[...], preferred_element_type=F32)

    @pl.when(kk == pl.num_programs(1) - 1)
    def _():
        outn_ref[...] = _rms(out_ref[...], g2_ref[...]).astype(outn_ref.dtype)


def _ffn(hn, res, wg, wu, wd, g2, tm=512):
    n, d = hn.shape
    f = wg.shape[1]
    tf = f
    resident = pl.Buffered(1)
    return pl.pallas_call(
        _ffn_kernel,
        grid=(n // tm, f // tf),
        in_specs=[pl.BlockSpec((tm, d), lambda i, k: (i, 0)),
                  pl.BlockSpec((tm, d), lambda i, k: (i, 0)),
                  pl.BlockSpec((d, tf), lambda i, k: (0, k), pipeline_mode=resident),
                  pl.BlockSpec((d, tf), lambda i, k: (0, k), pipeline_mode=resident),
                  pl.BlockSpec((tf, d), lambda i, k: (k, 0), pipeline_mode=resident),
                  pl.BlockSpec((1, d), lambda i, k: (0, 0))],
        out_specs=[pl.BlockSpec((tm, d), lambda i, k: (i, 0))] * 2,
        out_shape=[jax.ShapeDtypeStruct((n, d), F32), jax.ShapeDtypeStruct((n, d), BF16)],
        compiler_params=_cparams("parallel", "arbitrary"),
        name="swiglu_dense",
    )(hn, res, wg, wu, wd, g2)


def _mm_split_kernel(a_ref, w_ref, o_ref, r_ref):
    tm = a_ref.shape[0]
    r = jnp.dot(a_ref[...], w_ref[...], preferred_element_type=F32)
    for g in range(N_SUPER):
        r_ref[g] = r[:, g * LANES:(g + 1) * LANES]
    for g in range(N_SUPER):
        for l in range(S5_CHUNK):
            piece = r_ref[g, pl.ds(l, tm // S5_CHUNK, stride=S5_CHUNK), :]
            o_ref[g, :, l * LANES:(l + 1) * LANES] = piece.astype(o_ref.dtype)


def _mm_split(a, w, tm=512):
    n, k = a.shape
    return pl.pallas_call(
        _mm_split_kernel,
        grid=(n // tm,),
        in_specs=[pl.BlockSpec((tm, k), lambda i: (i, 0)),
                  pl.BlockSpec(w.shape, lambda i: (0, 0))],
        out_specs=pl.BlockSpec((N_SUPER, tm // S5_CHUNK, S5_CHUNK * LANES), lambda i: (0, i, 0)),
        out_shape=jax.ShapeDtypeStruct((N_SUPER, n // S5_CHUNK, S5_CHUNK * LANES), BF16),
        scratch_shapes=[pltpu.VMEM((N_SUPER, tm, LANES), F32)],
        compiler_params=_cparams("parallel"),
        name="s5_in_proj",
    )(a, w)


def _s5_param_kernel(lr_ref, li_ref, ldt_ref, btr_ref, bti_ref, cr_ref, ci_ref, d_ref,
                     kt_ref, er_ref, ei_ref, wr_ref, wi_ref, ac_ref):
    L = S5_CHUNK
    lr = lr_ref[...]
    li = li_ref[...]
    dt = jnp.exp(ldt_ref[...])
    mag = jnp.exp(lr * dt)
    ar = mag * jnp.cos(li * dt)
    ai = mag * jnp.sin(li * dt)
    den = lr * lr + li * li
    gr = ((ar - 1.0) * lr + ai * li) / den
    gi = (ai * lr - (ar - 1.0) * li) / den
    btr = btr_ref[...]
    bti = bti_ref[...]
    bbr = gr * btr - gi * bti
    bbi = gr * bti + gi * btr
    kf = lax.broadcasted_iota(jnp.int32, (L, 1), 0).astype(F32)
    pmag = jnp.exp(lr * dt * kf)
    pr = pmag * jnp.cos(li * dt * kf)
    pi = pmag * jnp.sin(li * dt * kf)
    cr = cr_ref[...]
    ci = ci_ref[...]
    pr3 = pr[:, None, :]
    pi3 = pi[:, None, :]
    wr = cr[None] * pr3 - ci[None] * pi3
    wi = cr[None] * pi3 + ci[None] * pr3
    dn = (((1,), (1,)), ((), ()))
    hp = lax.Precision.HIGHEST
    kt = (lax.dot_general(wr.reshape(L * S5_GROUP, S5_STATE), bbr, dn, precision=hp,
                          preferred_element_type=F32)
          - lax.dot_general(wi.reshape(L * S5_GROUP, S5_STATE), bbi, dn, precision=hp,
                            preferred_element_type=F32))
    row = lax.broadcasted_iota(jnp.int32, kt.shape, 0)
    col = lax.broadcasted_iota(jnp.int32, kt.shape, 1)
    kt_ref[...] = kt + jnp.where(row == col, d_ref[...], 0.0)
    er_ref[...] = pr3 * bbr[None] - pi3 * bbi[None]
    ei_ref[...] = pr3 * bbi[None] + pi3 * bbr[None]
    wr_ref[...] = wr * ar - wi * ai
    wi_ref[...] = wr * ai + wi * ar
    lastr = pr[L - 1:L, :]
    lasti = pi[L - 1:L, :]
    ac_ref[0:1, :] = lastr * ar - lasti * ai
    ac_ref[1:2, :] = lastr * ai + lasti * ar


def _s5_operators(lam_re, lam_im, log_dt, b_re, b_im, c_re, c_im, d_skip):
    G, P, H, L = S5_GROUPS, S5_STATE, S5_GROUP, S5_CHUNK
    per_g = lambda *shape: pl.BlockSpec((None,) + shape, lambda g: (g,) + (0,) * len(shape))
    kt, er, ei, wr, wi, ac = pl.pallas_call(
        _s5_param_kernel,
        grid=(G,),
        in_specs=[per_g(1, P), per_g(1, P), per_g(1, 1), per_g(H, P), per_g(H, P),
                  per_g(H, P), per_g(H, P), per_g(1, H)],
        out_specs=[per_g(L * H, H), per_g(L, H, P), per_g(L, H, P), per_g(L, H, P), per_g(L, H, P),
                   per_g(2, P)],
        out_shape=[jax.ShapeDtypeStruct((G, L * H, H), F32)]
        + [jax.ShapeDtypeStruct((G, L, H, P), F32)] * 4
        + [jax.ShapeDtypeStruct((G, 2, P), F32)],
        compiler_params=_cparams("parallel"),
        name="s5_operators",
    )(lam_re.reshape(G, 1, P), lam_im.reshape(G, 1, P), log_dt.reshape(G, 1, 1),
      b_re.transpose(0, 2, 1), b_im.transpose(0, 2, 1), c_re, c_im, d_skip.reshape(G, 1, H))

    ns, sg = N_SUPER, S5_SUPER
    eye = jnp.eye(sg, dtype=F32)
    ktg = kt.reshape(ns, sg, L, H, H)
    kd = jnp.einsum('SgtOI,gk->StgIkO', ktg, eye).reshape(ns, L, LANES, LANES)
    ecat = jnp.stack([er, ei], axis=3)[:, ::-1]
    eb = ecat.reshape(ns, sg, L, H, 2, P).transpose(0, 2, 1, 3, 4, 5).reshape(ns, L, LANES, LANES)
    wcat = jnp.stack([wr, -wi], axis=3).reshape(ns, sg, L, H, 2, P)
    wc = wcat.transpose(0, 2, 4, 5, 1, 3).reshape(ns, L, LANES, LANES)
    acr = ac[:, 0, :].reshape(ns, sg, P)
    aci = ac[:, 1, :].reshape(ns, sg, P)
    a_same = jnp.concatenate([acr, acr], axis=-1)
    a_swap = jnp.concatenate([-aci, aci], axis=-1)
    return kd.astype(BF16), eb.astype(BF16), wc.astype(BF16), a_same, a_swap


def _group_of(shape, axis):
    return lax.broadcasted_iota(jnp.int32, shape, axis) // S5_GROUP


def _s5_inc_kernel(u_ref, eb_ref, o_ref, b_ref):
    tr = u_ref.shape[0]

    @pl.when(pl.program_id(1) == 0)
    def _():
        row_grp = _group_of((LANES, LANES), 0)
        for l in range(S5_CHUNK):
            blk = eb_ref[l]
            for g in range(S5_SUPER):
                b_ref[l * LANES:(l + 1) * LANES, g * LANES:(g + 1) * LANES] = jnp.where(
                    row_grp == g, blk, jnp.zeros_like(blk))

    inc = jnp.dot(u_ref[...], b_ref[...], preferred_element_type=F32)
    for g in range(S5_SUPER):
        o_ref[pl.ds(g, tr, stride=S5_SUPER), :] = inc[:, g * LANES:(g + 1) * LANES]


def _s5_scan_kernel(inc_ref, asame_ref, aswap_ref, x_ref, state_ref, *, cb):
    @pl.when(pl.program_id(0) == 0)
    def _():
        state_ref[...] = jnp.zeros_like(state_ref)

    nseq = state_ref.shape[0]
    a_same = asame_ref[...].reshape(nseq * S5_SUPER, LANES)
    a_swap = aswap_ref[...].reshape(nseq * S5_SUPER, LANES)

    def body(c, x):
        x_ref[:, c] = x.reshape(nseq, S5_SUPER, LANES)
        inc = inc_ref[:, c].reshape(nseq * S5_SUPER, LANES)
        return x * a_same + pltpu.roll(x, LANES // 2, 1) * a_swap + inc

    x0 = state_ref[...].reshape(nseq * S5_SUPER, LANES)
    state_ref[...] = lax.fori_loop(0, cb, body, x0).reshape(nseq, S5_SUPER, LANES)


def _s5_out_kernel(u_ref, x_ref, kd_ref, wc_ref, y_ref, ys_ref, t_ref, c_ref):
    tr = u_ref.shape[0]

    @pl.when(pl.program_id(1) == 0)
    def _():
        zero = jnp.zeros((LANES, LANES), t_ref.dtype)
        col_grp = _group_of((LANES, LANES), 1)
        for l_out in range(S5_CHUNK):
            cols = slice(l_out * LANES, (l_out + 1) * LANES)
            for l_in in range(S5_CHUNK):
                t_ref[l_in * LANES:(l_in + 1) * LANES, cols] = kd_ref[l_out - l_in] if l_out >= l_in else zero
            blk = wc_ref[l_out]
            for g in range(S5_SUPER):
                c_ref[g * LANES:(g + 1) * LANES, cols] = jnp.where(col_grp == g, blk, zero)

    x = jnp.concatenate([x_ref[pl.ds(g, tr, stride=S5_SUPER), :] for g in range(S5_SUPER)], axis=-1)
    y = jnp.dot(u_ref[...], t_ref[...], preferred_element_type=F32)
    y += jnp.dot(x.astype(BF16), c_ref[...], preferred_element_type=F32)
    y = jax.nn.gelu(y)
    for l in range(S5_CHUNK):
        ys_ref[pl.ds(l, tr, stride=S5_CHUNK), :] = y[:, l * LANES:(l + 1) * LANES]
    y_ref[...] = ys_ref[...].astype(y_ref.dtype)


def _s5_core(uc, kd, eb, wc, a_same, a_swap, batch, seq, tr=512):
    ns, rows, width = uc.shape
    L = S5_CHUNK
    n = rows * L
    n_chunks = seq // L
    sw = 2 * S5_SUPER * S5_STATE
    tr = min(tr, rows)
    blocks_spec = pl.BlockSpec((None, L, LANES, LANES), lambda s, r: (s, 0, 0, 0))
    inc = pl.pallas_call(
        _s5_inc_kernel,
        grid=(ns, rows // tr),
        in_specs=[pl.BlockSpec((None, tr, width), lambda s, r: (s, r, 0)), blocks_spec],
        out_specs=pl.BlockSpec((None, tr * S5_SUPER, LANES), lambda s, r: (s, r, 0)),
        out_shape=jax.ShapeDtypeStruct((ns, rows * S5_SUPER, LANES), F32),
        scratch_shapes=[pltpu.VMEM((width, sw), BF16)],
        compiler_params=_cparams("parallel", "arbitrary"),
        name="s5_chunk_state",
    )(uc, eb)
    nseq = ns * batch
    cb = min(32, n_chunks)
    tile = (S5_SUPER, LANES)
    bcast = lambda a: jnp.broadcast_to(a[:, None], (ns, batch) + tile).reshape((nseq,) + tile)
    seq_spec = pl.BlockSpec((nseq, cb) + tile, lambda c: (0, c, 0, 0))
    tab_spec = pl.BlockSpec((nseq,) + tile, lambda c: (0, 0, 0))
    xprev = pl.pallas_call(
        functools.partial(_s5_scan_kernel, cb=cb),
        grid=(n_chunks // cb,),
        in_specs=[seq_spec, tab_spec, tab_spec],
        out_specs=seq_spec,
        out_shape=jax.ShapeDtypeStruct((nseq, n_chunks) + tile, F32),
        scratch_shapes=[pltpu.VMEM((nseq,) + tile, F32)],
        compiler_params=_cparams("arbitrary"),
        name="s5_chunk_scan",
    )(inc.reshape((nseq, n_chunks) + tile), bcast(a_same), bcast(a_swap))
    xprev = xprev.reshape(ns, rows * S5_SUPER, LANES)
    return pl.pallas_call(
        _s5_out_kernel,
        grid=(ns, rows // tr),
        in_specs=[pl.BlockSpec((None, tr, width), lambda s, r: (s, r, 0)),
                  pl.BlockSpec((None, tr * S5_SUPER, LANES), lambda s, r: (s, r, 0)),
                  blocks_spec, blocks_spec],
        out_specs=pl.BlockSpec((None, tr * L, LANES), lambda s, r: (s, r, 0)),
        out_shape=jax.ShapeDtypeStruct((ns, n, LANES), BF16),
        scratch_shapes=[pltpu.VMEM((tr * L, LANES), F32), pltpu.VMEM((width, width), BF16),
                        pltpu.VMEM((sw, width), BF16)],
        compiler_params=_cparams("parallel", "arbitrary"),
        name="s5_chunk_out",
    )(uc, xprev, kd, wc)


def _glu_router_kernel(y_ref, w_ref, res_ref, g_ref, wr_ref, h_ref, hn_ref, route_ref):
    d = res_ref.shape[1]
    y = jnp.concatenate([y_ref[g] for g in range(N_SUPER)], axis=-1)
    z = jnp.dot(y, w_ref[...], preferred_element_type=F32)
    h = res_ref[...] + z[:, :d] * jax.nn.sigmoid(z[:, d:])
    h_ref[...] = h
    hn = _rms(h, g_ref[...])
    hn_ref[...] = hn.astype(hn_ref.dtype)
    hi = hn.astype(BF16)
    lo = (hn - hi.astype(F32)).astype(BF16)
    wr = wr_ref[...]
    first = jnp.dot(hi, wr, preferred_element_type=F32)
    logits = (first[:, :LANES] + first[:, LANES:]) + jnp.dot(lo, wr[:, :LANES], preferred_element_type=F32)
    lane = lax.broadcasted_iota(jnp.int32, logits.shape, 1)
    logits = jnp.where(lane < N_EXPERTS, logits, -jnp.inf)
    m1 = jnp.max(logits, axis=-1, keepdims=True)
    i1 = jnp.min(jnp.where(logits == m1, lane, LANES), axis=-1, keepdims=True)
    rest = jnp.where(lane == i1, -jnp.inf, logits)
    m2 = jnp.max(rest, axis=-1, keepdims=True)
    i2 = jnp.min(jnp.where(rest == m2, lane, LANES), axis=-1, keepdims=True)
    e2 = jnp.exp(m2 - m1)
    g1 = 1.0 / (1.0 + e2)
    route_ref[...] = (jnp.where(lane == 0, i1.astype(F32), 0.0) + jnp.where(lane == 1, i2.astype(F32), 0.0)
                      + jnp.where(lane == 2, g1, 0.0) + jnp.where(lane == 3, e2 * g1, 0.0))


def _glu_router(y, w_glu, res, g, w_router_pad, tm=512):
    ns, n, _ = y.shape
    d = res.shape[1]
    return pl.pallas_call(
        _glu_router_kernel,
        grid=(n // tm,),
        in_specs=[pl.BlockSpec((ns, tm, LANES), lambda i: (0, i, 0)),
                  pl.BlockSpec(w_glu.shape, lambda i: (0, 0)),
                  pl.BlockSpec((tm, d), lambda i: (i, 0)),
                  pl.BlockSpec((1, d), lambda i: (0, 0)),
                  pl.BlockSpec(w_router_pad.shape, lambda i: (0, 0))],
        out_specs=[pl.BlockSpec((tm, d), lambda i: (i, 0)),
                   pl.BlockSpec((tm, d), lambda i: (i, 0)),
                   pl.BlockSpec((tm, LANES), lambda i: (i, 0))],
        out_shape=[jax.ShapeDtypeStruct((n, d), F32), jax.ShapeDtypeStruct((n, d), F32),
                   jax.ShapeDtypeStruct((n, LANES), F32)],
        compiler_params=_cparams("parallel"),
        name="s5_glu_router",
    )(y, w_glu, res, g, w_router_pad)


MOE_TM = 512
MOE_TF = 1792
DISPATCH_BLOCK = 512
COMBINE_TM = 512


def _moe_plan(route, n_tok):
    ne, tm = N_EXPERTS, MOE_TM
    n_tiles = (2 * n_tok) // tm + ne
    eid = route[:, :2].astype(jnp.int32).reshape(-1)
    onehot = (eid[:, None] == jnp.arange(ne, dtype=jnp.int32)[None, :]).astype(jnp.int32)
    csum = jnp.cumsum(onehot, axis=0)
    rank = jnp.sum((csum - onehot) * onehot, axis=1)
    cnt = csum[-1]
    padded = ((cnt + tm - 1) // tm) * tm
    ends = jnp.cumsum(padded)
    off = ends - padded
    pos = (jnp.sum(off[None, :] * onehot, axis=1) + rank).astype(jnp.int32)
    tile_end = ends // tm
    n_valid = tile_end[-1:].astype(jnp.int32)
    tiles = jnp.arange(n_tiles, dtype=jnp.int32)
    tile_expert = jnp.minimum(jnp.sum(tiles[:, None] >= tile_end[None, :], axis=1), ne - 1).astype(jnp.int32)
    pad_start = (off + cnt).astype(jnp.int32)
    pad_len = (padded - cnt).astype(jnp.int32)
    return pos, tile_expert, n_valid, pad_start, pad_len, n_tiles


def _dispatch_kernel(pos_ref, pstart_ref, plen_ref, hn_ref, xs_ref, zero_ref, sem, *, n_tok):
    nb = n_tok // DISPATCH_BLOCK
    b = pl.program_id(0)

    def row_copy(r, k):
        dst = pos_ref[2 * (b * DISPATCH_BLOCK + r) + k]
        return pltpu.make_async_copy(hn_ref.at[pl.ds(r, 1)], xs_ref.at[pl.ds(dst, 1)], sem)

    def issue(r, _):
        row_copy(r, 0).start()
        row_copy(r, 1).start()
        return 0
    lax.fori_loop(0, DISPATCH_BLOCK, issue, 0, unroll=8)

    for _ in range(2):
        pltpu.make_async_copy(hn_ref, xs_ref.at[pl.ds(0, DISPATCH_BLOCK)], sem).wait()

    @pl.when(b == nb - 1)
    def _():
        zero_ref[...] = jnp.zeros_like(zero_ref)

        def pad_copy(e, r):
            return pltpu.make_async_copy(zero_ref.at[pl.ds(0, 1)], xs_ref.at[pl.ds(pstart_ref[e] + r, 1)], sem)

        for e in range(N_EXPERTS):
            def pbody(r, _, e=e):
                pad_copy(e, r).start()
                return 0
            lax.fori_loop(0, plen_ref[e], pbody, 0)
        for e in range(N_EXPERTS):
            def wbody(r, _, e=e):
                pad_copy(e, r).wait()
                return 0
            lax.fori_loop(0, plen_ref[e], wbody, 0)


def _dispatch(hn, pos, pad_start, pad_len, n_rows):
    n_tok, d = hn.shape
    return pl.pallas_call(
        functools.partial(_dispatch_kernel, n_tok=n_tok),
        grid_spec=pltpu.PrefetchScalarGridSpec(
            num_scalar_prefetch=3,
            grid=(n_tok // DISPATCH_BLOCK,),
            in_specs=[pl.BlockSpec((DISPATCH_BLOCK, d), lambda b, pos, ps, plen: (b, 0))],
            out_specs=pl.BlockSpec(memory_space=pl.ANY),
            scratch_shapes=[pltpu.VMEM((8, d), F32), pltpu.SemaphoreType.DMA]),
        out_shape=jax.ShapeDtypeStruct((n_rows, d), F32),
        compiler_params=_cparams("arbitrary"),
        name="moe_dispatch",
    )(pos, pad_start, pad_len, hn)


def _moe_kernel(te_ref, nv_ref, x_ref, wg_ref, wu_ref, wd_ref, y_ref, xb_ref):
    i = pl.program_id(0)
    k = pl.program_id(1)

    @pl.when(i < nv_ref[0])
    def _():
        @pl.when(k == 0)
        def _():
            xb_ref[...] = x_ref[...].astype(BF16)

        hn = xb_ref[...]
        gate = jnp.dot(hn, wg_ref[...], preferred_element_type=F32)
        up = jnp.dot(hn, wu_ref[...], preferred_element_type=F32)
        act = (jax.nn.silu(gate) * up).astype(BF16)
        part = jnp.dot(act, wd_ref[...], preferred_element_type=F32)

        @pl.when(k == 0)
        def _():
            y_ref[...] = part

        @pl.when(k > 0)
        def _():
            y_ref[...] += part


def _moe_experts(xs, tile_expert, n_valid, wg, wu, wd, n_tiles):
    tm, tf = MOE_TM, MOE_TF
    d = xs.shape[1]
    f = wg.shape[2]
    nk = f // tf

    def row_map(i, k, te, nv):
        return (jnp.minimum(i, nv[0] - 1), 0)

    def kk(i, k, nv):
        return jnp.where(i < nv[0], k, nk - 1)

    def w_in_map(i, k, te, nv):
        return (te[jnp.minimum(i, nv[0] - 1)], 0, kk(i, k, nv))

    def w_out_map(i, k, te, nv):
        return (te[jnp.minimum(i, nv[0] - 1)], kk(i, k, nv), 0)

    return pl.pallas_call(
        _moe_kernel,
        grid_spec=pltpu.PrefetchScalarGridSpec(
            num_scalar_prefetch=2,
            grid=(n_tiles, nk),
            in_specs=[pl.BlockSpec((tm, d), row_map),
                      pl.BlockSpec((None, d, tf), w_in_map),
                      pl.BlockSpec((None, d, tf), w_in_map),
                      pl.BlockSpec((None, tf, d), w_out_map)],
            out_specs=pl.BlockSpec((tm, d), row_map),
            scratch_shapes=[pltpu.VMEM((tm, d), BF16)]),
        out_shape=jax.ShapeDtypeStruct(xs.shape, F32),
        compiler_params=_cparams("arbitrary", "arbitrary"),
        name="moe_experts",
    )(tile_expert, n_valid, xs, wg, wu, wd)


def _combine_kernel(pos_ref, y_ref, h_ref, route_ref, g_ref, out_ref, buf_ref, sem, *, tm, n_steps):
    i = pl.program_id(0)
    slot = i % 2

    def row_copy(step, sl, r, k):
        t = step * tm + r
        return pltpu.make_async_copy(y_ref.at[pl.ds(pos_ref[2 * t + k], 1)],
                                     buf_ref.at[sl, k, pl.ds(r, 1)], sem.at[sl])

    def issue(step, sl):
        def body(r, _):
            row_copy(step, sl, r, 0).start()
            row_copy(step, sl, r, 1).start()
            return 0
        lax.fori_loop(0, tm, body, 0, unroll=8)

    @pl.when(i == 0)
    def _():
        issue(0, 0)

    @pl.when(i + 1 < n_steps)
    def _():
        issue(i + 1, 1 - slot)

    for k in range(2):
        pltpu.make_async_copy(y_ref.at[pl.ds(0, tm)], buf_ref.at[slot, k], sem.at[slot]).wait()

    route = route_ref[...]
    out = h_ref[...] + route[:, 2:3] * buf_ref[slot, 0] + route[:, 3:4] * buf_ref[slot, 1]
    out_ref[...] = _rms(out, g_ref[...])


def _combine(y, pos, h, route, g):
    n_tok, d = h.shape
    tm = COMBINE_TM
    n_steps = n_tok // tm
    return pl.pallas_call(
        functools.partial(_combine_kernel, tm=tm, n_steps=n_steps),
        grid_spec=pltpu.PrefetchScalarGridSpec(
            num_scalar_prefetch=1,
            grid=(n_steps,),
            in_specs=[pl.BlockSpec(memory_space=pl.ANY),
                      pl.BlockSpec((tm, d), lambda i, pos: (i, 0)),
                      pl.BlockSpec((tm, LANES), lambda i, pos: (i, 0)),
                      pl.BlockSpec((1, d), lambda i, pos: (0, 0))],
            out_specs=pl.BlockSpec((tm, d), lambda i, pos: (i, 0)),
            scratch_shapes=[pltpu.VMEM((2, 2, tm, d), F32), pltpu.SemaphoreType.DMA((2,))]),
        out_shape=jax.ShapeDtypeStruct((n_tok, d), F32),
        compiler_params=_cparams("arbitrary"),
        name="moe_combine",
    )(pos, y, h, route, g)


def _moe_layer(h, hn, route, wg, wu, wd, g):
    n_tok = h.shape[0]
    pos, tile_expert, n_valid, pad_start, pad_len, n_tiles = _moe_plan(route, n_tok)
    xs = _dispatch(hn, pos, pad_start, pad_len, n_tiles * MOE_TM)
    y = _moe_experts(xs, tile_expert, n_valid, wg, wu, wd, n_tiles)
    return _combine(y, pos, h, route, g)


def _mla_layer(h, positions, mix_g, w_in, q_norm, w_uq, kv_norm, w_ukv, w_o, ffn_g):
    batch, seq = positions.shape
    win_ext, wq_ext, wkv_ext = _mla_weights(w_in, w_uq, w_ukv)
    pos = positions.astype(F32).reshape(batch * seq, 1)
    q, k, v = _mla_proj(h, pos, mix_g.reshape(1, -1), win_ext, q_norm.reshape(1, -1),
                        kv_norm.reshape(1, -1), wq_ext, wkv_ext)
    o = _attention(q, k, v, batch, seq)
    wo = w_o.reshape(MLA_HEADS, V_DIM, D_MODEL)
    wo_ext = jnp.concatenate([wo, jnp.zeros((MLA_HEADS, HEAD_PAD - V_DIM, D_MODEL), wo.dtype)], axis=1)
    wo_ext = wo_ext.reshape(MLA_HEADS * HEAD_PAD, D_MODEL).astype(BF16)
    return _mm_res_norm(o, wo_ext, h, ffn_g.reshape(1, -1))


def _s5_layer(h, hn, batch, seq, w_in, lam_re, lam_im, log_dt, b_re, b_im, c_re, c_im, d_skip, w_glu,
              ffn_g, w_router):
    u = _mm_split(hn, w_in.astype(BF16))
    kd, eb, wc, a_same, a_swap = _s5_operators(lam_re, lam_im, log_dt, b_re, b_im, c_re, c_im, d_skip)
    y = _s5_core(u, kd, eb, wc, a_same, a_swap, batch, seq)
    wr_pad = jnp.concatenate([w_router, jnp.zeros((D_MODEL, LANES - N_EXPERTS), w_router.dtype)], axis=1)
    wr_hi = wr_pad.astype(BF16)
    wr_lo = (wr_pad - wr_hi.astype(F32)).astype(BF16)
    return _glu_router(y, w_glu.astype(BF16), h, ffn_g.reshape(1, -1), jnp.concatenate([wr_hi, wr_lo], axis=1))


def kernel(x, positions, mix_norm, ffn_norm, final_norm, mla_w_in, mla_q_norm, mla_w_uq, mla_kv_norm, mla_w_ukv, mla_w_o, ffn_w_gate, ffn_w_up, ffn_w_down, s5_w_in, s5_lambda_re, s5_lambda_im, s5_log_dt, s5_b_re, s5_b_im, s5_c_re, s5_c_im, s5_d, s5_w_glu, moe_w_router, moe_w_gate, moe_w_up, moe_w_down):
    batch, seq, d = x.shape
    h0 = x.reshape(batch * seq, d)
    h1, hn1 = _mla_layer(h0, positions, mix_norm[0], mla_w_in[0], mla_q_norm[0], mla_w_uq[0],
                         mla_kv_norm[0], mla_w_ukv[0], mla_w_o[0], ffn_norm[0])
    h2, hn2 = _ffn(hn1, h1, ffn_w_gate[0].astype(BF16), ffn_w_up[0].astype(BF16),
                   ffn_w_down[0].astype(BF16), mix_norm[1].reshape(1, -1))
    h3, hn3, route = _s5_layer(h2, hn2, batch, seq, s5_w_in[0], s5_lambda_re[0], s5_lambda_im[0],
                              s5_log_dt[0], s5_b_re[0], s5_b_im[0], s5_c_re[0], s5_c_im[0], s5_d[0],
                              s5_w_glu[0], ffn_norm[1], moe_w_router[0])
    out = _moe_layer(h3, hn3, route, moe_w_gate[0].astype(BF16), moe_w_up[0].astype(BF16),
                     moe_w_down[0].astype(BF16), final_norm.reshape(1, -1))
    return out.reshape(batch, seq, d)
```

```python
import functools
import math

import jax
import jax.numpy as jnp
import numpy as np
from jax import lax
from jax.experimental import pallas as pl
from jax.experimental.pallas import tpu as pltpu

F32 = jnp.float32
BF16 = jnp.bfloat16

D_MODEL = 1024
MLA_HEADS = 16
QK_NOPE = 64
QK_ROPE = 32
V_DIM = 64
Q_LORA = 384
KV_LORA = 256
ROPE_THETA = 10000.0
HEAD_PAD = 128
ATTN_HEADS_PER_STEP = 4
S5_GROUP = 16
S5_GROUPS = D_MODEL // S5_GROUP
S5_STATE = 64
S5_CHUNK = 16
S5_SUPER = 8
N_SUPER = S5_GROUPS // S5_SUPER
N_EXPERTS = 8
EPS = 1e-6
NEG_INF = -1e30
LANES = 128
VMEM_LIMIT = 56 * 1024 * 1024


def _cparams(*sem):
    return pltpu.CompilerParams(dimension_semantics=sem, vmem_limit_bytes=VMEM_LIMIT)


def _rms(x, g):
    return x * lax.rsqrt(jnp.mean(x * x, axis=-1, keepdims=True) + EPS) * g


def _mla_proj_kernel(x_ref, pos_ref, g_ref, win_ref, qn_ref, kvn_ref, wq_ref, wkv_ref, tab_ref,
                     q_ref, k_ref, v_ref):
    hn = _rms(x_ref[...], g_ref[...]).astype(BF16)
    proj = jnp.dot(hn, win_ref[...], preferred_element_type=F32)
    cq = _rms(proj[:, :Q_LORA], qn_ref[...]).astype(BF16)
    ckv = _rms(proj[:, Q_LORA:Q_LORA + KV_LORA], kvn_ref[...]).astype(BF16)
    ka = proj[:, 640:768]
    kb = proj[:, 768:896]
    tab = tab_ref[...]
    ang = pos_ref[...] * tab[0:1, :]
    cos = jnp.cos(ang)
    sin = jnp.sin(ang)
    kp = ka * (cos * tab[1:2, :]) + kb * (sin * tab[2:3, :])
    tq = cos * tab[3:4, :] + sin * tab[4:5, :]
    q = jnp.dot(cq, wq_ref[...], preferred_element_type=F32)
    kv = jnp.dot(ckv, wkv_ref[...], preferred_element_type=F32)
    for h in range(MLA_HEADS):
        sl = slice(h * HEAD_PAD, (h + 1) * HEAD_PAD)
        q_ref[:, sl] = (q[:, sl] * tq).astype(BF16)
        k_ref[:, sl] = (kv[:, sl] + kp).astype(BF16)
    v = kv[:, MLA_HEADS * HEAD_PAD:]
    vlane = lax.broadcasted_iota(jnp.int32, v.shape, 1) % HEAD_PAD
    v_ref[...] = jnp.where(vlane == V_DIM, 1.0, v).astype(BF16)


def _mla_tables():
    inv_freq = ROPE_THETA ** (-np.arange(0, QK_ROPE, 2, dtype=np.float32) / QK_ROPE)
    inv_freq = inv_freq.astype(np.float32)
    half = QK_ROPE // 2
    tab = np.zeros((8, LANES), np.float32)
    scale = math.log2(math.e) / math.sqrt(QK_NOPE + QK_ROPE)
    tab[3, :QK_NOPE] = scale
    kc = [1.0, -1.0, 1.0, 1.0]
    ks = [-1.0, 1.0, 1.0, 1.0]
    for grp in range(4):
        sl = slice(QK_NOPE + grp * half, QK_NOPE + (grp + 1) * half)
        tab[0, sl] = inv_freq
        tab[1, sl] = kc[grp]
        tab[2, sl] = ks[grp]
        if grp % 2 == 0:
            tab[3, sl] = scale
        else:
            tab[4, sl] = scale
    return jnp.asarray(tab)


def _mla_weights(w_in, w_uq, w_ukv):
    half = QK_ROPE // 2
    base = Q_LORA + KV_LORA
    k1 = w_in[:, base:base + half]
    k2 = w_in[:, base + half:base + QK_ROPE]
    z64 = jnp.zeros((D_MODEL, QK_NOPE), w_in.dtype)
    win_ext = jnp.concatenate([w_in[:, :base], z64, k1, k1, k2, k2, z64, k2, k2, k1, k1], axis=1)
    wq = w_uq.reshape(Q_LORA, MLA_HEADS, QK_NOPE + QK_ROPE)
    q1 = wq[:, :, QK_NOPE:QK_NOPE + half]
    q2 = wq[:, :, QK_NOPE + half:]
    wq_ext = jnp.concatenate([wq[:, :, :QK_NOPE], q1, q2, q2, q1], axis=2)
    wq_ext = wq_ext.reshape(Q_LORA, MLA_HEADS * HEAD_PAD)
    wkv = w_ukv.reshape(KV_LORA, MLA_HEADS, QK_NOPE + V_DIM)
    zk = jnp.zeros((KV_LORA, MLA_HEADS, HEAD_PAD - QK_NOPE), w_ukv.dtype)
    zv = jnp.zeros((KV_LORA, MLA_HEADS, HEAD_PAD - V_DIM), w_ukv.dtype)
    wk_ext = jnp.concatenate([wkv[:, :, :QK_NOPE], zk], axis=2).reshape(KV_LORA, -1)
    wv_ext = jnp.concatenate([wkv[:, :, QK_NOPE:], zv], axis=2).reshape(KV_LORA, -1)
    wkv_ext = jnp.concatenate([wk_ext, wv_ext], axis=1)
    return win_ext.astype(BF16), wq_ext.astype(BF16), wkv_ext.astype(BF16)


def _mla_proj(x2d, pos, g, win_ext, qn, kvn, wq_ext, wkv_ext, tm=512):
    n = x2d.shape[0]
    hw = MLA_HEADS * HEAD_PAD
    full = lambda a: pl.BlockSpec(a.shape, lambda i: (0,) * a.ndim)
    tab = _mla_tables()
    return pl.pallas_call(
        _mla_proj_kernel,
        grid=(n // tm,),
        in_specs=[pl.BlockSpec((tm, D_MODEL), lambda i: (i, 0)),
                  pl.BlockSpec((tm, 1), lambda i: (i, 0)),
                  full(g), full(win_ext), full(qn), full(kvn), full(wq_ext), full(wkv_ext), full(tab)],
        out_specs=[pl.BlockSpec((tm, hw), lambda i: (i, 0))] * 3,
        out_shape=[jax.ShapeDtypeStruct((n, hw), BF16)] * 3,
        compiler_params=_cparams("parallel"),
        name="mla_proj",
    )(x2d, pos, g, win_ext, qn, kvn, wq_ext, wkv_ext, tab)


def _attn_kernel(q_ref, k_ref, v_ref, o_ref, s0_ref, s1_ref, p0_ref, p1_ref, *, blk):
    i = pl.program_id(2)
    heads = range(ATTN_HEADS_PER_STEP)
    lanes = [slice(h * HEAD_PAD, (h + 1) * HEAD_PAD) for h in heads]
    q = [q_ref[:, lanes[h]] for h in heads]

    def stage_a(j, s_ref):
        off = pl.multiple_of(j * blk, blk)
        for h in heads:
            s_ref[h] = lax.dot_general(q[h], k_ref[pl.ds(off, blk), lanes[h]], (((1,), (1,)), ((), ())),
                                       preferred_element_type=F32)

    def stage_b(state, s_ref, p_ref, masked=False):
        out = []
        for h in heads:
            m, _, acc = state[h]
            s = s_ref[h]
            if masked:
                row = lax.broadcasted_iota(jnp.int32, (blk, blk), 0)
                col = lax.broadcasted_iota(jnp.int32, (blk, blk), 1)
                s = jnp.where(col <= row, s, NEG_INF)
            m_new = jnp.maximum(m, jnp.max(s, axis=-1, keepdims=True))
            p_ref[h] = jnp.exp2((s - m_new).astype(BF16))
            out.append((m_new, jnp.exp2(m - m_new), acc))
        return tuple(out)

    def stage_c(j, state, p_ref):
        off = pl.multiple_of(j * blk, blk)
        out = []
        for h in heads:
            m, alpha, acc = state[h]
            pv = jnp.dot(p_ref[h], v_ref[pl.ds(off, blk), lanes[h]], preferred_element_type=F32)
            out.append((m, alpha, alpha * acc + pv))
        return tuple(out)

    def steady(j, state, even):
        s_j, s_n, p_j, p_n = (s0_ref, s1_ref, p0_ref, p1_ref) if even else (s1_ref, s0_ref, p1_ref, p0_ref)
        state = stage_c(j, state, p_j)
        stage_a(j + 2, s_j)
        return stage_b(state, s_n, p_n)

    def drain(state, even_last):
        s_i, p_i, p_prev = (s0_ref, p0_ref, p1_ref) if even_last else (s1_ref, p1_ref, p0_ref)
        state = stage_c(i - 1, state, p_prev)
        state = stage_b(state, s_i, p_i, masked=True)
        finish(stage_c(i, state, p_i))

    def finish(state):
        lane = lax.broadcasted_iota(jnp.int32, (blk, HEAD_PAD), 1)
        for t in range(ATTN_HEADS_PER_STEP // 2):
            lo, hi = (state[2 * t + e][2] for e in range(2))
            lo = lo / lo[:, V_DIM:V_DIM + 1]
            hi = pltpu.roll(hi / hi[:, V_DIM:V_DIM + 1], V_DIM, 1)
            o_ref[:, lanes[t]] = jnp.where(lane < V_DIM, lo, hi).astype(o_ref.dtype)

    init = tuple((jnp.full((blk, 1), NEG_INF, F32), jnp.ones((blk, 1), F32), jnp.zeros((blk, HEAD_PAD), F32))
                 for _ in heads)
    stage_a(0, s0_ref)

    @pl.when(i == 0)
    def _():
        finish(stage_c(0, stage_b(init, s0_ref, p0_ref, masked=True), p0_ref))

    @pl.when(i > 0)
    def _():
        stage_a(1, s1_ref)
        state = stage_b(init, s0_ref, p0_ref)

        def pair(jj, st):
            return steady(2 * jj + 1, steady(2 * jj, st, even=True), even=False)
        state = lax.fori_loop(0, (i - 1) // 2, pair, state)

        @pl.when(i % 2 == 1)
        def _():
            drain(state, even_last=False)

        @pl.when(i % 2 == 0)
        def _():
            drain(steady(i - 2, state, even=True), even_last=True)


def _attention(q, k, v, batch, seq, blk=512):
    n, hw = q.shape
    nq = seq // blk
    nh = ATTN_HEADS_PER_STEP
    width = nh * HEAD_PAD
    return pl.pallas_call(
        functools.partial(_attn_kernel, blk=blk),
        grid=(batch, MLA_HEADS // nh, nq),
        in_specs=[pl.BlockSpec((blk, width), lambda b, h, i: (b * nq + i, h)),
                  pl.BlockSpec((seq, width), lambda b, h, i: (b, h), pipeline_mode=pl.Buffered(1)),
                  pl.BlockSpec((seq, width), lambda b, h, i: (b, h), pipeline_mode=pl.Buffered(1))],
        out_specs=pl.BlockSpec((blk, nh * V_DIM), lambda b, h, i: (b * nq + i, h)),
        out_shape=jax.ShapeDtypeStruct((n, MLA_HEADS * V_DIM), BF16),
        scratch_shapes=[pltpu.VMEM((nh, blk, blk), F32), pltpu.VMEM((nh, blk, blk), F32),
                        pltpu.VMEM((nh, blk, blk), BF16), pltpu.VMEM((nh, blk, blk), BF16)],
        compiler_params=_cparams("parallel", "parallel", "arbitrary"),
        name="mla_attention",
    )(q, k, v)


def _proj_ffn_kernel(o_ref, res_ref, wo_ref, g1_ref, wg_ref, wu_ref, wd_ref, g2_ref, out_ref, outn_ref):
    h1 = res_ref[...] + jnp.dot(o_ref[...], wo_ref[...], preferred_element_type=F32)
    hn = _rms(h1, g1_ref[...]).astype(BF16)
    gate = jnp.dot(hn, wg_ref[...], preferred_element_type=F32)
    up = jnp.dot(hn, wu_ref[...], preferred_element_type=F32)
    act = (jax.nn.silu(gate) * up).astype(BF16)
    out = h1 + jnp.dot(act, wd_ref[...], preferred_element_type=F32)
    out_ref[...] = out
    outn_ref[...] = _rms(out, g2_ref[...]).astype(outn_ref.dtype)


def _proj_ffn(o, res, wo, g1, wg, wu, wd, g2, tm=512):
    n, d = res.shape
    row = lambda a: pl.BlockSpec((tm, a.shape[1]), lambda i: (i, 0))
    resident = lambda a: pl.BlockSpec(a.shape, lambda i: (0, 0), pipeline_mode=pl.Buffered(1))
    return pl.pallas_call(
        _proj_ffn_kernel,
        grid=(n // tm,),
        in_specs=[row(o), row(res), resident(wo), resident(g1), resident(wg), resident(wu), resident(wd),
                  resident(g2)],
        out_specs=[pl.BlockSpec((tm, d), lambda i: (i, 0))] * 2,
        out_shape=[jax.ShapeDtypeStruct((n, d), F32), jax.ShapeDtypeStruct((n, d), BF16)],
        compiler_params=_cparams("parallel"),
        name="proj_swiglu_dense",
    )(o, res, wo, g1, wg, wu, wd, g2)


def _mm_split_kernel(a_ref, w_ref, o_ref, r_ref):
    tm = a_ref.shape[0]
    r = jnp.dot(a_ref[...], w_ref[...], preferred_element_type=F32)
    for g in range(N_SUPER):
        r_ref[g] = r[:, g * LANES:(g + 1) * LANES]
    for g in range(N_SUPER):
        for l in range(S5_CHUNK):
            piece = r_ref[g, pl.ds(l, tm // S5_CHUNK, stride=S5_CHUNK), :]
            o_ref[g, :, l * LANES:(l + 1) * LANES] = piece.astype(o_ref.dtype)


def _mm_split(a, w, tm=512):
    n, k = a.shape
    return pl.pallas_call(
        _mm_split_kernel,
        grid=(n // tm,),
        in_specs=[pl.BlockSpec((tm, k), lambda i: (i, 0)),
                  pl.BlockSpec(w.shape, lambda i: (0, 0))],
        out_specs=pl.BlockSpec((N_SUPER, tm // S5_CHUNK, S5_CHUNK * LANES), lambda i: (0, i, 0)),
        out_shape=jax.ShapeDtypeStruct((N_SUPER, n // S5_CHUNK, S5_CHUNK * LANES), BF16),
        scratch_shapes=[pltpu.VMEM((N_SUPER, tm, LANES), F32)],
        compiler_params=_cparams("parallel"),
        name="s5_in_proj",
    )(a, w)


def _s5_param_kernel(lr_ref, li_ref, ldt_ref, btr_ref, bti_ref, cr_ref, ci_ref, d_ref,
                     kt_ref, er_ref, ei_ref, wr_ref, wi_ref, ac_ref):
    L = S5_CHUNK
    lr = lr_ref[...]
    li = li_ref[...]
    dt = jnp.exp(ldt_ref[...])
    mag = jnp.exp(lr * dt)
    ar = mag * jnp.cos(li * dt)
    ai = mag * jnp.sin(li * dt)
    den = lr * lr + li * li
    gr = ((ar - 1.0) * lr + ai * li) / den
    gi = (ai * lr - (ar - 1.0) * li) / den
    btr = btr_ref[...]
    bti = bti_ref[...]
    bbr = gr * btr - gi * bti
    bbi = gr * bti + gi * btr
    kf = lax.broadcasted_iota(jnp.int32, (L, 1), 0).astype(F32)
    pmag = jnp.exp(lr * dt * kf)
    pr = pmag * jnp.cos(li * dt * kf)
    pi = pmag * jnp.sin(li * dt * kf)
    cr = cr_ref[...]
    ci = ci_ref[...]
    pr3 = pr[:, None, :]
    pi3 = pi[:, None, :]
    wr = cr[None] * pr3 - ci[None] * pi3
    wi = cr[None] * pi3 + ci[None] * pr3
    dn = (((1,), (1,)), ((), ()))
    hp = lax.Precision.HIGHEST
    kt = (lax.dot_general(wr.reshape(L * S5_GROUP, S5_STATE), bbr, dn, precision=hp,
                          preferred_element_type=F32)
          - lax.dot_general(wi.reshape(L * S5_GROUP, S5_STATE), bbi, dn, precision=hp,
                            preferred_element_type=F32))
    row = lax.broadcasted_iota(jnp.int32, kt.shape, 0)
    col = lax.broadcasted_iota(jnp.int32, kt.shape, 1)
    kt_ref[...] = kt + jnp.where(row == col, d_ref[...], 0.0)
    er_ref[...] = pr3 * bbr[None] - pi3 * bbi[None]
    ei_ref[...] = pr3 * bbi[None] + pi3 * bbr[None]
    wr_ref[...] = wr * ar - wi * ai
    wi_ref[...] = wr * ai + wi * ar
    lastr = pr[L - 1:L, :]
    lasti = pi[L - 1:L, :]
    ac_ref[0:1, :] = lastr * ar - lasti * ai
    ac_ref[1:2, :] = lastr * ai + lasti * ar


def _s5_operators(lam_re, lam_im, log_dt, b_re, b_im, c_re, c_im, d_skip):
    G, P, H, L = S5_GROUPS, S5_STATE, S5_GROUP, S5_CHUNK
    per_g = lambda *shape: pl.BlockSpec((None,) + shape, lambda g: (g,) + (0,) * len(shape))
    kt, er, ei, wr, wi, ac = pl.pallas_call(
        _s5_param_kernel,
        grid=(G,),
        in_specs=[per_g(1, P), per_g(1, P), per_g(1, 1), per_g(H, P), per_g(H, P),
                  per_g(H, P), per_g(H, P), per_g(1, H)],
        out_specs=[per_g(L * H, H), per_g(L, H, P), per_g(L, H, P), per_g(L, H, P), per_g(L, H, P),
                   per_g(2, P)],
        out_shape=[jax.ShapeDtypeStruct((G, L * H, H), F32)]
        + [jax.ShapeDtypeStruct((G, L, H, P), F32)] * 4
        + [jax.ShapeDtypeStruct((G, 2, P), F32)],
        compiler_params=_cparams("parallel"),
        name="s5_operators",
    )(lam_re.reshape(G, 1, P), lam_im.reshape(G, 1, P), log_dt.reshape(G, 1, 1),
      b_re.transpose(0, 2, 1), b_im.transpose(0, 2, 1), c_re, c_im, d_skip.reshape(G, 1, H))

    ns, sg = N_SUPER, S5_SUPER
    eye = jnp.eye(sg, dtype=F32)
    ktg = kt.reshape(ns, sg, L, H, H)
    kd = jnp.einsum('SgtOI,gk->StgIkO', ktg, eye).reshape(ns, L, LANES, LANES)
    ecat = jnp.stack([er, ei], axis=3)[:, ::-1]
    eb = ecat.reshape(ns, sg, L, H, 2, P).transpose(0, 2, 1, 3, 4, 5).reshape(ns, L, LANES, LANES)
    wcat = jnp.stack([wr, -wi], axis=3).reshape(ns, sg, L, H, 2, P)
    wc = wcat.transpose(0, 2, 4, 5, 1, 3).reshape(ns, L, LANES, LANES)
    acr = ac[:, 0, :].reshape(ns, sg, P)
    aci = ac[:, 1, :].reshape(ns, sg, P)
    a_same = jnp.concatenate([acr, acr], axis=-1)
    a_swap = jnp.concatenate([-aci, aci], axis=-1)
    return kd.astype(BF16), eb.astype(BF16), wc.astype(BF16), a_same, a_swap


def _group_of(shape, axis):
    return lax.broadcasted_iota(jnp.int32, shape, axis) // S5_GROUP


def _s5_inc_kernel(u_ref, eb_ref, o_ref, b_ref):
    tr = u_ref.shape[0]

    @pl.when(pl.program_id(1) == 0)
    def _():
        row_grp = _group_of((LANES, LANES), 0)
        for l in range(S5_CHUNK):
            blk = eb_ref[l]
            for g in range(S5_SUPER):
                b_ref[l * LANES:(l + 1) * LANES, g * LANES:(g + 1) * LANES] = jnp.where(
                    row_grp == g, blk, jnp.zeros_like(blk))

    inc = jnp.dot(u_ref[...], b_ref[...], preferred_element_type=F32)
    for g in range(S5_SUPER):
        o_ref[pl.ds(g, tr, stride=S5_SUPER), :] = inc[:, g * LANES:(g + 1) * LANES]


def _s5_scan_kernel(inc_ref, asame_ref, aswap_ref, x_ref, state_ref, *, cb):
    @pl.when(pl.program_id(0) == 0)
    def _():
        state_ref[...] = jnp.zeros_like(state_ref)

    nseq = state_ref.shape[0]
    a_same = asame_ref[...].reshape(nseq * S5_SUPER, LANES)
    a_swap = aswap_ref[...].reshape(nseq * S5_SUPER, LANES)

    def body(c, x):
        x_ref[:, c] = x.reshape(nseq, S5_SUPER, LANES)
        inc = inc_ref[:, c].reshape(nseq * S5_SUPER, LANES)
        return x * a_same + pltpu.roll(x, LANES // 2, 1) * a_swap + inc

    x0 = state_ref[...].reshape(nseq * S5_SUPER, LANES)
    state_ref[...] = lax.fori_loop(0, cb, body, x0).reshape(nseq, S5_SUPER, LANES)


def _s5_out_kernel(u_ref, x_ref, kd_ref, wc_ref, y_ref, ys_ref, t_ref, c_ref):
    tr = u_ref.shape[0]

    @pl.when(pl.program_id(1) == 0)
    def _():
        zero = jnp.zeros((LANES, LANES), t_ref.dtype)
        col_grp = _group_of((LANES, LANES), 1)
        for l_out in range(S5_CHUNK):
            cols = slice(l_out * LANES, (l_out + 1) * LANES)
            for l_in in range(S5_CHUNK):
                t_ref[l_in * LANES:(l_in + 1) * LANES, cols] = kd_ref[l_out - l_in] if l_out >= l_in else zero
            blk = wc_ref[l_out]
            for g in range(S5_SUPER):
                c_ref[g * LANES:(g + 1) * LANES, cols] = jnp.where(col_grp == g, blk, zero)

    x = jnp.concatenate([x_ref[pl.ds(g, tr, stride=S5_SUPER), :] for g in range(S5_SUPER)], axis=-1)
    y = jnp.dot(u_ref[...], t_ref[...], preferred_element_type=F32)
    y += jnp.dot(x.astype(BF16), c_ref[...], preferred_element_type=F32)
    y = jax.nn.gelu(y)
    for l in range(S5_CHUNK):
        ys_ref[pl.ds(l, tr, stride=S5_CHUNK), :] = y[:, l * LANES:(l + 1) * LANES]
    y_ref[...] = ys_ref[...].astype(y_ref.dtype)


def _s5_core(uc, kd, eb, wc, a_same, a_swap, batch, seq, tr=512):
    ns, rows, width = uc.shape
    L = S5_CHUNK
    n = rows * L
    n_chunks = seq // L
    sw = 2 * S5_SUPER * S5_STATE
    tr = min(tr, rows)
    blocks_spec = pl.BlockSpec((None, L, LANES, LANES), lambda s, r: (s, 0, 0, 0))
    inc = pl.pallas_call(
        _s5_inc_kernel,
        grid=(ns, rows // tr),
        in_specs=[pl.BlockSpec((None, tr, width), lambda s, r: (s, r, 0)), blocks_spec],
        out_specs=pl.BlockSpec((None, tr * S5_SUPER, LANES), lambda s, r: (s, r, 0)),
        out_shape=jax.ShapeDtypeStruct((ns, rows * S5_SUPER, LANES), F32),
        scratch_shapes=[pltpu.VMEM((width, sw), BF16)],
        compiler_params=_cparams("parallel", "arbitrary"),
        name="s5_chunk_state",
    )(uc, eb)
    nseq = ns * batch
    cb = min(32, n_chunks)
    tile = (S5_SUPER, LANES)
    bcast = lambda a: jnp.broadcast_to(a[:, None], (ns, batch) + tile).reshape((nseq,) + tile)
    seq_spec = pl.BlockSpec((nseq, cb) + tile, lambda c: (0, c, 0, 0))
    tab_spec = pl.BlockSpec((nseq,) + tile, lambda c: (0, 0, 0))
    xprev = pl.pallas_call(
        functools.partial(_s5_scan_kernel, cb=cb),
        grid=(n_chunks // cb,),
        in_specs=[seq_spec, tab_spec, tab_spec],
        out_specs=seq_spec,
        out_shape=jax.ShapeDtypeStruct((nseq, n_chunks) + tile, F32),
        scratch_shapes=[pltpu.VMEM((nseq,) + tile, F32)],
        compiler_params=_cparams("arbitrary"),
        name="s5_chunk_scan",
    )(inc.reshape((nseq, n_chunks) + tile), bcast(a_same), bcast(a_swap))
    xprev = xprev.reshape(ns, rows * S5_SUPER, LANES)
    return pl.pallas_call(
        _s5_out_kernel,
        grid=(ns, rows // tr),
        in_specs=[pl.BlockSpec((None, tr, width), lambda s, r: (s, r, 0)),
                  pl.BlockSpec((None, tr * S5_SUPER, LANES), lambda s, r: (s, r, 0)),
                  blocks_spec, blocks_spec],
        out_specs=pl.BlockSpec((None, tr * L, LANES), lambda s, r: (s, r, 0)),
        out_shape=jax.ShapeDtypeStruct((ns, n, LANES), BF16),
        scratch_shapes=[pltpu.VMEM((tr * L, LANES), F32), pltpu.VMEM((width, width), BF16),
                        pltpu.VMEM((sw, width), BF16)],
        compiler_params=_cparams("parallel", "arbitrary"),
        name="s5_chunk_out",
    )(uc, xprev, kd, wc)


def _glu_router_kernel(y_ref, w_ref, res_ref, g_ref, wr_ref, h_ref, hn_ref, route_ref):
    d = res_ref.shape[1]
    y = jnp.concatenate([y_ref[g] for g in range(N_SUPER)], axis=-1)
    z = jnp.dot(y, w_ref[...], preferred_element_type=F32)
    h = res_ref[...] + z[:, :d] * jax.nn.sigmoid(z[:, d:])
    h_ref[...] = h
    hn = _rms(h, g_ref[...])
    hn_ref[...] = hn.astype(hn_ref.dtype)
    hi = hn.astype(BF16)
    lo = (hn - hi.astype(F32)).astype(BF16)
    wr = wr_ref[...]
    first = jnp.dot(hi, wr, preferred_element_type=F32)
    logits = (first[:, :LANES] + first[:, LANES:]) + jnp.dot(lo, wr[:, :LANES], preferred_element_type=F32)
    lane = lax.broadcasted_iota(jnp.int32, logits.shape, 1)
    logits = jnp.where(lane < N_EXPERTS, logits, -jnp.inf)
    m1 = jnp.max(logits, axis=-1, keepdims=True)
    i1 = jnp.min(jnp.where(logits == m1, lane, LANES), axis=-1, keepdims=True)
    rest = jnp.where(lane == i1, -jnp.inf, logits)
    m2 = jnp.max(rest, axis=-1, keepdims=True)
    i2 = jnp.min(jnp.where(rest == m2, lane, LANES), axis=-1, keepdims=True)
    e2 = jnp.exp(m2 - m1)
    g1 = 1.0 / (1.0 + e2)
    route_ref[...] = (jnp.where(lane == 0, i1.astype(F32), 0.0) + jnp.where(lane == 1, i2.astype(F32), 0.0)
                      + jnp.where(lane == 2, g1, 0.0) + jnp.where(lane == 3, e2 * g1, 0.0))


def _glu_router(y, w_glu, res, g, w_router_pad, tm=512):
    ns, n, _ = y.shape
    d = res.shape[1]
    return pl.pallas_call(
        _glu_router_kernel,
        grid=(n // tm,),
        in_specs=[pl.BlockSpec((ns, tm, LANES), lambda i: (0, i, 0)),
                  pl.BlockSpec(w_glu.shape, lambda i: (0, 0)),
                  pl.BlockSpec((tm, d), lambda i: (i, 0)),
                  pl.BlockSpec((1, d), lambda i: (0, 0)),
                  pl.BlockSpec(w_router_pad.shape, lambda i: (0, 0))],
        out_specs=[pl.BlockSpec((tm, d), lambda i: (i, 0)),
                   pl.BlockSpec((tm, d), lambda i: (i, 0)),
                   pl.BlockSpec((tm, LANES), lambda i: (i, 0))],
        out_shape=[jax.ShapeDtypeStruct((n, d), F32), jax.ShapeDtypeStruct((n, d), F32),
                   jax.ShapeDtypeStruct((n, LANES), F32)],
        compiler_params=_cparams("parallel"),
        name="s5_glu_router",
    )(y, w_glu, res, g, w_router_pad)


MOE_TM = 512
MOE_TF = 1792
DISPATCH_BLOCK = 512
COMBINE_TM = 512


def _moe_plan(route, n_tok):
    ne, tm = N_EXPERTS, MOE_TM
    n_tiles = (2 * n_tok) // tm + ne
    eid = route[:, :2].astype(jnp.int32).reshape(-1)
    onehot = (eid[:, None] == jnp.arange(ne, dtype=jnp.int32)[None, :]).astype(jnp.int32)
    csum = jnp.cumsum(onehot, axis=0)
    rank = jnp.sum((csum - onehot) * onehot, axis=1)
    cnt = csum[-1]
    padded = ((cnt + tm - 1) // tm) * tm
    ends = jnp.cumsum(padded)
    off = ends - padded
    pos = (jnp.sum(off[None, :] * onehot, axis=1) + rank).astype(jnp.int32)
    tile_end = ends // tm
    n_valid = tile_end[-1:].astype(jnp.int32)
    tiles = jnp.arange(n_tiles, dtype=jnp.int32)
    tile_expert = jnp.minimum(jnp.sum(tiles[:, None] >= tile_end[None, :], axis=1), ne - 1).astype(jnp.int32)
    pad_start = (off + cnt).astype(jnp.int32)
    pad_len = (padded - cnt).astype(jnp.int32)
    return pos, tile_expert, n_valid, pad_start, pad_len, n_tiles


def _dispatch_kernel(pos_ref, pstart_ref, plen_ref, hn_ref, xs_ref, zero_ref, sem, *, n_tok):
    nb = n_tok // DISPATCH_BLOCK
    b = pl.program_id(0)

    def row_copy(r, k):
        dst = pos_ref[2 * (b * DISPATCH_BLOCK + r) + k]
        return pltpu.make_async_copy(hn_ref.at[pl.ds(r, 1)], xs_ref.at[pl.ds(dst, 1)], sem)

    def issue(r, _):
        row_copy(r, 0).start()
        row_copy(r, 1).start()
        return 0
    lax.fori_loop(0, DISPATCH_BLOCK, issue, 0, unroll=8)

    for _ in range(2):
        pltpu.make_async_copy(hn_ref, xs_ref.at[pl.ds(0, DISPATCH_BLOCK)], sem).wait()

    @pl.when(b == nb - 1)
    def _():
        zero_ref[...] = jnp.zeros_like(zero_ref)

        def pad_copy(e, r):
            return pltpu.make_async_copy(zero_ref.at[pl.ds(0, 1)], xs_ref.at[pl.ds(pstart_ref[e] + r, 1)], sem)

        for e in range(N_EXPERTS):
            def pbody(r, _, e=e):
                pad_copy(e, r).start()
                return 0
            lax.fori_loop(0, plen_ref[e], pbody, 0)
        for e in range(N_EXPERTS):
            def wbody(r, _, e=e):
                pad_copy(e, r).wait()
                return 0
            lax.fori_loop(0, plen_ref[e], wbody, 0)


def _dispatch(hn, pos, pad_start, pad_len, n_rows):
    n_tok, d = hn.shape
    return pl.pallas_call(
        functools.partial(_dispatch_kernel, n_tok=n_tok),
        grid_spec=pltpu.PrefetchScalarGridSpec(
            num_scalar_prefetch=3,
            grid=(n_tok // DISPATCH_BLOCK,),
            in_specs=[pl.BlockSpec((DISPATCH_BLOCK, d), lambda b, pos, ps, plen: (b, 0))],
            out_specs=pl.BlockSpec(memory_space=pl.ANY),
            scratch_shapes=[pltpu.VMEM((8, d), F32), pltpu.SemaphoreType.DMA]),
        out_shape=jax.ShapeDtypeStruct((n_rows, d), F32),
        compiler_params=_cparams("arbitrary"),
        name="moe_dispatch",
    )(pos, pad_start, pad_len, hn)


def _moe_kernel(te_ref, nv_ref, x_ref, wg_ref, wu_ref, wd_ref, y_ref, xb_ref):
    i = pl.program_id(0)
    k = pl.program_id(1)

    @pl.when(i < nv_ref[0])
    def _():
        @pl.when(k == 0)
        def _():
            xb_ref[...] = x_ref[...].astype(BF16)

        hn = xb_ref[...]
        gate = jnp.dot(hn, wg_ref[...], preferred_element_type=F32)
        up = jnp.dot(hn, wu_ref[...], preferred_element_type=F32)
        act = (jax.nn.silu(gate) * up).astype(BF16)
        part = jnp.dot(act, wd_ref[...], preferred_element_type=F32)

        @pl.when(k == 0)
        def _():
            y_ref[...] = part

        @pl.when(k > 0)
        def _():
            y_ref[...] += part


def _moe_experts(xs, tile_expert, n_valid, wg, wu, wd, n_tiles):
    tm, tf = MOE_TM, MOE_TF
    d = xs.shape[1]
    f = wg.shape[2]
    nk = f // tf

    def row_map(i, k, te, nv):
        return (jnp.minimum(i, nv[0] - 1), 0)

    def kk(i, k, nv):
        return jnp.where(i < nv[0], k, nk - 1)

    def w_in_map(i, k, te, nv):
        return (te[jnp.minimum(i, nv[0] - 1)], 0, kk(i, k, nv))

    def w_out_map(i, k, te, nv):
        return (te[jnp.minimum(i, nv[0] - 1)], kk(i, k, nv), 0)

    return pl.pallas_call(
        _moe_kernel,
        grid_spec=pltpu.PrefetchScalarGridSpec(
            num_scalar_prefetch=2,
            grid=(n_tiles, nk),
            in_specs=[pl.BlockSpec((tm, d), row_map),
                      pl.BlockSpec((None, d, tf), w_in_map),
                      pl.BlockSpec((None, d, tf), w_in_map),
                      pl.BlockSpec((None, tf, d), w_out_map)],
            out_specs=pl.BlockSpec((tm, d), row_map),
            scratch_shapes=[pltpu.VMEM((tm, d), BF16)]),
        out_shape=jax.ShapeDtypeStruct(xs.shape, F32),
        compiler_params=_cparams("arbitrary", "arbitrary"),
        name="moe_experts",
    )(tile_expert, n_valid, xs, wg, wu, wd)


def _combine_kernel(pos_ref, y_ref, h_ref, route_ref, g_ref, out_ref, buf_ref, sem, *, tm, n_steps):
    i = pl.program_id(0)
    slot = i % 2

    def row_copy(step, sl, r, k):
        t = step * tm + r
        return pltpu.make_async_copy(y_ref.at[pl.ds(pos_ref[2 * t + k], 1)],
                                     buf_ref.at[sl, k, pl.ds(r, 1)], sem.at[sl])

    def issue(step, sl):
        def body(r, _):
            row_copy(step, sl, r, 0).start()
            row_copy(step, sl, r, 1).start()
            return 0
        lax.fori_loop(0, tm, body, 0, unroll=8)

    @pl.when(i == 0)
    def _():
        issue(0, 0)

    @pl.when(i + 1 < n_steps)
    def _():
        issue(i + 1, 1 - slot)

    for k in range(2):
        pltpu.make_async_copy(y_ref.at[pl.ds(0, tm)], buf_ref.at[slot, k], sem.at[slot]).wait()

    route = route_ref[...]
    out = h_ref[...] + route[:, 2:3] * buf_ref[slot, 0] + route[:, 3:4] * buf_ref[slot, 1]
    out_ref[...] = _rms(out, g_ref[...])


def _combine(y, pos, h, route, g):
    n_tok, d = h.shape
    tm = COMBINE_TM
    n_steps = n_tok // tm
    return pl.pallas_call(
        functools.partial(_combine_kernel, tm=tm, n_steps=n_steps),
        grid_spec=pltpu.PrefetchScalarGridSpec(
            num_scalar_prefetch=1,
            grid=(n_steps,),
            in_specs=[pl.BlockSpec(memory_space=pl.ANY),
                      pl.BlockSpec((tm, d), lambda i, pos: (i, 0)),
                      pl.BlockSpec((tm, LANES), lambda i, pos: (i, 0)),
                      pl.BlockSpec((1, d), lambda i, pos: (0, 0))],
            out_specs=pl.BlockSpec((tm, d), lambda i, pos: (i, 0)),
            scratch_shapes=[pltpu.VMEM((2, 2, tm, d), F32), pltpu.SemaphoreType.DMA((2,))]),
        out_shape=jax.ShapeDtypeStruct((n_tok, d), F32),
        compiler_params=_cparams("arbitrary"),
        name="moe_combine",
    )(pos, y, h, route, g)


def _moe_layer(h, hn, route, wg, wu, wd, g):
    n_tok = h.shape[0]
    pos, tile_expert, n_valid, pad_start, pad_len, n_tiles = _moe_plan(route, n_tok)
    xs = _dispatch(hn, pos, pad_start, pad_len, n_tiles * MOE_TM)
    y = _moe_experts(xs, tile_expert, n_valid, wg, wu, wd, n_tiles)
    return _combine(y, pos, h, route, g)


def _mla_layer(h, positions, mix_g, w_in, q_norm, w_uq, kv_norm, w_ukv, w_o, ffn_g, w_gate, w_up, w_down,
               next_g):
    batch, seq = positions.shape
    win_ext, wq_ext, wkv_ext = _mla_weights(w_in, w_uq, w_ukv)
    pos = positions.astype(F32).reshape(batch * seq, 1)
    q, k, v = _mla_proj(h, pos, mix_g.reshape(1, -1), win_ext, q_norm.reshape(1, -1),
                        kv_norm.reshape(1, -1), wq_ext, wkv_ext)
    o = _attention(q, k, v, batch, seq)
    return _proj_ffn(o, h, w_o.astype(BF16), ffn_g.reshape(1, -1), w_gate.astype(BF16), w_up.astype(BF16),
                     w_down.astype(BF16), next_g.reshape(1, -1))


def _s5_layer(h, hn, batch, seq, w_in, lam_re, lam_im, log_dt, b_re, b_im, c_re, c_im, d_skip, w_glu,
              ffn_g, w_router):
    u = _mm_split(hn, w_in.astype(BF16))
    kd, eb, wc, a_same, a_swap = _s5_operators(lam_re, lam_im, log_dt, b_re, b_im, c_re, c_im, d_skip)
    y = _s5_core(u, kd, eb, wc, a_same, a_swap, batch, seq)
    wr_pad = jnp.concatenate([w_router, jnp.zeros((D_MODEL, LANES - N_EXPERTS), w_router.dtype)], axis=1)
    wr_hi = wr_pad.astype(BF16)
    wr_lo = (wr_pad - wr_hi.astype(F32)).astype(BF16)
    return _glu_router(y, w_glu.astype(BF16), h, ffn_g.reshape(1, -1), jnp.concatenate([wr_hi, wr_lo], axis=1))


def kernel(x, positions, mix_norm, ffn_norm, final_norm, mla_w_in, mla_q_norm, mla_w_uq, mla_kv_norm, mla_w_ukv, mla_w_o, ffn_w_gate, ffn_w_up, ffn_w_down, s5_w_in, s5_lambda_re, s5_lambda_im, s5_log_dt, s5_b_re, s5_b_im, s5_c_re, s5_c_im, s5_d, s5_w_glu, moe_w_router, moe_w_gate, moe_w_up, moe_w_down):
    batch, seq, d = x.shape
    h0 = x.reshape(batch * seq, d)
    h2, hn2 = _mla_layer(h0, positions, mix_norm[0], mla_w_in[0], mla_q_norm[0], mla_w_uq[0],
                         mla_kv_norm[0], mla_w_ukv[0], mla_w_o[0], ffn_norm[0],
                         ffn_w_gate[0], ffn_w_up[0], ffn_w_down[0], mix_norm[1])
    h3, hn3, route = _s5_layer(h2, hn2, batch, seq, s5_w_in[0], s5_lambda_re[0], s5_lambda_im[0],
                              s5_log_dt[0], s5_b_re[0], s5_b_im[0], s5_c_re[0], s5_c_im[0], s5_d[0],
                              s5_w_glu[0], ffn_norm[1], moe_w_router[0])
    out = _moe_layer(h3, hn3, route, moe_w_gate[0].astype(BF16), moe_w_up[0].astype(BF16),
                     moe_w_down[0].astype(BF16), final_norm.reshape(1, -1))
    return out.reshape(batch, seq, d)
```

```python
import functools
import math

import jax
import jax.numpy as jnp
import numpy as np
from jax import lax
from jax.experimental import pallas as pl
from jax.experimental.pallas import tpu as pltpu

F32 = jnp.float32
BF16 = jnp.bfloat16

D_MODEL = 1024
MLA_HEADS = 16
QK_NOPE = 64
QK_ROPE = 32
V_DIM = 64
Q_LORA = 384
KV_LORA = 256
ROPE_THETA = 10000.0
HEAD_PAD = 128
ATTN_HEADS_PER_STEP = 4
S5_GROUP = 16
S5_GROUPS = D_MODEL // S5_GROUP
S5_STATE = 64
S5_CHUNK = 16
S5_SUPER = 8
N_SUPER = S5_GROUPS // S5_SUPER
N_EXPERTS = 8
EPS = 1e-6
NEG_INF = -1e30
LANES = 128
VMEM_LIMIT = 56 * 1024 * 1024


def _cparams(*sem):
    return pltpu.CompilerParams(dimension_semantics=sem, vmem_limit_bytes=VMEM_LIMIT)


def _rms(x, g):
    return x * lax.rsqrt(jnp.mean(x * x, axis=-1, keepdims=True) + EPS) * g


def _mla_proj_kernel(x_ref, pos_ref, g_ref, win_ref, qn_ref, kvn_ref, wq_ref, wkv_ref, tab_ref,
                     q_ref, k_ref, v_ref):
    hn = _rms(x_ref[...], g_ref[...]).astype(BF16)
    proj = jnp.dot(hn, win_ref[...], preferred_element_type=F32)
    cq = _rms(proj[:, :Q_LORA], qn_ref[...]).astype(BF16)
    ckv = _rms(proj[:, Q_LORA:Q_LORA + KV_LORA], kvn_ref[...]).astype(BF16)
    ka = proj[:, 640:768]
    kb = proj[:, 768:896]
    tab = tab_ref[...]
    ang = pos_ref[...] * tab[0:1, :]
    cos = jnp.cos(ang)
    sin = jnp.sin(ang)
    kp = ka * (cos * tab[1:2, :]) + kb * (sin * tab[2:3, :])
    tq = cos * tab[3:4, :] + sin * tab[4:5, :]
    q = jnp.dot(cq, wq_ref[...], preferred_element_type=F32)
    kv = jnp.dot(ckv, wkv_ref[...], preferred_element_type=F32)
    for h in range(MLA_HEADS):
        sl = slice(h * HEAD_PAD, (h + 1) * HEAD_PAD)
        q_ref[:, sl] = (q[:, sl] * tq).astype(BF16)
        k_ref[:, sl] = (kv[:, sl] + kp).astype(BF16)
    v = kv[:, MLA_HEADS * HEAD_PAD:]
    vlane = lax.broadcasted_iota(jnp.int32, v.shape, 1) % HEAD_PAD
    v_ref[...] = jnp.where(vlane == V_DIM, 1.0, v).astype(BF16)


def _mla_tables():
    inv_freq = ROPE_THETA ** (-np.arange(0, QK_ROPE, 2, dtype=np.float32) / QK_ROPE)
    inv_freq = inv_freq.astype(np.float32)
    half = QK_ROPE // 2
    tab = np.zeros((8, LANES), np.float32)
    scale = math.log2(math.e) / math.sqrt(QK_NOPE + QK_ROPE)
    tab[3, :QK_NOPE] = scale
    kc = [1.0, -1.0, 1.0, 1.0]
    ks = [-1.0, 1.0, 1.0, 1.0]
    for grp in range(4):
        sl = slice(QK_NOPE + grp * half, QK_NOPE + (grp + 1) * half)
        tab[0, sl] = inv_freq
        tab[1, sl] = kc[grp]
        tab[2, sl] = ks[grp]
        if grp % 2 == 0:
            tab[3, sl] = scale
        else:
            tab[4, sl] = scale
    return jnp.asarray(tab)


def _mla_weights(w_in, w_uq, w_ukv):
    half = QK_ROPE // 2
    base = Q_LORA + KV_LORA
    k1 = w_in[:, base:base + half]
    k2 = w_in[:, base + half:base + QK_ROPE]
    z64 = jnp.zeros((D_MODEL, QK_NOPE), w_in.dtype)
    win_ext = jnp.concatenate([w_in[:, :base], z64, k1, k1, k2, k2, z64, k2, k2, k1, k1], axis=1)
    wq = w_uq.reshape(Q_LORA, MLA_HEADS, QK_NOPE + QK_ROPE)
    q1 = wq[:, :, QK_NOPE:QK_NOPE + half]
    q2 = wq[:, :, QK_NOPE + half:]
    wq_ext = jnp.concatenate([wq[:, :, :QK_NOPE], q1, q2, q2, q1], axis=2)
    wq_ext = wq_ext.reshape(Q_LORA, MLA_HEADS * HEAD_PAD)
    wkv = w_ukv.reshape(KV_LORA, MLA_HEADS, QK_NOPE + V_DIM)
    zk = jnp.zeros((KV_LORA, MLA_HEADS, HEAD_PAD - QK_NOPE), w_ukv.dtype)
    zv = jnp.zeros((KV_LORA, MLA_HEADS, HEAD_PAD - V_DIM), w_ukv.dtype)
    wk_ext = jnp.concatenate([wkv[:, :, :QK_NOPE], zk], axis=2).reshape(KV_LORA, -1)
    wv_ext = jnp.concatenate([wkv[:, :, QK_NOPE:], zv], axis=2).reshape(KV_LORA, -1)
    wkv_ext = jnp.concatenate([wk_ext, wv_ext], axis=1)
    return win_ext.astype(BF16), wq_ext.astype(BF16), wkv_ext.astype(BF16)


def _mla_proj(x2d, pos, g, win_ext, qn, kvn, wq_ext, wkv_ext, tm=512):
    n = x2d.shape[0]
    hw = MLA_HEADS * HEAD_PAD
    full = lambda a: pl.BlockSpec(a.shape, lambda i: (0,) * a.ndim)
    tab = _mla_tables()
    return pl.pallas_call(
        _mla_proj_kernel,
        grid=(n // tm,),
        in_specs=[pl.BlockSpec((tm, D_MODEL), lambda i: (i, 0)),
                  pl.BlockSpec((tm, 1), lambda i: (i, 0)),
                  full(g), full(win_ext), full(qn), full(kvn), full(wq_ext), full(wkv_ext), full(tab)],
        out_specs=[pl.BlockSpec((tm, hw), lambda i: (i, 0))] * 3,
        out_shape=[jax.ShapeDtypeStruct((n, hw), BF16)] * 3,
        compiler_params=_cparams("parallel"),
        name="mla_proj",
    )(x2d, pos, g, win_ext, qn, kvn, wq_ext, wkv_ext, tab)


def _attn_kernel(q_ref, k_ref, v_ref, o_ref, s0_ref, s1_ref, p0_ref, p1_ref, *, blk):
    i = pl.program_id(2)
    heads = range(ATTN_HEADS_PER_STEP)
    lanes = [slice(h * HEAD_PAD, (h + 1) * HEAD_PAD) for h in heads]
    q = [q_ref[:, lanes[h]] for h in heads]

    def stage_a(j, s_ref):
        off = pl.multiple_of(j * blk, blk)
        for h in heads:
            s_ref[h] = lax.dot_general(q[h], k_ref[pl.ds(off, blk), lanes[h]], (((1,), (1,)), ((), ())),
                                       preferred_element_type=F32)

    def stage_b(state, s_ref, p_ref, masked=False):
        out = []
        for h in heads:
            m, _, acc = state[h]
            s = s_ref[h]
            if masked:
                row = lax.broadcasted_iota(jnp.int32, (blk, blk), 0)
                col = lax.broadcasted_iota(jnp.int32, (blk, blk), 1)
                s = jnp.where(col <= row, s, NEG_INF)
            m_new = jnp.maximum(m, jnp.max(s, axis=-1, keepdims=True))
            p_ref[h] = jnp.exp2((s - m_new).astype(BF16))
            out.append((m_new, jnp.exp2(m - m_new), acc))
        return tuple(out)

    def stage_c(j, state, p_ref):
        off = pl.multiple_of(j * blk, blk)
        out = []
        for h in heads:
            m, alpha, acc = state[h]
            pv = jnp.dot(p_ref[h], v_ref[pl.ds(off, blk), lanes[h]], preferred_element_type=F32)
            out.append((m, alpha, alpha * acc + pv))
        return tuple(out)

    def steady(j, state, even):
        s_j, s_n, p_j, p_n = (s0_ref, s1_ref, p0_ref, p1_ref) if even else (s1_ref, s0_ref, p1_ref, p0_ref)
        state = stage_c(j, state, p_j)
        stage_a(j + 2, s_j)
        return stage_b(state, s_n, p_n)

    def drain(state, even_last):
        s_i, p_i, p_prev = (s0_ref, p0_ref, p1_ref) if even_last else (s1_ref, p1_ref, p0_ref)
        state = stage_c(i - 1, state, p_prev)
        state = stage_b(state, s_i, p_i, masked=True)
        finish(stage_c(i, state, p_i))

    def finish(state):
        lane = lax.broadcasted_iota(jnp.int32, (blk, HEAD_PAD), 1)
        for t in range(ATTN_HEADS_PER_STEP // 2):
            lo, hi = (state[2 * t + e][2] for e in range(2))
            lo = lo / lo[:, V_DIM:V_DIM + 1]
            hi = pltpu.roll(hi / hi[:, V_DIM:V_DIM + 1], V_DIM, 1)
            o_ref[:, lanes[t]] = jnp.where(lane < V_DIM, lo, hi).astype(o_ref.dtype)

    init = tuple((jnp.full((blk, 1), NEG_INF, F32), jnp.ones((blk, 1), F32), jnp.zeros((blk, HEAD_PAD), F32))
                 for _ in heads)
    stage_a(0, s0_ref)

    @pl.when(i == 0)
    def _():
        finish(stage_c(0, stage_b(init, s0_ref, p0_ref, masked=True), p0_ref))

    @pl.when(i > 0)
    def _():
        stage_a(1, s1_ref)
        state = stage_b(init, s0_ref, p0_ref)

        def run(j0, count, st):
            for t in range(count):
                st = steady(j0 + t, st, even=(t % 2 == 0))
            return st

        n_steady = i - 1
        state = lax.fori_loop(0, n_steady // 4, lambda g, st: run(4 * g, 4, st), state)
        j_rest = 4 * (n_steady // 4)
        for rest in range(4):
            @pl.when(n_steady % 4 == rest)
            def _(rest=rest):
                drain(run(j_rest, rest, state), even_last=(rest % 2 == 1))


def _attention(q, k, v, batch, seq, blk=512):
    n, hw = q.shape
    nq = seq // blk
    nh = ATTN_HEADS_PER_STEP
    width = nh * HEAD_PAD
    return pl.pallas_call(
        functools.partial(_attn_kernel, blk=blk),
        grid=(batch, MLA_HEADS // nh, nq),
        in_specs=[pl.BlockSpec((blk, width), lambda b, h, i: (b * nq + i, h)),
                  pl.BlockSpec((seq, width), lambda b, h, i: (b, h), pipeline_mode=pl.Buffered(1)),
                  pl.BlockSpec((seq, width), lambda b, h, i: (b, h), pipeline_mode=pl.Buffered(1))],
        out_specs=pl.BlockSpec((blk, nh * V_DIM), lambda b, h, i: (b * nq + i, h)),
        out_shape=jax.ShapeDtypeStruct((n, MLA_HEADS * V_DIM), BF16),
        scratch_shapes=[pltpu.VMEM((nh, blk, blk), F32), pltpu.VMEM((nh, blk, blk), F32),
                        pltpu.VMEM((nh, blk, blk), BF16), pltpu.VMEM((nh, blk, blk), BF16)],
        compiler_params=_cparams("parallel", "parallel", "arbitrary"),
        name="mla_attention",
    )(q, k, v)


def _proj_ffn_kernel(o_ref, res_ref, wo_ref, g1_ref, wg_ref, wu_ref, wd_ref, g2_ref, out_ref, outn_ref):
    h1 = res_ref[...] + jnp.dot(o_ref[...], wo_ref[...], preferred_element_type=F32)
    hn = _rms(h1, g1_ref[...]).astype(BF16)
    gate = jnp.dot(hn, wg_ref[...], preferred_element_type=F32)
    up = jnp.dot(hn, wu_ref[...], preferred_element_type=F32)
    act = (jax.nn.silu(gate) * up).astype(BF16)
    out = h1 + jnp.dot(act, wd_ref[...], preferred_element_type=F32)
    out_ref[...] = out
    outn_ref[...] = _rms(out, g2_ref[...]).astype(outn_ref.dtype)


def _proj_ffn(o, res, wo, g1, wg, wu, wd, g2, tm=512):
    n, d = res.shape
    row = lambda a: pl.BlockSpec((tm, a.shape[1]), lambda i: (i, 0))
    resident = lambda a: pl.BlockSpec(a.shape, lambda i: (0, 0), pipeline_mode=pl.Buffered(1))
    return pl.pallas_call(
        _proj_ffn_kernel,
        grid=(n // tm,),
        in_specs=[row(o), row(res), resident(wo), resident(g1), resident(wg), resident(wu), resident(wd),
                  resident(g2)],
        out_specs=[pl.BlockSpec((tm, d), lambda i: (i, 0))] * 2,
        out_shape=[jax.ShapeDtypeStruct((n, d), F32), jax.ShapeDtypeStruct((n, d), BF16)],
        compiler_params=_cparams("parallel"),
        name="proj_swiglu_dense",
    )(o, res, wo, g1, wg, wu, wd, g2)


def _mm_split_kernel(a_ref, w_ref, o_ref, r_ref):
    tm = a_ref.shape[0]
    r = jnp.dot(a_ref[...], w_ref[...], preferred_element_type=F32)
    for g in range(N_SUPER):
        r_ref[g] = r[:, g * LANES:(g + 1) * LANES]
    for g in range(N_SUPER):
        for l in range(S5_CHUNK):
            piece = r_ref[g, pl.ds(l, tm // S5_CHUNK, stride=S5_CHUNK), :]
            o_ref[g, :, l * LANES:(l + 1) * LANES] = piece.astype(o_ref.dtype)


def _mm_split(a, w, tm=512):
    n, k = a.shape
    return pl.pallas_call(
        _mm_split_kernel,
        grid=(n // tm,),
        in_specs=[pl.BlockSpec((tm, k), lambda i: (i, 0)),
                  pl.BlockSpec(w.shape, lambda i: (0, 0))],
        out_specs=pl.BlockSpec((N_SUPER, tm // S5_CHUNK, S5_CHUNK * LANES), lambda i: (0, i, 0)),
        out_shape=jax.ShapeDtypeStruct((N_SUPER, n // S5_CHUNK, S5_CHUNK * LANES), BF16),
        scratch_shapes=[pltpu.VMEM((N_SUPER, tm, LANES), F32)],
        compiler_params=_cparams("parallel"),
        name="s5_in_proj",
    )(a, w)


def _s5_param_kernel(lr_ref, li_ref, ldt_ref, btr_ref, bti_ref, cr_ref, ci_ref, d_ref,
                     kt_ref, er_ref, ei_ref, wr_ref, wi_ref, ac_ref):
    L = S5_CHUNK
    lr = lr_ref[...]
    li = li_ref[...]
    dt = jnp.exp(ldt_ref[...])
    mag = jnp.exp(lr * dt)
    ar = mag * jnp.cos(li * dt)
    ai = mag * jnp.sin(li * dt)
    den = lr * lr + li * li
    gr = ((ar - 1.0) * lr + ai * li) / den
    gi = (ai * lr - (ar - 1.0) * li) / den
    btr = btr_ref[...]
    bti = bti_ref[...]
    bbr = gr * btr - gi * bti
    bbi = gr * bti + gi * btr
    kf = lax.broadcasted_iota(jnp.int32, (L, 1), 0).astype(F32)
    pmag = jnp.exp(lr * dt * kf)
    pr = pmag * jnp.cos(li * dt * kf)
    pi = pmag * jnp.sin(li * dt * kf)
    cr = cr_ref[...]
    ci = ci_ref[...]
    pr3 = pr[:, None, :]
    pi3 = pi[:, None, :]
    wr = cr[None] * pr3 - ci[None] * pi3
    wi = cr[None] * pi3 + ci[None] * pr3
    dn = (((1,), (1,)), ((), ()))
    hp = lax.Precision.HIGHEST
    kt = (lax.dot_general(wr.reshape(L * S5_GROUP, S5_STATE), bbr, dn, precision=hp,
                          preferred_element_type=F32)
          - lax.dot_general(wi.reshape(L * S5_GROUP, S5_STATE), bbi, dn, precision=hp,
                            preferred_element_type=F32))
    row = lax.broadcasted_iota(jnp.int32, kt.shape, 0)
    col = lax.broadcasted_iota(jnp.int32, kt.shape, 1)
    kt_ref[...] = kt + jnp.where(row == col, d_ref[...], 0.0)
    er_ref[...] = pr3 * bbr[None] - pi3 * bbi[None]
    ei_ref[...] = pr3 * bbi[None] + pi3 * bbr[None]
    wr_ref[...] = wr * ar - wi * ai
    wi_ref[...] = wr * ai + wi * ar
    lastr = pr[L - 1:L, :]
    lasti = pi[L - 1:L, :]
    ac_ref[0:1, :] = lastr * ar - lasti * ai
    ac_ref[1:2, :] = lastr * ai + lasti * ar


def _s5_operators(lam_re, lam_im, log_dt, b_re, b_im, c_re, c_im, d_skip):
    G, P, H, L = S5_GROUPS, S5_STATE, S5_GROUP, S5_CHUNK
    per_g = lambda *shape: pl.BlockSpec((None,) + shape, lambda g: (g,) + (0,) * len(shape))
    kt, er, ei, wr, wi, ac = pl.pallas_call(
        _s5_param_kernel,
        grid=(G,),
        in_specs=[per_g(1, P), per_g(1, P), per_g(1, 1), per_g(H, P), per_g(H, P),
                  per_g(H, P), per_g(H, P), per_g(1, H)],
        out_specs=[per_g(L * H, H), per_g(L, H, P), per_g(L, H, P), per_g(L, H, P), per_g(L, H, P),
                   per_g(2, P)],
        out_shape=[jax.ShapeDtypeStruct((G, L * H, H), F32)]
        + [jax.ShapeDtypeStruct((G, L, H, P), F32)] * 4
        + [jax.ShapeDtypeStruct((G, 2, P), F32)],
        compiler_params=_cparams("parallel"),
        name="s5_operators",
    )(lam_re.reshape(G, 1, P), lam_im.reshape(G, 1, P), log_dt.reshape(G, 1, 1),
      b_re.transpose(0, 2, 1), b_im.transpose(0, 2, 1), c_re, c_im, d_skip.reshape(G, 1, H))

    ns, sg = N_SUPER, S5_SUPER
    eye = jnp.eye(sg, dtype=F32)
    ktg = kt.reshape(ns, sg, L, H, H)
    kd = jnp.einsum('SgtOI,gk->StgIkO', ktg, eye).reshape(ns, L, LANES, LANES)
    ecat = jnp.stack([er, ei], axis=3)[:, ::-1]
    eb = ecat.reshape(ns, sg, L, H, 2, P).transpose(0, 2, 1, 3, 4, 5).reshape(ns, L, LANES, LANES)
    wcat = jnp.stack([wr, -wi], axis=3).reshape(ns, sg, L, H, 2, P)
    wc = wcat.transpose(0, 2, 4, 5, 1, 3).reshape(ns, L, LANES, LANES)
    acr = ac[:, 0, :].reshape(ns, sg, P)
    aci = ac[:, 1, :].reshape(ns, sg, P)
    a_same = jnp.concatenate([acr, acr], axis=-1)
    a_swap = jnp.concatenate([-aci, aci], axis=-1)
    return kd.astype(BF16), eb.astype(BF16), wc.astype(BF16), a_same, a_swap


def _group_of(shape, axis):
    return lax.broadcasted_iota(jnp.int32, shape, axis) // S5_GROUP


def _s5_inc_kernel(u_ref, eb_ref, o_ref, b_ref):
    tr = u_ref.shape[0]

    @pl.when(pl.program_id(1) == 0)
    def _():
        row_grp = _group_of((LANES, LANES), 0)
        for l in range(S5_CHUNK):
            blk = eb_ref[l]
            for g in range(S5_SUPER):
                b_ref[l * LANES:(l + 1) * LANES, g * LANES:(g + 1) * LANES] = jnp.where(
                    row_grp == g, blk, jnp.zeros_like(blk))

    inc = jnp.dot(u_ref[...], b_ref[...], preferred_element_type=F32)
    for g in range(S5_SUPER):
        o_ref[pl.ds(g, tr, stride=S5_SUPER), :] = inc[:, g * LANES:(g + 1) * LANES]


def _s5_scan_kernel(inc_ref, asame_ref, aswap_ref, x_ref, state_ref, *, cb):
    @pl.when(pl.program_id(0) == 0)
    def _():
        state_ref[...] = jnp.zeros_like(state_ref)

    nseq = state_ref.shape[0]
    a_same = asame_ref[...].reshape(nseq * S5_SUPER, LANES)
    a_swap = aswap_ref[...].reshape(nseq * S5_SUPER, LANES)

    def body(c, x):
        x_ref[:, c] = x.reshape(nseq, S5_SUPER, LANES)
        inc = inc_ref[:, c].reshape(nseq * S5_SUPER, LANES)
        return x * a_same + pltpu.roll(x, LANES // 2, 1) * a_swap + inc

    x0 = state_ref[...].reshape(nseq * S5_SUPER, LANES)
    state_ref[...] = lax.fori_loop(0, cb, body, x0).reshape(nseq, S5_SUPER, LANES)


def _s5_out_kernel(u_ref, x_ref, kd_ref, wc_ref, y_ref, ys_ref, t_ref, c_ref):
    tr = u_ref.shape[0]

    @pl.when(pl.program_id(1) == 0)
    def _():
        zero = jnp.zeros((LANES, LANES), t_ref.dtype)
        col_grp = _group_of((LANES, LANES), 1)
        for l_out in range(S5_CHUNK):
            cols = slice(l_out * LANES, (l_out + 1) * LANES)
            for l_in in range(S5_CHUNK):
                t_ref[l_in * LANES:(l_in + 1) * LANES, cols] = kd_ref[l_out - l_in] if l_out >= l_in else zero
            blk = wc_ref[l_out]
            for g in range(S5_SUPER):
                c_ref[g * LANES:(g + 1) * LANES, cols] = jnp.where(col_grp == g, blk, zero)

    x = jnp.concatenate([x_ref[pl.ds(g, tr, stride=S5_SUPER), :] for g in range(S5_SUPER)], axis=-1)
    u = u_ref[...]
    pair = 2 * LANES
    y = jnp.concatenate(
        [jnp.dot(u[:, :(b + 1) * pair], t_ref[:(b + 1) * pair, b * pair:(b + 1) * pair],
                 preferred_element_type=F32) for b in range(S5_CHUNK // 2)], axis=-1)
    y += jnp.dot(x.astype(BF16), c_ref[...], preferred_element_type=F32)
    y = jax.nn.gelu(y)
    for l in range(S5_CHUNK):
        ys_ref[pl.ds(l, tr, stride=S5_CHUNK), :] = y[:, l * LANES:(l + 1) * LANES]
    y_ref[...] = ys_ref[...].astype(y_ref.dtype)


def _s5_core(uc, kd, eb, wc, a_same, a_swap, batch, seq, tr=512):
    ns, rows, width = uc.shape
    L = S5_CHUNK
    n = rows * L
    n_chunks = seq // L
    sw = 2 * S5_SUPER * S5_STATE
    tr = min(tr, rows)
    blocks_spec = pl.BlockSpec((None, L, LANES, LANES), lambda s, r: (s, 0, 0, 0))
    inc = pl.pallas_call(
        _s5_inc_kernel,
        grid=(ns, rows // tr),
        in_specs=[pl.BlockSpec((None, tr, width), lambda s, r: (s, r, 0)), blocks_spec],
        out_specs=pl.BlockSpec((None, tr * S5_SUPER, LANES), lambda s, r: (s, r, 0)),
        out_shape=jax.ShapeDtypeStruct((ns, rows * S5_SUPER, LANES), F32),
        scratch_shapes=[pltpu.VMEM((width, sw), BF16)],
        compiler_params=_cparams("parallel", "arbitrary"),
        name="s5_chunk_state",
    )(uc, eb)
    nseq = ns * batch
    cb = min(32, n_chunks)
    tile = (S5_SUPER, LANES)
    bcast = lambda a: jnp.broadcast_to(a[:, None], (ns, batch) + tile).reshape((nseq,) + tile)
    seq_spec = pl.BlockSpec((nseq, cb) + tile, lambda c: (0, c, 0, 0))
    tab_spec = pl.BlockSpec((nseq,) + tile, lambda c: (0, 0, 0))
    xprev = pl.pallas_call(
        functools.partial(_s5_scan_kernel, cb=cb),
        grid=(n_chunks // cb,),
        in_specs=[seq_spec, tab_spec, tab_spec],
        out_specs=seq_spec,
        out_shape=jax.ShapeDtypeStruct((nseq, n_chunks) + tile, F32),
        scratch_shapes=[pltpu.VMEM((nseq,) + tile, F32)],
        compiler_params=_cparams("arbitrary"),
        name="s5_chunk_scan",
    )(inc.reshape((nseq, n_chunks) + tile), bcast(a_same), bcast(a_swap))
    xprev = xprev.reshape(ns, rows * S5_SUPER, LANES)
    return pl.pallas_call(
        _s5_out_kernel,
        grid=(ns, rows // tr),
        in_specs=[pl.BlockSpec((None, tr, width), lambda s, r: (s, r, 0)),
                  pl.BlockSpec((None, tr * S5_SUPER, LANES), lambda s, r: (s, r, 0)),
                  blocks_spec, blocks_spec],
        out_specs=pl.BlockSpec((None, tr * L, LANES), lambda s, r: (s, r, 0)),
        out_shape=jax.ShapeDtypeStruct((ns, n, LANES), BF16),
        scratch_shapes=[pltpu.VMEM((tr * L, LANES), F32), pltpu.VMEM((width, width), BF16),
                        pltpu.VMEM((sw, width), BF16)],
        compiler_params=_cparams("parallel", "arbitrary"),
        name="s5_chunk_out",
    )(uc, xprev, kd, wc)


def _glu_router_kernel(y_ref, w_ref, res_ref, g_ref, wr_ref, tri_ref, h_ref, hn_ref, route_ref, cnt_ref,
                       count_ref):
    d = res_ref.shape[1]
    y = jnp.concatenate([y_ref[g] for g in range(N_SUPER)], axis=-1)
    z = jnp.dot(y, w_ref[...], preferred_element_type=F32)
    h = res_ref[...] + z[:, :d] * jax.nn.sigmoid(z[:, d:])
    h_ref[...] = h
    hn = _rms(h, g_ref[...])
    hn_ref[...] = hn.astype(hn_ref.dtype)
    hi = hn.astype(BF16)
    lo = (hn - hi.astype(F32)).astype(BF16)
    wr = wr_ref[...]
    first = jnp.dot(hi, wr, preferred_element_type=F32)
    logits = (first[:, :LANES] + first[:, LANES:]) + jnp.dot(lo, wr[:, :LANES], preferred_element_type=F32)
    lane = lax.broadcasted_iota(jnp.int32, logits.shape, 1)
    logits = jnp.where(lane < N_EXPERTS, logits, -jnp.inf)
    m1 = jnp.max(logits, axis=-1, keepdims=True)
    i1 = jnp.min(jnp.where(logits == m1, lane, LANES), axis=-1, keepdims=True)
    rest = jnp.where(lane == i1, -jnp.inf, logits)
    m2 = jnp.max(rest, axis=-1, keepdims=True)
    i2 = jnp.min(jnp.where(rest == m2, lane, LANES), axis=-1, keepdims=True)
    e2 = jnp.exp(m2 - m1)
    g1 = 1.0 / (1.0 + e2)

    @pl.when(pl.program_id(0) == 0)
    def _():
        count_ref[...] = jnp.zeros_like(count_ref)

    chosen = jnp.where(jnp.logical_or(lane == i1, lane == i2), 1.0, 0.0)
    before = jnp.dot(tri_ref[...], chosen.astype(BF16), preferred_element_type=F32) + count_ref[0:1, :]
    r1 = jnp.sum(jnp.where(lane == i1, before, 0.0), axis=-1, keepdims=True)
    r2 = jnp.sum(jnp.where(lane == i2, before, 0.0), axis=-1, keepdims=True)
    count_ref[0:1, :] += jnp.sum(chosen, axis=0, keepdims=True)
    cnt_ref[...] = count_ref[...]
    route_ref[...] = (jnp.where(lane == 0, i1.astype(F32), 0.0) + jnp.where(lane == 1, i2.astype(F32), 0.0)
                      + jnp.where(lane == 2, g1, 0.0) + jnp.where(lane == 3, e2 * g1, 0.0)
                      + jnp.where(lane == 4, r1, 0.0) + jnp.where(lane == 5, r2, 0.0))


def _glu_router(y, w_glu, res, g, w_router_pad, tm=512):
    ns, n, _ = y.shape
    d = res.shape[1]
    tri = jnp.asarray(np.tril(np.ones((tm, tm), np.float32), -1), BF16)
    return pl.pallas_call(
        _glu_router_kernel,
        grid=(n // tm,),
        in_specs=[pl.BlockSpec((ns, tm, LANES), lambda i: (0, i, 0)),
                  pl.BlockSpec(w_glu.shape, lambda i: (0, 0)),
                  pl.BlockSpec((tm, d), lambda i: (i, 0)),
                  pl.BlockSpec((1, d), lambda i: (0, 0)),
                  pl.BlockSpec(w_router_pad.shape, lambda i: (0, 0)),
                  pl.BlockSpec((tm, tm), lambda i: (0, 0))],
        out_specs=[pl.BlockSpec((tm, d), lambda i: (i, 0)),
                   pl.BlockSpec((tm, d), lambda i: (i, 0)),
                   pl.BlockSpec((tm, LANES), lambda i: (i, 0)),
                   pl.BlockSpec((8, LANES), lambda i: (0, 0))],
        out_shape=[jax.ShapeDtypeStruct((n, d), F32), jax.ShapeDtypeStruct((n, d), F32),
                   jax.ShapeDtypeStruct((n, LANES), F32), jax.ShapeDtypeStruct((8, LANES), F32)],
        scratch_shapes=[pltpu.VMEM((8, LANES), F32)],
        compiler_params=_cparams("arbitrary"),
        name="s5_glu_router",
    )(y, w_glu, res, g, w_router_pad, tri)


MOE_TM = 512
MOE_TF = 1792
DISPATCH_BLOCK = 512
COMBINE_TM = 512


def _moe_plan(route, counts, n_tok):
    ne, tm = N_EXPERTS, MOE_TM
    n_tiles = (2 * n_tok) // tm + ne
    eid = route[:, :2].astype(jnp.int32).reshape(-1)
    rank = route[:, 4:6].astype(jnp.int32).reshape(-1)
    onehot = (eid[:, None] == jnp.arange(ne, dtype=jnp.int32)[None, :]).astype(jnp.int32)
    cnt = counts[0, :ne].astype(jnp.int32)
    padded = ((cnt + tm - 1) // tm) * tm
    ends = jnp.cumsum(padded)
    off = ends - padded
    pos = (jnp.sum(off[None, :] * onehot, axis=1) + rank).astype(jnp.int32)
    tile_end = ends // tm
    n_valid = tile_end[-1:].astype(jnp.int32)
    tiles = jnp.arange(n_tiles, dtype=jnp.int32)
    tile_expert = jnp.minimum(jnp.sum(tiles[:, None] >= tile_end[None, :], axis=1), ne - 1).astype(jnp.int32)
    pad_start = (off + cnt).astype(jnp.int32)
    pad_len = (padded - cnt).astype(jnp.int32)
    return pos, tile_expert, n_valid, pad_start, pad_len, n_tiles


def _dispatch_kernel(pos_ref, pstart_ref, plen_ref, hn_ref, xs_ref, zero_ref, sem, *, n_tok):
    nb = n_tok // DISPATCH_BLOCK
    b = pl.program_id(0)

    def row_copy(r, k):
        dst = pos_ref[2 * (b * DISPATCH_BLOCK + r) + k]
        return pltpu.make_async_copy(hn_ref.at[pl.ds(r, 1)], xs_ref.at[pl.ds(dst, 1)], sem)

    def issue(r, _):
        row_copy(r, 0).start()
        row_copy(r, 1).start()
        return 0
    lax.fori_loop(0, DISPATCH_BLOCK, issue, 0, unroll=8)

    for _ in range(2):
        pltpu.make_async_copy(hn_ref, xs_ref.at[pl.ds(0, DISPATCH_BLOCK)], sem).wait()

    @pl.when(b == nb - 1)
    def _():
        zero_ref[...] = jnp.zeros_like(zero_ref)

        def pad_copy(e, r):
            return pltpu.make_async_copy(zero_ref.at[pl.ds(0, 1)], xs_ref.at[pl.ds(pstart_ref[e] + r, 1)], sem)

        for e in range(N_EXPERTS):
            def pbody(r, _, e=e):
                pad_copy(e, r).start()
                return 0
            lax.fori_loop(0, plen_ref[e], pbody, 0)
        for e in range(N_EXPERTS):
            def wbody(r, _, e=e):
                pad_copy(e, r).wait()
                return 0
            lax.fori_loop(0, plen_ref[e], wbody, 0)


def _dispatch(hn, pos, pad_start, pad_len, n_rows):
    n_tok, d = hn.shape
    return pl.pallas_call(
        functools.partial(_dispatch_kernel, n_tok=n_tok),
        grid_spec=pltpu.PrefetchScalarGridSpec(
            num_scalar_prefetch=3,
            grid=(n_tok // DISPATCH_BLOCK,),
            in_specs=[pl.BlockSpec((DISPATCH_BLOCK, d), lambda b, pos, ps, plen: (b, 0))],
            out_specs=pl.BlockSpec(memory_space=pl.ANY),
            scratch_shapes=[pltpu.VMEM((8, d), F32), pltpu.SemaphoreType.DMA]),
        out_shape=jax.ShapeDtypeStruct((n_rows, d), F32),
        compiler_params=_cparams("arbitrary"),
        name="moe_dispatch",
    )(pos, pad_start, pad_len, hn)


def _moe_kernel(te_ref, nv_ref, x_ref, wg_ref, wu_ref, wd_ref, y_ref, xb_ref):
    i = pl.program_id(0)
    k = pl.program_id(1)

    @pl.when(i < nv_ref[0])
    def _():
        @pl.when(k == 0)
        def _():
            xb_ref[...] = x_ref[...].astype(BF16)

        hn = xb_ref[...]
        gate = jnp.dot(hn, wg_ref[...], preferred_element_type=F32)
        up = jnp.dot(hn, wu_ref[...], preferred_element_type=F32)
        act = (jax.nn.silu(gate) * up).astype(BF16)
        part = jnp.dot(act, wd_ref[...], preferred_element_type=F32)

        @pl.when(k == 0)
        def _():
            y_ref[...] = part

        @pl.when(k > 0)
        def _():
            y_ref[...] += part


def _moe_experts(xs, tile_expert, n_valid, wg, wu, wd, n_tiles):
    tm, tf = MOE_TM, MOE_TF
    d = xs.shape[1]
    f = wg.shape[2]
    nk = f // tf

    def row_map(i, k, te, nv):
        return (jnp.minimum(i, nv[0] - 1), 0)

    def kk(i, k, nv):
        return jnp.where(i < nv[0], k, nk - 1)

    def w_in_map(i, k, te, nv):
        return (te[jnp.minimum(i, nv[0] - 1)], 0, kk(i, k, nv))

    def w_out_map(i, k, te, nv):
        return (te[jnp.minimum(i, nv[0] - 1)], kk(i, k, nv), 0)

    return pl.pallas_call(
        _moe_kernel,
        grid_spec=pltpu.PrefetchScalarGridSpec(
            num_scalar_prefetch=2,
            grid=(n_tiles, nk),
            in_specs=[pl.BlockSpec((tm, d), row_map),
                      pl.BlockSpec((None, d, tf), w_in_map),
                      pl.BlockSpec((None, d, tf), w_in_map),
                      pl.BlockSpec((None, tf, d), w_out_map)],
            out_specs=pl.BlockSpec((tm, d), row_map),
            scratch_shapes=[pltpu.VMEM((tm, d), BF16)]),
        out_shape=jax.ShapeDtypeStruct(xs.shape, F32),
        compiler_params=_cparams("arbitrary", "arbitrary"),
        name="moe_experts",
    )(tile_expert, n_valid, xs, wg, wu, wd)


def _combine_kernel(pos_ref, y_ref, h_ref, route_ref, g_ref, out_ref, buf_ref, sem, *, tm, n_steps):
    i = pl.program_id(0)
    slot = i % 2

    def row_copy(step, sl, r, k):
        t = step * tm + r
        return pltpu.make_async_copy(y_ref.at[pl.ds(pos_ref[2 * t + k], 1)],
                                     buf_ref.at[sl, k, pl.ds(r, 1)], sem.at[sl])

    def issue(step, sl):
        def body(r, _):
            row_copy(step, sl, r, 0).start()
            row_copy(step, sl, r, 1).start()
            return 0
        lax.fori_loop(0, tm, body, 0, unroll=8)

    @pl.when(i == 0)
    def _():
        issue(0, 0)

    @pl.when(i + 1 < n_steps)
    def _():
        issue(i + 1, 1 - slot)

    for k in range(2):
        pltpu.make_async_copy(y_ref.at[pl.ds(0, tm)], buf_ref.at[slot, k], sem.at[slot]).wait()

    route = route_ref[...]
    out = h_ref[...] + route[:, 2:3] * buf_ref[slot, 0] + route[:, 3:4] * buf_ref[slot, 1]
    out_ref[...] = _rms(out, g_ref[...])


def _combine(y, pos, h, route, g):
    n_tok, d = h.shape
    tm = COMBINE_TM
    n_steps = n_tok // tm
    return pl.pallas_call(
        functools.partial(_combine_kernel, tm=tm, n_steps=n_steps),
        grid_spec=pltpu.PrefetchScalarGridSpec(
            num_scalar_prefetch=1,
            grid=(n_steps,),
            in_specs=[pl.BlockSpec(memory_space=pl.ANY),
                      pl.BlockSpec((tm, d), lambda i, pos: (i, 0)),
                      pl.BlockSpec((tm, LANES), lambda i, pos: (i, 0)),
                      pl.BlockSpec((1, d), lambda i, pos: (0, 0))],
            out_specs=pl.BlockSpec((tm, d), lambda i, pos: (i, 0)),
            scratch_shapes=[pltpu.VMEM((2, 2, tm, d), F32), pltpu.SemaphoreType.DMA((2,))]),
        out_shape=jax.ShapeDtypeStruct((n_tok, d), F32),
        compiler_params=_cparams("arbitrary"),
        name="moe_combine",
    )(pos, y, h, route, g)


def _moe_layer(h, hn, route, counts, wg, wu, wd, g):
    n_tok = h.shape[0]
    pos, tile_expert, n_valid, pad_start, pad_len, n_tiles = _moe_plan(route, counts, n_tok)
    xs = _dispatch(hn, pos, pad_start, pad_len, n_tiles * MOE_TM)
    y = _moe_experts(xs, tile_expert, n_valid, wg, wu, wd, n_tiles)
    return _combine(y, pos, h, route, g)


def _mla_layer(h, positions, mix_g, w_in, q_norm, w_uq, kv_norm, w_ukv, w_o, ffn_g, w_gate, w_up, w_down,
               next_g):
    batch, seq = positions.shape
    win_ext, wq_ext, wkv_ext = _mla_weights(w_in, w_uq, w_ukv)
    pos = positions.astype(F32).reshape(batch * seq, 1)
    q, k, v = _mla_proj(h, pos, mix_g.reshape(1, -1), win_ext, q_norm.reshape(1, -1),
                        kv_norm.reshape(1, -1), wq_ext, wkv_ext)
    o = _attention(q, k, v, batch, seq)
    return _proj_ffn(o, h, w_o.astype(BF16), ffn_g.reshape(1, -1), w_gate.astype(BF16), w_up.astype(BF16),
                     w_down.astype(BF16), next_g.reshape(1, -1))


def _s5_layer(h, hn, batch, seq, w_in, lam_re, lam_im, log_dt, b_re, b_im, c_re, c_im, d_skip, w_glu,
              ffn_g, w_router):
    u = _mm_split(hn, w_in.astype(BF16))
    kd, eb, wc, a_same, a_swap = _s5_operators(lam_re, lam_im, log_dt, b_re, b_im, c_re, c_im, d_skip)
    y = _s5_core(u, kd, eb, wc, a_same, a_swap, batch, seq)
    wr_pad = jnp.concatenate([w_router, jnp.zeros((D_MODEL, LANES - N_EXPERTS), w_router.dtype)], axis=1)
    wr_hi = wr_pad.astype(BF16)
    wr_lo = (wr_pad - wr_hi.astype(F32)).astype(BF16)
    return _glu_router(y, w_glu.astype(BF16), h, ffn_g.reshape(1, -1), jnp.concatenate([wr_hi, wr_lo], axis=1))


def kernel(x, positions, mix_norm, ffn_norm, final_norm, mla_w_in, mla_q_norm, mla_w_uq, mla_kv_norm, mla_w_ukv, mla_w_o, ffn_w_gate, ffn_w_up, ffn_w_down, s5_w_in, s5_lambda_re, s5_lambda_im, s5_log_dt, s5_b_re, s5_b_im, s5_c_re, s5_c_im, s5_d, s5_w_glu, moe_w_router, moe_w_gate, moe_w_up, moe_w_down):
    batch, seq, d = x.shape
    h0 = x.reshape(batch * seq, d)
    h2, hn2 = _mla_layer(h0, positions, mix_norm[0], mla_w_in[0], mla_q_norm[0], mla_w_uq[0],
                         mla_kv_norm[0], mla_w_ukv[0], mla_w_o[0], ffn_norm[0],
                         ffn_w_gate[0], ffn_w_up[0], ffn_w_down[0], mix_norm[1])
    h3, hn3, route, counts = _s5_layer(h2, hn2, batch, seq, s5_w_in[0], s5_lambda_re[0], s5_lambda_im[0],
                                      s5_log_dt[0], s5_b_re[0], s5_b_im[0], s5_c_re[0], s5_c_im[0], s5_d[0],
                                      s5_w_glu[0], ffn_norm[1], moe_w_router[0])
    out = _moe_layer(h3, hn3, route, counts, moe_w_gate[0].astype(BF16), moe_w_up[0].astype(BF16),
                     moe_w_down[0].astype(BF16), final_norm.reshape(1, -1))
    return out.reshape(batch, seq, d)
```

```python
import functools
import math

import jax
import jax.numpy as jnp
import numpy as np
from jax import lax
from jax.experimental import pallas as pl
from jax.experimental.pallas import tpu as pltpu

F32 = jnp.float32
BF16 = jnp.bfloat16

D_MODEL = 1024
MLA_HEADS = 16
QK_NOPE = 64
QK_ROPE = 32
V_DIM = 64
Q_LORA = 384
KV_LORA = 256
ROPE_THETA = 10000.0
HEAD_PAD = 128
ATTN_HEADS_PER_STEP = 4
S5_GROUP = 16
S5_GROUPS = D_MODEL // S5_GROUP
S5_STATE = 64
S5_CHUNK = 16
S5_SUPER = 8
N_SUPER = S5_GROUPS // S5_SUPER
N_EXPERTS = 8
EPS = 1e-6
NEG_INF = -1e30
LANES = 128
VMEM_LIMIT = 56 * 1024 * 1024


def _cparams(*sem):
    return pltpu.CompilerParams(dimension_semantics=sem, vmem_limit_bytes=VMEM_LIMIT)


def _rms(x, g):
    return x * lax.rsqrt(jnp.mean(x * x, axis=-1, keepdims=True) + EPS) * g


def _mla_proj_kernel(x_ref, pos_ref, g_ref, win_ref, qn_ref, kvn_ref, wq_ref, wkv_ref, tab_ref,
                     q_ref, k_ref, v_ref):
    hn = _rms(x_ref[...], g_ref[...]).astype(BF16)
    proj = jnp.dot(hn, win_ref[...], preferred_element_type=F32)
    cq = _rms(proj[:, :Q_LORA], qn_ref[...]).astype(BF16)
    ckv = _rms(proj[:, Q_LORA:Q_LORA + KV_LORA], kvn_ref[...]).astype(BF16)
    ka = proj[:, 640:768]
    kb = proj[:, 768:896]
    tab = tab_ref[...]
    ang = pos_ref[...] * tab[0:1, :]
    cos = jnp.cos(ang)
    sin = jnp.sin(ang)
    kp = ka * (cos * tab[1:2, :]) + kb * (sin * tab[2:3, :])
    tq = cos * tab[3:4, :] + sin * tab[4:5, :]
    q = jnp.dot(cq, wq_ref[...], preferred_element_type=F32)
    kv = jnp.dot(ckv, wkv_ref[...], preferred_element_type=F32)
    for h in range(MLA_HEADS):
        sl = slice(h * HEAD_PAD, (h + 1) * HEAD_PAD)
        q_ref[:, sl] = (q[:, sl] * tq).astype(BF16)
        k_ref[:, sl] = (kv[:, sl] + kp).astype(BF16)
    v = kv[:, MLA_HEADS * HEAD_PAD:]
    vlane = lax.broadcasted_iota(jnp.int32, v.shape, 1) % HEAD_PAD
    v_ref[...] = jnp.where(vlane == V_DIM, 1.0, v).astype(BF16)


def _mla_tables():
    inv_freq = ROPE_THETA ** (-np.arange(0, QK_ROPE, 2, dtype=np.float32) / QK_ROPE)
    inv_freq = inv_freq.astype(np.float32)
    half = QK_ROPE // 2
    tab = np.zeros((8, LANES), np.float32)
    scale = math.log2(math.e) / math.sqrt(QK_NOPE + QK_ROPE)
    tab[3, :QK_NOPE] = scale
    kc = [1.0, -1.0, 1.0, 1.0]
    ks = [-1.0, 1.0, 1.0, 1.0]
    for grp in range(4):
        sl = slice(QK_NOPE + grp * half, QK_NOPE + (grp + 1) * half)
        tab[0, sl] = inv_freq
        tab[1, sl] = kc[grp]
        tab[2, sl] = ks[grp]
        if grp % 2 == 0:
            tab[3, sl] = scale
        else:
            tab[4, sl] = scale
    return jnp.asarray(tab)


def _mla_weights(w_in, w_uq, w_ukv):
    half = QK_ROPE // 2
    base = Q_LORA + KV_LORA
    k1 = w_in[:, base:base + half]
    k2 = w_in[:, base + half:base + QK_ROPE]
    z64 = jnp.zeros((D_MODEL, QK_NOPE), w_in.dtype)
    win_ext = jnp.concatenate([w_in[:, :base], z64, k1, k1, k2, k2, z64, k2, k2, k1, k1], axis=1)
    wq = w_uq.reshape(Q_LORA, MLA_HEADS, QK_NOPE + QK_ROPE)
    q1 = wq[:, :, QK_NOPE:QK_NOPE + half]
    q2 = wq[:, :, QK_NOPE + half:]
    wq_ext = jnp.concatenate([wq[:, :, :QK_NOPE], q1, q2, q2, q1], axis=2)
    wq_ext = wq_ext.reshape(Q_LORA, MLA_HEADS * HEAD_PAD)
    wkv = w_ukv.reshape(KV_LORA, MLA_HEADS, QK_NOPE + V_DIM)
    zk = jnp.zeros((KV_LORA, MLA_HEADS, HEAD_PAD - QK_NOPE), w_ukv.dtype)
    zv = jnp.zeros((KV_LORA, MLA_HEADS, HEAD_PAD - V_DIM), w_ukv.dtype)
    wk_ext = jnp.concatenate([wkv[:, :, :QK_NOPE], zk], axis=2).reshape(KV_LORA, -1)
    wv_ext = jnp.concatenate([wkv[:, :, QK_NOPE:], zv], axis=2).reshape(KV_LORA, -1)
    wkv_ext = jnp.concatenate([wk_ext, wv_ext], axis=1)
    return win_ext.astype(BF16), wq_ext.astype(BF16), wkv_ext.astype(BF16)


def _mla_proj(x2d, pos, g, win_ext, qn, kvn, wq_ext, wkv_ext, tm=512):
    n = x2d.shape[0]
    hw = MLA_HEADS * HEAD_PAD
    full = lambda a: pl.BlockSpec(a.shape, lambda i: (0,) * a.ndim)
    tab = _mla_tables()
    return pl.pallas_call(
        _mla_proj_kernel,
        grid=(n // tm,),
        in_specs=[pl.BlockSpec((tm, D_MODEL), lambda i: (i, 0)),
                  pl.BlockSpec((tm, 1), lambda i: (i, 0)),
                  full(g), full(win_ext), full(qn), full(kvn), full(wq_ext), full(wkv_ext), full(tab)],
        out_specs=[pl.BlockSpec((tm, hw), lambda i: (i, 0))] * 3,
        out_shape=[jax.ShapeDtypeStruct((n, hw), BF16)] * 3,
        compiler_params=_cparams("parallel"),
        name="mla_proj",
    )(x2d, pos, g, win_ext, qn, kvn, wq_ext, wkv_ext, tab)


def _attn_kernel(q_ref, k_ref, v_ref, o_ref, s0_ref, s1_ref, p0_ref, p1_ref, *, blk):
    i = pl.program_id(2)
    heads = range(ATTN_HEADS_PER_STEP)
    lanes = [slice(h * HEAD_PAD, (h + 1) * HEAD_PAD) for h in heads]
    q = [q_ref[:, lanes[h]] for h in heads]

    def stage_a(j, s_ref):
        off = pl.multiple_of(j * blk, blk)
        for h in heads:
            s_ref[h] = lax.dot_general(q[h], k_ref[pl.ds(off, blk), lanes[h]], (((1,), (1,)), ((), ())),
                                       preferred_element_type=F32)

    def stage_b(state, s_ref, p_ref, masked=False):
        out = []
        for h in heads:
            m, _, acc = state[h]
            s = s_ref[h]
            if masked:
                row = lax.broadcasted_iota(jnp.int32, (blk, blk), 0)
                col = lax.broadcasted_iota(jnp.int32, (blk, blk), 1)
                s = jnp.where(col <= row, s, NEG_INF)
            m_new = jnp.maximum(m, jnp.max(s, axis=-1, keepdims=True))
            p_ref[h] = jnp.exp2((s - m_new).astype(BF16))
            out.append((m_new, jnp.exp2(m - m_new), acc))
        return tuple(out)

    def stage_c(j, state, p_ref):
        off = pl.multiple_of(j * blk, blk)
        out = []
        for h in heads:
            m, alpha, acc = state[h]
            pv = jnp.dot(p_ref[h], v_ref[pl.ds(off, blk), lanes[h]], preferred_element_type=F32)
            out.append((m, alpha, alpha * acc + pv))
        return tuple(out)

    def steady(j, state, even):
        s_j, s_n, p_j, p_n = (s0_ref, s1_ref, p0_ref, p1_ref) if even else (s1_ref, s0_ref, p1_ref, p0_ref)
        state = stage_c(j, state, p_j)
        stage_a(j + 2, s_j)
        return stage_b(state, s_n, p_n)

    def drain(state, even_last):
        s_i, p_i, p_prev = (s0_ref, p0_ref, p1_ref) if even_last else (s1_ref, p1_ref, p0_ref)
        state = stage_c(i - 1, state, p_prev)
        state = stage_b(state, s_i, p_i, masked=True)
        finish(stage_c(i, state, p_i))

    def finish(state):
        lane = lax.broadcasted_iota(jnp.int32, (blk, HEAD_PAD), 1)
        for t in range(ATTN_HEADS_PER_STEP // 2):
            lo, hi = (state[2 * t + e][2] for e in range(2))
            lo = lo / lo[:, V_DIM:V_DIM + 1]
            hi = pltpu.roll(hi / hi[:, V_DIM:V_DIM + 1], V_DIM, 1)
            o_ref[:, lanes[t]] = jnp.where(lane < V_DIM, lo, hi).astype(o_ref.dtype)

    init = tuple((jnp.full((blk, 1), NEG_INF, F32), jnp.ones((blk, 1), F32), jnp.zeros((blk, HEAD_PAD), F32))
                 for _ in heads)
    stage_a(0, s0_ref)

    @pl.when(i == 0)
    def _():
        finish(stage_c(0, stage_b(init, s0_ref, p0_ref, masked=True), p0_ref))

    @pl.when(i > 0)
    def _():
        stage_a(1, s1_ref)
        state = stage_b(init, s0_ref, p0_ref)

        def run(j0, count, st):
            for t in range(count):
                st = steady(j0 + t, st, even=(t % 2 == 0))
            return st

        n_steady = i - 1
        state = lax.fori_loop(0, n_steady // 4, lambda g, st: run(4 * g, 4, st), state)
        j_rest = 4 * (n_steady // 4)
        for rest in range(4):
            @pl.when(n_steady % 4 == rest)
            def _(rest=rest):
                drain(run(j_rest, rest, state), even_last=(rest % 2 == 1))


def _attention(q, k, v, batch, seq, blk=512):
    n, hw = q.shape
    nq = seq // blk
    nh = ATTN_HEADS_PER_STEP
    width = nh * HEAD_PAD
    return pl.pallas_call(
        functools.partial(_attn_kernel, blk=blk),
        grid=(batch, MLA_HEADS // nh, nq),
        in_specs=[pl.BlockSpec((blk, width), lambda b, h, i: (b * nq + i, h)),
                  pl.BlockSpec((seq, width), lambda b, h, i: (b, h), pipeline_mode=pl.Buffered(1)),
                  pl.BlockSpec((seq, width), lambda b, h, i: (b, h), pipeline_mode=pl.Buffered(1))],
        out_specs=pl.BlockSpec((blk, nh * V_DIM), lambda b, h, i: (b * nq + i, h)),
        out_shape=jax.ShapeDtypeStruct((n, MLA_HEADS * V_DIM), BF16),
        scratch_shapes=[pltpu.VMEM((nh, blk, blk), F32), pltpu.VMEM((nh, blk, blk), F32),
                        pltpu.VMEM((nh, blk, blk), BF16), pltpu.VMEM((nh, blk, blk), BF16)],
        compiler_params=_cparams("parallel", "parallel", "arbitrary"),
        name="mla_attention",
    )(q, k, v)


def _proj_ffn_kernel(o_ref, res_ref, wo_ref, g1_ref, wg_ref, wu_ref, wd_ref, g2_ref, out_ref, outn_ref):
    h1 = res_ref[...] + jnp.dot(o_ref[...], wo_ref[...], preferred_element_type=F32)
    hn = _rms(h1, g1_ref[...]).astype(BF16)
    gate = jnp.dot(hn, wg_ref[...], preferred_element_type=F32)
    up = jnp.dot(hn, wu_ref[...], preferred_element_type=F32)
    act = (jax.nn.silu(gate) * up).astype(BF16)
    out = h1 + jnp.dot(act, wd_ref[...], preferred_element_type=F32)
    out_ref[...] = out
    outn_ref[...] = _rms(out, g2_ref[...]).astype(outn_ref.dtype)


def _proj_ffn(o, res, wo, g1, wg, wu, wd, g2, tm=512):
    n, d = res.shape
    row = lambda a: pl.BlockSpec((tm, a.shape[1]), lambda i: (i, 0))
    resident = lambda a: pl.BlockSpec(a.shape, lambda i: (0, 0), pipeline_mode=pl.Buffered(1))
    return pl.pallas_call(
        _proj_ffn_kernel,
        grid=(n // tm,),
        in_specs=[row(o), row(res), resident(wo), resident(g1), resident(wg), resident(wu), resident(wd),
                  resident(g2)],
        out_specs=[pl.BlockSpec((tm, d), lambda i: (i, 0))] * 2,
        out_shape=[jax.ShapeDtypeStruct((n, d), F32), jax.ShapeDtypeStruct((n, d), BF16)],
        compiler_params=_cparams("parallel"),
        name="proj_swiglu_dense",
    )(o, res, wo, g1, wg, wu, wd, g2)


def _mm_split_kernel(a_ref, w_ref, o_ref, r_ref):
    tm = a_ref.shape[0]
    r = jnp.dot(a_ref[...], w_ref[...], preferred_element_type=F32)
    for g in range(N_SUPER):
        r_ref[g] = r[:, g * LANES:(g + 1) * LANES]
    for g in range(N_SUPER):
        for l in range(S5_CHUNK):
            piece = r_ref[g, pl.ds(l, tm // S5_CHUNK, stride=S5_CHUNK), :]
            o_ref[g, :, l * LANES:(l + 1) * LANES] = piece.astype(o_ref.dtype)


def _mm_split(a, w, tm=512):
    n, k = a.shape
    return pl.pallas_call(
        _mm_split_kernel,
        grid=(n // tm,),
        in_specs=[pl.BlockSpec((tm, k), lambda i: (i, 0)),
                  pl.BlockSpec(w.shape, lambda i: (0, 0))],
        out_specs=pl.BlockSpec((N_SUPER, tm // S5_CHUNK, S5_CHUNK * LANES), lambda i: (0, i, 0)),
        out_shape=jax.ShapeDtypeStruct((N_SUPER, n // S5_CHUNK, S5_CHUNK * LANES), BF16),
        scratch_shapes=[pltpu.VMEM((N_SUPER, tm, LANES), F32)],
        compiler_params=_cparams("parallel"),
        name="s5_in_proj",
    )(a, w)


def _s5_param_kernel(lr_ref, li_ref, ldt_ref, btr_ref, bti_ref, cr_ref, ci_ref, d_ref,
                     kt_ref, er_ref, ei_ref, wr_ref, wi_ref, ac_ref):
    L = S5_CHUNK
    lr = lr_ref[...]
    li = li_ref[...]
    dt = jnp.exp(ldt_ref[...])
    mag = jnp.exp(lr * dt)
    ar = mag * jnp.cos(li * dt)
    ai = mag * jnp.sin(li * dt)
    den = lr * lr + li * li
    gr = ((ar - 1.0) * lr + ai * li) / den
    gi = (ai * lr - (ar - 1.0) * li) / den
    btr = btr_ref[...]
    bti = bti_ref[...]
    bbr = gr * btr - gi * bti
    bbi = gr * bti + gi * btr
    kf = lax.broadcasted_iota(jnp.int32, (L, 1), 0).astype(F32)
    pmag = jnp.exp(lr * dt * kf)
    pr = pmag * jnp.cos(li * dt * kf)
    pi = pmag * jnp.sin(li * dt * kf)
    cr = cr_ref[...]
    ci = ci_ref[...]
    pr3 = pr[:, None, :]
    pi3 = pi[:, None, :]
    wr = cr[None] * pr3 - ci[None] * pi3
    wi = cr[None] * pi3 + ci[None] * pr3
    dn = (((1,), (1,)), ((), ()))
    hp = lax.Precision.HIGHEST
    kt = (lax.dot_general(wr.reshape(L * S5_GROUP, S5_STATE), bbr, dn, precision=hp,
                          preferred_element_type=F32)
          - lax.dot_general(wi.reshape(L * S5_GROUP, S5_STATE), bbi, dn, precision=hp,
                            preferred_element_type=F32))
    row = lax.broadcasted_iota(jnp.int32, kt.shape, 0)
    col = lax.broadcasted_iota(jnp.int32, kt.shape, 1)
    kt_ref[...] = kt + jnp.where(row == col, d_ref[...], 0.0)
    er_ref[...] = pr3 * bbr[None] - pi3 * bbi[None]
    ei_ref[...] = pr3 * bbi[None] + pi3 * bbr[None]
    wr_ref[...] = wr * ar - wi * ai
    wi_ref[...] = wr * ai + wi * ar
    lastr = pr[L - 1:L, :]
    lasti = pi[L - 1:L, :]
    ac_ref[0:1, :] = lastr * ar - lasti * ai
    ac_ref[1:2, :] = lastr * ai + lasti * ar


def _s5_operators(lam_re, lam_im, log_dt, b_re, b_im, c_re, c_im, d_skip):
    G, P, H, L = S5_GROUPS, S5_STATE, S5_GROUP, S5_CHUNK
    per_g = lambda *shape: pl.BlockSpec((None,) + shape, lambda g: (g,) + (0,) * len(shape))
    kt, er, ei, wr, wi, ac = pl.pallas_call(
        _s5_param_kernel,
        grid=(G,),
        in_specs=[per_g(1, P), per_g(1, P), per_g(1, 1), per_g(H, P), per_g(H, P),
                  per_g(H, P), per_g(H, P), per_g(1, H)],
        out_specs=[per_g(L * H, H), per_g(L, H, P), per_g(L, H, P), per_g(L, H, P), per_g(L, H, P),
                   per_g(2, P)],
        out_shape=[jax.ShapeDtypeStruct((G, L * H, H), F32)]
        + [jax.ShapeDtypeStruct((G, L, H, P), F32)] * 4
        + [jax.ShapeDtypeStruct((G, 2, P), F32)],
        compiler_params=_cparams("parallel"),
        name="s5_operators",
    )(lam_re.reshape(G, 1, P), lam_im.reshape(G, 1, P), log_dt.reshape(G, 1, 1),
      b_re.transpose(0, 2, 1), b_im.transpose(0, 2, 1), c_re, c_im, d_skip.reshape(G, 1, H))

    ns, sg = N_SUPER, S5_SUPER
    eye = jnp.eye(sg, dtype=F32)
    ktg = kt.reshape(ns, sg, L, H, H)
    kd = jnp.einsum('SgtOI,gk->StgIkO', ktg, eye).reshape(ns, L, LANES, LANES)
    ecat = jnp.stack([er, ei], axis=3)[:, ::-1]
    eb = ecat.reshape(ns, sg, L, H, 2, P).transpose(0, 2, 1, 3, 4, 5).reshape(ns, L, LANES, LANES)
    wcat = jnp.stack([wr, -wi], axis=3).reshape(ns, sg, L, H, 2, P)
    wc = wcat.transpose(0, 2, 4, 5, 1, 3).reshape(ns, L, LANES, LANES)
    acr = ac[:, 0, :].reshape(ns, sg, P)
    aci = ac[:, 1, :].reshape(ns, sg, P)
    a_same = jnp.concatenate([acr, acr], axis=-1)
    a_swap = jnp.concatenate([-aci, aci], axis=-1)
    return kd.astype(BF16), eb.astype(BF16), wc.astype(BF16), a_same, a_swap


def _group_of(shape, axis):
    return lax.broadcasted_iota(jnp.int32, shape, axis) // S5_GROUP


def _s5_inc_kernel(u_ref, eb_ref, o_ref, b_ref):
    tr = u_ref.shape[0]

    @pl.when(pl.program_id(1) == 0)
    def _():
        row_grp = _group_of((LANES, LANES), 0)
        for l in range(S5_CHUNK):
            blk = eb_ref[l]
            for g in range(S5_SUPER):
                b_ref[l * LANES:(l + 1) * LANES, g * LANES:(g + 1) * LANES] = jnp.where(
                    row_grp == g, blk, jnp.zeros_like(blk))

    inc = jnp.dot(u_ref[...], b_ref[...], preferred_element_type=F32)
    for g in range(S5_SUPER):
        o_ref[pl.ds(g, tr, stride=S5_SUPER), :] = inc[:, g * LANES:(g + 1) * LANES]


def _s5_scan_kernel(inc_ref, asame_ref, aswap_ref, x_ref, state_ref, *, cb):
    @pl.when(pl.program_id(0) == 0)
    def _():
        state_ref[...] = jnp.zeros_like(state_ref)

    nseq = state_ref.shape[0]
    a_same = asame_ref[...].reshape(nseq * S5_SUPER, LANES)
    a_swap = aswap_ref[...].reshape(nseq * S5_SUPER, LANES)

    def body(c, x):
        x_ref[:, c] = x.reshape(nseq, S5_SUPER, LANES)
        inc = inc_ref[:, c].reshape(nseq * S5_SUPER, LANES)
        return x * a_same + pltpu.roll(x, LANES // 2, 1) * a_swap + inc

    x0 = state_ref[...].reshape(nseq * S5_SUPER, LANES)
    state_ref[...] = lax.fori_loop(0, cb, body, x0).reshape(nseq, S5_SUPER, LANES)


def _s5_out_kernel(u_ref, x_ref, kd_ref, wc_ref, y_ref, ys_ref, t_ref, c_ref):
    tr = u_ref.shape[0]

    @pl.when(pl.program_id(1) == 0)
    def _():
        zero = jnp.zeros((LANES, LANES), t_ref.dtype)
        col_grp = _group_of((LANES, LANES), 1)
        for l_out in range(S5_CHUNK):
            cols = slice(l_out * LANES, (l_out + 1) * LANES)
            for l_in in range(S5_CHUNK):
                t_ref[l_in * LANES:(l_in + 1) * LANES, cols] = kd_ref[l_out - l_in] if l_out >= l_in else zero
            blk = wc_ref[l_out]
            for g in range(S5_SUPER):
                c_ref[g * LANES:(g + 1) * LANES, cols] = jnp.where(col_grp == g, blk, zero)

    x = jnp.concatenate([x_ref[pl.ds(g, tr, stride=S5_SUPER), :] for g in range(S5_SUPER)], axis=-1)
    u = u_ref[...]
    pair = 2 * LANES
    y = jnp.concatenate(
        [jnp.dot(u[:, :(b + 1) * pair], t_ref[:(b + 1) * pair, b * pair:(b + 1) * pair],
                 preferred_element_type=F32) for b in range(S5_CHUNK // 2)], axis=-1)
    y += jnp.dot(x.astype(BF16), c_ref[...], preferred_element_type=F32)
    y = jax.nn.gelu(y)
    for l in range(S5_CHUNK):
        ys_ref[pl.ds(l, tr, stride=S5_CHUNK), :] = y[:, l * LANES:(l + 1) * LANES]
    y_ref[...] = ys_ref[...].astype(y_ref.dtype)


def _s5_core(uc, kd, eb, wc, a_same, a_swap, batch, seq, tr=512):
    ns, rows, width = uc.shape
    L = S5_CHUNK
    n = rows * L
    n_chunks = seq // L
    sw = 2 * S5_SUPER * S5_STATE
    tr = min(tr, rows)
    blocks_spec = pl.BlockSpec((None, L, LANES, LANES), lambda s, r: (s, 0, 0, 0))
    inc = pl.pallas_call(
        _s5_inc_kernel,
        grid=(ns, rows // tr),
        in_specs=[pl.BlockSpec((None, tr, width), lambda s, r: (s, r, 0)), blocks_spec],
        out_specs=pl.BlockSpec((None, tr * S5_SUPER, LANES), lambda s, r: (s, r, 0)),
        out_shape=jax.ShapeDtypeStruct((ns, rows * S5_SUPER, LANES), F32),
        scratch_shapes=[pltpu.VMEM((width, sw), BF16)],
        compiler_params=_cparams("parallel", "arbitrary"),
        name="s5_chunk_state",
    )(uc, eb)
    nseq = ns * batch
    cb = min(32, n_chunks)
    tile = (S5_SUPER, LANES)
    bcast = lambda a: jnp.broadcast_to(a[:, None], (ns, batch) + tile).reshape((nseq,) + tile)
    seq_spec = pl.BlockSpec((nseq, cb) + tile, lambda c: (0, c, 0, 0))
    tab_spec = pl.BlockSpec((nseq,) + tile, lambda c: (0, 0, 0))
    xprev = pl.pallas_call(
        functools.partial(_s5_scan_kernel, cb=cb),
        grid=(n_chunks // cb,),
        in_specs=[seq_spec, tab_spec, tab_spec],
        out_specs=seq_spec,
        out_shape=jax.ShapeDtypeStruct((nseq, n_chunks) + tile, F32),
        scratch_shapes=[pltpu.VMEM((nseq,) + tile, F32)],
        compiler_params=_cparams("arbitrary"),
        name="s5_chunk_scan",
    )(inc.reshape((nseq, n_chunks) + tile), bcast(a_same), bcast(a_swap))
    xprev = xprev.reshape(ns, rows * S5_SUPER, LANES)
    return pl.pallas_call(
        _s5_out_kernel,
        grid=(ns, rows // tr),
        in_specs=[pl.BlockSpec((None, tr, width), lambda s, r: (s, r, 0)),
                  pl.BlockSpec((None, tr * S5_SUPER, LANES), lambda s, r: (s, r, 0)),
                  blocks_spec, blocks_spec],
        out_specs=pl.BlockSpec((None, tr * L, LANES), lambda s, r: (s, r, 0)),
        out_shape=jax.ShapeDtypeStruct((ns, n, LANES), BF16),
        scratch_shapes=[pltpu.VMEM((tr * L, LANES), F32), pltpu.VMEM((width, width), BF16),
                        pltpu.VMEM((sw, width), BF16)],
        compiler_params=_cparams("parallel", "arbitrary"),
        name="s5_chunk_out",
    )(uc, xprev, kd, wc)


def _glu_router_kernel(y_ref, w_ref, res_ref, g_ref, wr_ref, tri_ref, h_ref, hn_ref, route_ref, cnt_ref,
                       count_ref):
    d = res_ref.shape[1]
    y = jnp.concatenate([y_ref[g] for g in range(N_SUPER)], axis=-1)
    z = jnp.dot(y, w_ref[...], preferred_element_type=F32)
    h = res_ref[...] + z[:, :d] * jax.nn.sigmoid(z[:, d:])
    h_ref[...] = h
    hn = _rms(h, g_ref[...])
    hn_ref[...] = hn.astype(hn_ref.dtype)
    hi = hn.astype(BF16)
    lo = (hn - hi.astype(F32)).astype(BF16)
    wr = wr_ref[...]
    first = jnp.dot(hi, wr, preferred_element_type=F32)
    logits = (first[:, :LANES] + first[:, LANES:]) + jnp.dot(lo, wr[:, :LANES], preferred_element_type=F32)
    lane = lax.broadcasted_iota(jnp.int32, logits.shape, 1)
    logits = jnp.where(lane < N_EXPERTS, logits, -jnp.inf)
    m1 = jnp.max(logits, axis=-1, keepdims=True)
    i1 = jnp.min(jnp.where(logits == m1, lane, LANES), axis=-1, keepdims=True)
    rest = jnp.where(lane == i1, -jnp.inf, logits)
    m2 = jnp.max(rest, axis=-1, keepdims=True)
    i2 = jnp.min(jnp.where(rest == m2, lane, LANES), axis=-1, keepdims=True)
    e2 = jnp.exp(m2 - m1)
    g1 = 1.0 / (1.0 + e2)

    @pl.when(pl.program_id(0) == 0)
    def _():
        count_ref[...] = jnp.zeros_like(count_ref)

    chosen = jnp.where(jnp.logical_or(lane == i1, lane == i2), 1.0, 0.0)
    before = jnp.dot(tri_ref[...], chosen.astype(BF16), preferred_element_type=F32) + count_ref[0:1, :]
    r1 = jnp.sum(jnp.where(lane == i1, before, 0.0), axis=-1, keepdims=True)
    r2 = jnp.sum(jnp.where(lane == i2, before, 0.0), axis=-1, keepdims=True)
    count_ref[0:1, :] += jnp.sum(chosen, axis=0, keepdims=True)
    cnt_ref[...] = count_ref[...]
    route_ref[...] = (jnp.where(lane == 0, i1.astype(F32), 0.0) + jnp.where(lane == 1, i2.astype(F32), 0.0)
                      + jnp.where(lane == 2, g1, 0.0) + jnp.where(lane == 3, e2 * g1, 0.0)
                      + jnp.where(lane == 4, r1, 0.0) + jnp.where(lane == 5, r2, 0.0))


def _glu_router(y, w_glu, res, g, w_router_pad, tm=512):
    ns, n, _ = y.shape
    d = res.shape[1]
    tri = jnp.asarray(np.tril(np.ones((tm, tm), np.float32), -1), BF16)
    return pl.pallas_call(
        _glu_router_kernel,
        grid=(n // tm,),
        in_specs=[pl.BlockSpec((ns, tm, LANES), lambda i: (0, i, 0)),
                  pl.BlockSpec(w_glu.shape, lambda i: (0, 0)),
                  pl.BlockSpec((tm, d), lambda i: (i, 0)),
                  pl.BlockSpec((1, d), lambda i: (0, 0)),
                  pl.BlockSpec(w_router_pad.shape, lambda i: (0, 0)),
                  pl.BlockSpec((tm, tm), lambda i: (0, 0))],
        out_specs=[pl.BlockSpec((tm, d), lambda i: (i, 0)),
                   pl.BlockSpec((tm, d), lambda i: (i, 0)),
                   pl.BlockSpec((tm, LANES), lambda i: (i, 0)),
                   pl.BlockSpec((8, LANES), lambda i: (0, 0))],
        out_shape=[jax.ShapeDtypeStruct((n, d), F32), jax.ShapeDtypeStruct((n, d), F32),
                   jax.ShapeDtypeStruct((n, LANES), F32), jax.ShapeDtypeStruct((8, LANES), F32)],
        scratch_shapes=[pltpu.VMEM((8, LANES), F32)],
        compiler_params=_cparams("arbitrary"),
        name="s5_glu_router",
    )(y, w_glu, res, g, w_router_pad, tri)


MOE_TM = 512
MOE_TF = 1792
DISPATCH_BLOCK = 512
COMBINE_TM = 512


def _moe_plan(route, counts, n_tok):
    ne, tm = N_EXPERTS, MOE_TM
    n_tiles = (2 * n_tok) // tm + ne
    eid = route[:, :2].astype(jnp.int32).reshape(-1)
    rank = route[:, 4:6].astype(jnp.int32).reshape(-1)
    onehot = (eid[:, None] == jnp.arange(ne, dtype=jnp.int32)[None, :]).astype(jnp.int32)
    cnt = counts[0, :ne].astype(jnp.int32)
    padded = ((cnt + tm - 1) // tm) * tm
    ends = jnp.cumsum(padded)
    off = ends - padded
    pos = (jnp.sum(off[None, :] * onehot, axis=1) + rank).astype(jnp.int32)
    tile_end = ends // tm
    n_valid = tile_end[-1:].astype(jnp.int32)
    tiles = jnp.arange(n_tiles, dtype=jnp.int32)
    tile_expert = jnp.minimum(jnp.sum(tiles[:, None] >= tile_end[None, :], axis=1), ne - 1).astype(jnp.int32)
    pad_start = (off + cnt).astype(jnp.int32)
    pad_len = (padded - cnt).astype(jnp.int32)
    return pos, tile_expert, n_valid, pad_start, pad_len, n_tiles


def _dispatch_kernel(pos_ref, pstart_ref, plen_ref, hn_ref, xs_ref, zero_ref, sem, *, n_tok):
    nb = n_tok // DISPATCH_BLOCK
    b = pl.program_id(0)

    def row_copy(r, k):
        dst = pos_ref[2 * (b * DISPATCH_BLOCK + r) + k]
        return pltpu.make_async_copy(hn_ref.at[pl.ds(r, 1)], xs_ref.at[pl.ds(dst, 1)], sem)

    for r in range(DISPATCH_BLOCK):
        row_copy(r, 0).start()
        row_copy(r, 1).start()

    for _ in range(2):
        pltpu.make_async_copy(hn_ref, xs_ref.at[pl.ds(0, DISPATCH_BLOCK)], sem).wait()

    @pl.when(b == nb - 1)
    def _():
        zero_ref[...] = jnp.zeros_like(zero_ref)

        def pad_copy(e, r):
            return pltpu.make_async_copy(zero_ref.at[pl.ds(0, 1)], xs_ref.at[pl.ds(pstart_ref[e] + r, 1)], sem)

        for e in range(N_EXPERTS):
            def pbody(r, _, e=e):
                pad_copy(e, r).start()
                return 0
            lax.fori_loop(0, plen_ref[e], pbody, 0)
        for e in range(N_EXPERTS):
            def wbody(r, _, e=e):
                pad_copy(e, r).wait()
                return 0
            lax.fori_loop(0, plen_ref[e], wbody, 0)


def _dispatch(hn, pos, pad_start, pad_len, n_rows):
    n_tok, d = hn.shape
    return pl.pallas_call(
        functools.partial(_dispatch_kernel, n_tok=n_tok),
        grid_spec=pltpu.PrefetchScalarGridSpec(
            num_scalar_prefetch=3,
            grid=(n_tok // DISPATCH_BLOCK,),
            in_specs=[pl.BlockSpec((DISPATCH_BLOCK, d), lambda b, pos, ps, plen: (b, 0))],
            out_specs=pl.BlockSpec(memory_space=pl.ANY),
            scratch_shapes=[pltpu.VMEM((8, d), F32), pltpu.SemaphoreType.DMA]),
        out_shape=jax.ShapeDtypeStruct((n_rows, d), F32),
        compiler_params=_cparams("arbitrary"),
        name="moe_dispatch",
    )(pos, pad_start, pad_len, hn)


def _moe_kernel(te_ref, nv_ref, x_ref, wg_ref, wu_ref, wd_ref, y_ref, xb_ref):
    i = pl.program_id(0)
    k = pl.program_id(1)

    @pl.when(i < nv_ref[0])
    def _():
        @pl.when(k == 0)
        def _():
            xb_ref[...] = x_ref[...].astype(BF16)

        hn = xb_ref[...]
        gate = jnp.dot(hn, wg_ref[...], preferred_element_type=F32)
        up = jnp.dot(hn, wu_ref[...], preferred_element_type=F32)
        act = (jax.nn.silu(gate) * up).astype(BF16)
        part = jnp.dot(act, wd_ref[...], preferred_element_type=F32)

        @pl.when(k == 0)
        def _():
            y_ref[...] = part

        @pl.when(k > 0)
        def _():
            y_ref[...] += part


def _moe_experts(xs, tile_expert, n_valid, wg, wu, wd, n_tiles):
    tm, tf = MOE_TM, MOE_TF
    d = xs.shape[1]
    f = wg.shape[2]
    nk = f // tf

    def row_map(i, k, te, nv):
        return (jnp.minimum(i, nv[0] - 1), 0)

    def kk(i, k, nv):
        return jnp.where(i < nv[0], k, nk - 1)

    def w_in_map(i, k, te, nv):
        return (te[jnp.minimum(i, nv[0] - 1)], 0, kk(i, k, nv))

    def w_out_map(i, k, te, nv):
        return (te[jnp.minimum(i, nv[0] - 1)], kk(i, k, nv), 0)

    return pl.pallas_call(
        _moe_kernel,
        grid_spec=pltpu.PrefetchScalarGridSpec(
            num_scalar_prefetch=2,
            grid=(n_tiles, nk),
            in_specs=[pl.BlockSpec((tm, d), row_map),
                      pl.BlockSpec((None, d, tf), w_in_map),
                      pl.BlockSpec((None, d, tf), w_in_map),
                      pl.BlockSpec((None, tf, d), w_out_map)],
            out_specs=pl.BlockSpec((tm, d), row_map),
            scratch_shapes=[pltpu.VMEM((tm, d), BF16)]),
        out_shape=jax.ShapeDtypeStruct(xs.shape, F32),
        compiler_params=_cparams("arbitrary", "arbitrary"),
        name="moe_experts",
    )(tile_expert, n_valid, xs, wg, wu, wd)


def _combine_kernel(pos_ref, y_ref, h_ref, route_ref, g_ref, out_ref, buf_ref, sem, *, tm, n_steps):
    i = pl.program_id(0)
    slot = i % 2

    def row_copy(step, sl, r, k):
        t = step * tm + r
        return pltpu.make_async_copy(y_ref.at[pl.ds(pos_ref[2 * t + k], 1)],
                                     buf_ref.at[sl, k, pl.ds(r, 1)], sem.at[sl])

    def issue(step, sl):
        for r in range(tm):
            row_copy(step, sl, r, 0).start()
            row_copy(step, sl, r, 1).start()

    @pl.when(i == 0)
    def _():
        issue(0, 0)

    @pl.when(i + 1 < n_steps)
    def _():
        issue(i + 1, 1 - slot)

    for k in range(2):
        pltpu.make_async_copy(y_ref.at[pl.ds(0, tm)], buf_ref.at[slot, k], sem.at[slot]).wait()

    route = route_ref[...]
    out = h_ref[...] + route[:, 2:3] * buf_ref[slot, 0] + route[:, 3:4] * buf_ref[slot, 1]
    out_ref[...] = _rms(out, g_ref[...])


def _combine(y, pos, h, route, g):
    n_tok, d = h.shape
    tm = COMBINE_TM
    n_steps = n_tok // tm
    return pl.pallas_call(
        functools.partial(_combine_kernel, tm=tm, n_steps=n_steps),
        grid_spec=pltpu.PrefetchScalarGridSpec(
            num_scalar_prefetch=1,
            grid=(n_steps,),
            in_specs=[pl.BlockSpec(memory_space=pl.ANY),
                      pl.BlockSpec((tm, d), lambda i, pos: (i, 0)),
                      pl.BlockSpec((tm, LANES), lambda i, pos: (i, 0)),
                      pl.BlockSpec((1, d), lambda i, pos: (0, 0))],
            out_specs=pl.BlockSpec((tm, d), lambda i, pos: (i, 0)),
            scratch_shapes=[pltpu.VMEM((2, 2, tm, d), F32), pltpu.SemaphoreType.DMA((2,))]),
        out_shape=jax.ShapeDtypeStruct((n_tok, d), F32),
        compiler_params=_cparams("arbitrary"),
        name="moe_combine",
    )(pos, y, h, route, g)


def _moe_layer(h, hn, route, counts, wg, wu, wd, g):
    n_tok = h.shape[0]
    pos, tile_expert, n_valid, pad_start, pad_len, n_tiles = _moe_plan(route, counts, n_tok)
    xs = _dispatch(hn, pos, pad_start, pad_len, n_tiles * MOE_TM)
    y = _moe_experts(xs, tile_expert, n_valid, wg, wu, wd, n_tiles)
    return _combine(y, pos, h, route, g)


def _mla_layer(h, positions, mix_g, w_in, q_norm, w_uq, kv_norm, w_ukv, w_o, ffn_g, w_gate, w_up, w_down,
               next_g):
    batch, seq = positions.shape
    win_ext, wq_ext, wkv_ext = _mla_weights(w_in, w_uq, w_ukv)
    pos = positions.astype(F32).reshape(batch * seq, 1)
    q, k, v = _mla_proj(h, pos, mix_g.reshape(1, -1), win_ext, q_norm.reshape(1, -1),
                        kv_norm.reshape(1, -1), wq_ext, wkv_ext)
    o = _attention(q, k, v, batch, seq)
    return _proj_ffn(o, h, w_o.astype(BF16), ffn_g.reshape(1, -1), w_gate.astype(BF16), w_up.astype(BF16),
                     w_down.astype(BF16), next_g.reshape(1, -1))


def _s5_layer(h, hn, batch, seq, w_in, lam_re, lam_im, log_dt, b_re, b_im, c_re, c_im, d_skip, w_glu,
              ffn_g, w_router):
    u = _mm_split(hn, w_in.astype(BF16))
    kd, eb, wc, a_same, a_swap = _s5_operators(lam_re, lam_im, log_dt, b_re, b_im, c_re, c_im, d_skip)
    y = _s5_core(u, kd, eb, wc, a_same, a_swap, batch, seq)
    wr_pad = jnp.concatenate([w_router, jnp.zeros((D_MODEL, LANES - N_EXPERTS), w_router.dtype)], axis=1)
    wr_hi = wr_pad.astype(BF16)
    wr_lo = (wr_pad - wr_hi.astype(F32)).astype(BF16)
    return _glu_router(y, w_glu.astype(BF16), h, ffn_g.reshape(1, -1), jnp.concatenate([wr_hi, wr_lo], axis=1))


def kernel(x, positions, mix_norm, ffn_norm, final_norm, mla_w_in, mla_q_norm, mla_w_uq, mla_kv_norm, mla_w_ukv, mla_w_o, ffn_w_gate, ffn_w_up, ffn_w_down, s5_w_in, s5_lambda_re, s5_lambda_im, s5_log_dt, s5_b_re, s5_b_im, s5_c_re, s5_c_im, s5_d, s5_w_glu, moe_w_router, moe_w_gate, moe_w_up, moe_w_down):
    batch, seq, d = x.shape
    h0 = x.reshape(batch * seq, d)
    h2, hn2 = _mla_layer(h0, positions, mix_norm[0], mla_w_in[0], mla_q_norm[0], mla_w_uq[0],
                         mla_kv_norm[0], mla_w_ukv[0], mla_w_o[0], ffn_norm[0],
                         ffn_w_gate[0], ffn_w_up[0], ffn_w_down[0], mix_norm[1])
    h3, hn3, route, counts = _s5_layer(h2, hn2, batch, seq, s5_w_in[0], s5_lambda_re[0], s5_lambda_im[0],
                                      s5_log_dt[0], s5_b_re[0], s5_b_im[0], s5_c_re[0], s5_c_im[0], s5_d[0],
                                      s5_w_glu[0], ffn_norm[1], moe_w_router[0])
    out = _moe_layer(h3, hn3, route, counts, moe_w_gate[0].astype(BF16), moe_w_up[0].astype(BF16),
                     moe_w_down[0].astype(BF16), final_norm.reshape(1, -1))
    return out.reshape(batch, seq, d)
```

```python
import functools
import math

import jax
import jax.numpy as jnp
import numpy as np
from jax import lax
from jax.experimental import pallas as pl
from jax.experimental.pallas import tpu as pltpu

F32 = jnp.float32
BF16 = jnp.bfloat16

D_MODEL = 1024
MLA_HEADS = 16
QK_NOPE = 64
QK_ROPE = 32
V_DIM = 64
Q_LORA = 384
KV_LORA = 256
ROPE_THETA = 10000.0
HEAD_PAD = 128
ATTN_STEPS_PER_TRIP = 6
ATTN_HEADS_PER_STEP = 4
S5_GROUP = 16
S5_GROUPS = D_MODEL // S5_GROUP
S5_STATE = 64
S5_CHUNK = 16
S5_SUPER = 8
N_SUPER = S5_GROUPS // S5_SUPER
N_EXPERTS = 8
EPS = 1e-6
NEG_INF = -1e30
LANES = 128
VMEM_LIMIT = 56 * 1024 * 1024


def _cparams(*sem):
    return pltpu.CompilerParams(dimension_semantics=sem, vmem_limit_bytes=VMEM_LIMIT)


def _rms(x, g):
    return x * lax.rsqrt(jnp.mean(x * x, axis=-1, keepdims=True) + EPS) * g


def _mla_proj_kernel(x_ref, pos_ref, g_ref, win_ref, qn_ref, kvn_ref, wq_ref, wkv_ref, tab_ref,
                     q_ref, k_ref, v_ref):
    hn = _rms(x_ref[...], g_ref[...]).astype(BF16)
    proj = jnp.dot(hn, win_ref[...], preferred_element_type=F32)
    cq = _rms(proj[:, :Q_LORA], qn_ref[...]).astype(BF16)
    ckv = _rms(proj[:, Q_LORA:Q_LORA + KV_LORA], kvn_ref[...]).astype(BF16)
    ka = proj[:, 640:768]
    kb = proj[:, 768:896]
    tab = tab_ref[...]
    ang = pos_ref[...] * tab[0:1, :]
    cos = jnp.cos(ang)
    sin = jnp.sin(ang)
    kp = ka * (cos * tab[1:2, :]) + kb * (sin * tab[2:3, :])
    tq = cos * tab[3:4, :] + sin * tab[4:5, :]
    q = jnp.dot(cq, wq_ref[...], preferred_element_type=F32)
    kv = jnp.dot(ckv, wkv_ref[...], preferred_element_type=F32)
    for h in range(MLA_HEADS):
        sl = slice(h * HEAD_PAD, (h + 1) * HEAD_PAD)
        q_ref[:, sl] = (q[:, sl] * tq).astype(BF16)
        k_ref[:, sl] = (kv[:, sl] + kp).astype(BF16)
    v = kv[:, MLA_HEADS * HEAD_PAD:]
    vlane = lax.broadcasted_iota(jnp.int32, v.shape, 1) % HEAD_PAD
    v_ref[...] = jnp.where(vlane == V_DIM, 1.0, v).astype(BF16)


def _mla_tables():
    inv_freq = ROPE_THETA ** (-np.arange(0, QK_ROPE, 2, dtype=np.float32) / QK_ROPE)
    inv_freq = inv_freq.astype(np.float32)
    half = QK_ROPE // 2
    tab = np.zeros((8, LANES), np.float32)
    scale = math.log2(math.e) / math.sqrt(QK_NOPE + QK_ROPE)
    tab[3, :QK_NOPE] = scale
    kc = [1.0, -1.0, 1.0, 1.0]
    ks = [-1.0, 1.0, 1.0, 1.0]
    for grp in range(4):
        sl = slice(QK_NOPE + grp * half, QK_NOPE + (grp + 1) * half)
        tab[0, sl] = inv_freq
        tab[1, sl] = kc[grp]
        tab[2, sl] = ks[grp]
        if grp % 2 == 0:
            tab[3, sl] = scale
        else:
            tab[4, sl] = scale
    return jnp.asarray(tab)


def _mla_weights(w_in, w_uq, w_ukv):
    half = QK_ROPE // 2
    base = Q_LORA + KV_LORA
    k1 = w_in[:, base:base + half]
    k2 = w_in[:, base + half:base + QK_ROPE]
    z64 = jnp.zeros((D_MODEL, QK_NOPE), w_in.dtype)
    win_ext = jnp.concatenate([w_in[:, :base], z64, k1, k1, k2, k2, z64, k2, k2, k1, k1], axis=1)
    wq = w_uq.reshape(Q_LORA, MLA_HEADS, QK_NOPE + QK_ROPE)
    q1 = wq[:, :, QK_NOPE:QK_NOPE + half]
    q2 = wq[:, :, QK_NOPE + half:]
    wq_ext = jnp.concatenate([wq[:, :, :QK_NOPE], q1, q2, q2, q1], axis=2)
    wq_ext = wq_ext.reshape(Q_LORA, MLA_HEADS * HEAD_PAD)
    wkv = w_ukv.reshape(KV_LORA, MLA_HEADS, QK_NOPE + V_DIM)
    zk = jnp.zeros((KV_LORA, MLA_HEADS, HEAD_PAD - QK_NOPE), w_ukv.dtype)
    zv = jnp.zeros((KV_LORA, MLA_HEADS, HEAD_PAD - V_DIM), w_ukv.dtype)
    wk_ext = jnp.concatenate([wkv[:, :, :QK_NOPE], zk], axis=2).reshape(KV_LORA, -1)
    wv_ext = jnp.concatenate([wkv[:, :, QK_NOPE:], zv], axis=2).reshape(KV_LORA, -1)
    wkv_ext = jnp.concatenate([wk_ext, wv_ext], axis=1)
    return win_ext.astype(BF16), wq_ext.astype(BF16), wkv_ext.astype(BF16)


def _mla_proj(x2d, pos, g, win_ext, qn, kvn, wq_ext, wkv_ext, tm=512):
    n = x2d.shape[0]
    hw = MLA_HEADS * HEAD_PAD
    full = lambda a: pl.BlockSpec(a.shape, lambda i: (0,) * a.ndim)
    tab = _mla_tables()
    return pl.pallas_call(
        _mla_proj_kernel,
        grid=(n // tm,),
        in_specs=[pl.BlockSpec((tm, D_MODEL), lambda i: (i, 0)),
                  pl.BlockSpec((tm, 1), lambda i: (i, 0)),
                  full(g), full(win_ext), full(qn), full(kvn), full(wq_ext), full(wkv_ext), full(tab)],
        out_specs=[pl.BlockSpec((tm, hw), lambda i: (i, 0))] * 3,
        out_shape=[jax.ShapeDtypeStruct((n, hw), BF16)] * 3,
        compiler_params=_cparams("parallel"),
        name="mla_proj",
    )(x2d, pos, g, win_ext, qn, kvn, wq_ext, wkv_ext, tab)


def _attn_kernel(q_ref, k_ref, v_ref, o_ref, s0_ref, s1_ref, p0_ref, p1_ref, *, blk):
    i = pl.program_id(2)
    heads = range(ATTN_HEADS_PER_STEP)
    lanes = [slice(h * HEAD_PAD, (h + 1) * HEAD_PAD) for h in heads]
    q = [q_ref[:, lanes[h]] for h in heads]

    def stage_a(j, s_ref):
        off = pl.multiple_of(j * blk, blk)
        for h in heads:
            s_ref[h] = lax.dot_general(q[h], k_ref[pl.ds(off, blk), lanes[h]], (((1,), (1,)), ((), ())),
                                       preferred_element_type=F32)

    def stage_b(state, s_ref, p_ref, masked=False):
        out = []
        for h in heads:
            m, _, acc = state[h]
            s = s_ref[h]
            if masked:
                row = lax.broadcasted_iota(jnp.int32, (blk, blk), 0)
                col = lax.broadcasted_iota(jnp.int32, (blk, blk), 1)
                s = jnp.where(col <= row, s, NEG_INF)
            m_new = jnp.maximum(m, jnp.max(s, axis=-1, keepdims=True))
            p_ref[h] = jnp.exp2((s - m_new).astype(BF16))
            out.append((m_new, jnp.exp2(m - m_new), acc))
        return tuple(out)

    def stage_c(j, state, p_ref):
        off = pl.multiple_of(j * blk, blk)
        out = []
        for h in heads:
            m, alpha, acc = state[h]
            pv = jnp.dot(p_ref[h], v_ref[pl.ds(off, blk), lanes[h]], preferred_element_type=F32)
            out.append((m, alpha, alpha * acc + pv))
        return tuple(out)

    def steady(j, state, even):
        s_j, s_n, p_j, p_n = (s0_ref, s1_ref, p0_ref, p1_ref) if even else (s1_ref, s0_ref, p1_ref, p0_ref)
        state = stage_c(j, state, p_j)
        stage_a(j + 2, s_j)
        return stage_b(state, s_n, p_n)

    def drain(state, even_last):
        s_i, p_i, p_prev = (s0_ref, p0_ref, p1_ref) if even_last else (s1_ref, p1_ref, p0_ref)
        state = stage_c(i - 1, state, p_prev)
        state = stage_b(state, s_i, p_i, masked=True)
        finish(stage_c(i, state, p_i))

    def finish(state):
        lane = lax.broadcasted_iota(jnp.int32, (blk, HEAD_PAD), 1)
        for t in range(ATTN_HEADS_PER_STEP // 2):
            lo, hi = (state[2 * t + e][2] for e in range(2))
            lo = lo / lo[:, V_DIM:V_DIM + 1]
            hi = pltpu.roll(hi / hi[:, V_DIM:V_DIM + 1], V_DIM, 1)
            o_ref[:, lanes[t]] = jnp.where(lane < V_DIM, lo, hi).astype(o_ref.dtype)

    init = tuple((jnp.full((blk, 1), NEG_INF, F32), jnp.ones((blk, 1), F32), jnp.zeros((blk, HEAD_PAD), F32))
                 for _ in heads)
    stage_a(0, s0_ref)

    @pl.when(i == 0)
    def _():
        finish(stage_c(0, stage_b(init, s0_ref, p0_ref, masked=True), p0_ref))

    @pl.when(i > 0)
    def _():
        stage_a(1, s1_ref)
        state = stage_b(init, s0_ref, p0_ref)

        def run(j0, count, st):
            for t in range(count):
                st = steady(j0 + t, st, even=(t % 2 == 0))
            return st

        grp = ATTN_STEPS_PER_TRIP
        n_steady = i - 1
        state = lax.fori_loop(0, n_steady // grp, lambda g, st: run(grp * g, grp, st), state)
        j_rest = grp * (n_steady // grp)
        for rest in range(grp):
            @pl.when(n_steady % grp == rest)
            def _(rest=rest):
                drain(run(j_rest, rest, state), even_last=(rest % 2 == 1))


def _attention(q, k, v, batch, seq, blk=512):
    n, hw = q.shape
    nq = seq // blk
    nh = ATTN_HEADS_PER_STEP
    width = nh * HEAD_PAD
    return pl.pallas_call(
        functools.partial(_attn_kernel, blk=blk),
        grid=(batch, MLA_HEADS // nh, nq),
        in_specs=[pl.BlockSpec((blk, width), lambda b, h, i: (b * nq + i, h)),
                  pl.BlockSpec((seq, width), lambda b, h, i: (b, h), pipeline_mode=pl.Buffered(1)),
                  pl.BlockSpec((seq, width), lambda b, h, i: (b, h), pipeline_mode=pl.Buffered(1))],
        out_specs=pl.BlockSpec((blk, nh * V_DIM), lambda b, h, i: (b * nq + i, h)),
        out_shape=jax.ShapeDtypeStruct((n, MLA_HEADS * V_DIM), BF16),
        scratch_shapes=[pltpu.VMEM((nh, blk, blk), F32), pltpu.VMEM((nh, blk, blk), F32),
                        pltpu.VMEM((nh, blk, blk), BF16), pltpu.VMEM((nh, blk, blk), BF16)],
        compiler_params=_cparams("parallel", "parallel", "arbitrary"),
        name="mla_attention",
    )(q, k, v)


def _proj_ffn_kernel(o_ref, res_ref, wo_ref, g1_ref, wg_ref, wu_ref, wd_ref, g2_ref, out_ref, outn_ref):
    h1 = res_ref[...] + jnp.dot(o_ref[...], wo_ref[...], preferred_element_type=F32)
    hn = _rms(h1, g1_ref[...]).astype(BF16)
    gate = jnp.dot(hn, wg_ref[...], preferred_element_type=F32)
    up = jnp.dot(hn, wu_ref[...], preferred_element_type=F32)
    act = (jax.nn.silu(gate) * up).astype(BF16)
    out = h1 + jnp.dot(act, wd_ref[...], preferred_element_type=F32)
    out_ref[...] = out
    outn_ref[...] = _rms(out, g2_ref[...]).astype(outn_ref.dtype)


def _proj_ffn(o, res, wo, g1, wg, wu, wd, g2, tm=512):
    n, d = res.shape
    row = lambda a: pl.BlockSpec((tm, a.shape[1]), lambda i: (i, 0))
    resident = lambda a: pl.BlockSpec(a.shape, lambda i: (0, 0), pipeline_mode=pl.Buffered(1))
    return pl.pallas_call(
        _proj_ffn_kernel,
        grid=(n // tm,),
        in_specs=[row(o), row(res), resident(wo), resident(g1), resident(wg), resident(wu), resident(wd),
                  resident(g2)],
        out_specs=[pl.BlockSpec((tm, d), lambda i: (i, 0))] * 2,
        out_shape=[jax.ShapeDtypeStruct((n, d), F32), jax.ShapeDtypeStruct((n, d), BF16)],
        compiler_params=_cparams("parallel"),
        name="proj_swiglu_dense",
    )(o, res, wo, g1, wg, wu, wd, g2)


def _mm_split_kernel(a_ref, w_ref, o_ref, r_ref):
    tm = a_ref.shape[0]
    r = jnp.dot(a_ref[...], w_ref[...], preferred_element_type=F32)
    for g in range(N_SUPER):
        r_ref[g] = r[:, g * LANES:(g + 1) * LANES]
    for g in range(N_SUPER):
        for l in range(S5_CHUNK):
            piece = r_ref[g, pl.ds(l, tm // S5_CHUNK, stride=S5_CHUNK), :]
            o_ref[g, :, l * LANES:(l + 1) * LANES] = piece.astype(o_ref.dtype)


def _mm_split(a, w, tm=512):
    n, k = a.shape
    return pl.pallas_call(
        _mm_split_kernel,
        grid=(n // tm,),
        in_specs=[pl.BlockSpec((tm, k), lambda i: (i, 0)),
                  pl.BlockSpec(w.shape, lambda i: (0, 0))],
        out_specs=pl.BlockSpec((N_SUPER, tm // S5_CHUNK, S5_CHUNK * LANES), lambda i: (0, i, 0)),
        out_shape=jax.ShapeDtypeStruct((N_SUPER, n // S5_CHUNK, S5_CHUNK * LANES), BF16),
        scratch_shapes=[pltpu.VMEM((N_SUPER, tm, LANES), F32)],
        compiler_params=_cparams("parallel"),
        name="s5_in_proj",
    )(a, w)


def _s5_param_kernel(lr_ref, li_ref, ldt_ref, btr_ref, bti_ref, cr_ref, ci_ref, d_ref,
                     kt_ref, er_ref, ei_ref, wr_ref, wi_ref, ac_ref):
    L = S5_CHUNK
    lr = lr_ref[...]
    li = li_ref[...]
    dt = jnp.exp(ldt_ref[...])
    mag = jnp.exp(lr * dt)
    ar = mag * jnp.cos(li * dt)
    ai = mag * jnp.sin(li * dt)
    den = lr * lr + li * li
    gr = ((ar - 1.0) * lr + ai * li) / den
    gi = (ai * lr - (ar - 1.0) * li) / den
    btr = btr_ref[...]
    bti = bti_ref[...]
    bbr = gr * btr - gi * bti
    bbi = gr * bti + gi * btr
    kf = lax.broadcasted_iota(jnp.int32, (L, 1), 0).astype(F32)
    pmag = jnp.exp(lr * dt * kf)
    pr = pmag * jnp.cos(li * dt * kf)
    pi = pmag * jnp.sin(li * dt * kf)
    cr = cr_ref[...]
    ci = ci_ref[...]
    pr3 = pr[:, None, :]
    pi3 = pi[:, None, :]
    wr = cr[None] * pr3 - ci[None] * pi3
    wi = cr[None] * pi3 + ci[None] * pr3
    dn = (((1,), (1,)), ((), ()))
    hp = lax.Precision.HIGHEST
    kt = (lax.dot_general(wr.reshape(L * S5_GROUP, S5_STATE), bbr, dn, precision=hp,
                          preferred_element_type=F32)
          - lax.dot_general(wi.reshape(L * S5_GROUP, S5_STATE), bbi, dn, precision=hp,
                            preferred_element_type=F32))
    row = lax.broadcasted_iota(jnp.int32, kt.shape, 0)
    col = lax.broadcasted_iota(jnp.int32, kt.shape, 1)
    kt_ref[...] = kt + jnp.where(row == col, d_ref[...], 0.0)
    er_ref[...] = pr3 * bbr[None] - pi3 * bbi[None]
    ei_ref[...] = pr3 * bbi[None] + pi3 * bbr[None]
    wr_ref[...] = wr * ar - wi * ai
    wi_ref[...] = wr * ai + wi * ar
    lastr = pr[L - 1:L, :]
    lasti = pi[L - 1:L, :]
    ac_ref[0:1, :] = lastr * ar - lasti * ai
    ac_ref[1:2, :] = lastr * ai + lasti * ar


def _s5_operators(lam_re, lam_im, log_dt, b_re, b_im, c_re, c_im, d_skip):
    G, P, H, L = S5_GROUPS, S5_STATE, S5_GROUP, S5_CHUNK
    per_g = lambda *shape: pl.BlockSpec((None,) + shape, lambda g: (g,) + (0,) * len(shape))
    kt, er, ei, wr, wi, ac = pl.pallas_call(
        _s5_param_kernel,
        grid=(G,),
        in_specs=[per_g(1, P), per_g(1, P), per_g(1, 1), per_g(H, P), per_g(H, P),
                  per_g(H, P), per_g(H, P), per_g(1, H)],
        out_specs=[per_g(L * H, H), per_g(L, H, P), per_g(L, H, P), per_g(L, H, P), per_g(L, H, P),
                   per_g(2, P)],
        out_shape=[jax.ShapeDtypeStruct((G, L * H, H), F32)]
        + [jax.ShapeDtypeStruct((G, L, H, P), F32)] * 4
        + [jax.ShapeDtypeStruct((G, 2, P), F32)],
        compiler_params=_cparams("parallel"),
        name="s5_operators",
    )(lam_re.reshape(G, 1, P), lam_im.reshape(G, 1, P), log_dt.reshape(G, 1, 1),
      b_re.transpose(0, 2, 1), b_im.transpose(0, 2, 1), c_re, c_im, d_skip.reshape(G, 1, H))

    ns, sg = N_SUPER, S5_SUPER
    eye = jnp.eye(sg, dtype=F32)
    ktg = kt.reshape(ns, sg, L, H, H)
    kd = jnp.einsum('SgtOI,gk->StgIkO', ktg, eye).reshape(ns, L, LANES, LANES)
    ecat = jnp.stack([er, ei], axis=3)[:, ::-1]
    eb = ecat.reshape(ns, sg, L, H, 2, P).transpose(0, 2, 1, 3, 4, 5).reshape(ns, L, LANES, LANES)
    wcat = jnp.stack([wr, -wi], axis=3).reshape(ns, sg, L, H, 2, P)
    wc = wcat.transpose(0, 2, 4, 5, 1, 3).reshape(ns, L, LANES, LANES)
    acr = ac[:, 0, :].reshape(ns, sg, P)
    aci = ac[:, 1, :].reshape(ns, sg, P)
    a_same = jnp.concatenate([acr, acr], axis=-1)
    a_swap = jnp.concatenate([-aci, aci], axis=-1)
    return kd.astype(BF16), eb.astype(BF16), wc.astype(BF16), a_same, a_swap


def _group_of(shape, axis):
    return lax.broadcasted_iota(jnp.int32, shape, axis) // S5_GROUP


def _s5_inc_kernel(u_ref, eb_ref, o_ref, b_ref):
    tr = u_ref.shape[0]

    @pl.when(pl.program_id(1) == 0)
    def _():
        row_grp = _group_of((LANES, LANES), 0)
        for l in range(S5_CHUNK):
            blk = eb_ref[l]
            for g in range(S5_SUPER):
                b_ref[l * LANES:(l + 1) * LANES, g * LANES:(g + 1) * LANES] = jnp.where(
                    row_grp == g, blk, jnp.zeros_like(blk))

    inc = jnp.dot(u_ref[...], b_ref[...], preferred_element_type=F32)
    for g in range(S5_SUPER):
        o_ref[pl.ds(g, tr, stride=S5_SUPER), :] = inc[:, g * LANES:(g + 1) * LANES]


def _s5_scan_kernel(inc_ref, asame_ref, aswap_ref, x_ref, state_ref, *, cb):
    @pl.when(pl.program_id(0) == 0)
    def _():
        state_ref[...] = jnp.zeros_like(state_ref)

    nseq = state_ref.shape[0]
    a_same = asame_ref[...].reshape(nseq * S5_SUPER, LANES)
    a_swap = aswap_ref[...].reshape(nseq * S5_SUPER, LANES)

    def body(c, x):
        x_ref[:, c] = x.reshape(nseq, S5_SUPER, LANES)
        inc = inc_ref[:, c].reshape(nseq * S5_SUPER, LANES)
        return x * a_same + pltpu.roll(x, LANES // 2, 1) * a_swap + inc

    x0 = state_ref[...].reshape(nseq * S5_SUPER, LANES)
    state_ref[...] = lax.fori_loop(0, cb, body, x0).reshape(nseq, S5_SUPER, LANES)


def _s5_out_kernel(u_ref, x_ref, kd_ref, wc_ref, y_ref, ys_ref, t_ref, c_ref):
    tr = u_ref.shape[0]

    @pl.when(pl.program_id(1) == 0)
    def _():
        zero = jnp.zeros((LANES, LANES), t_ref.dtype)
        col_grp = _group_of((LANES, LANES), 1)
        for l_out in range(S5_CHUNK):
            cols = slice(l_out * LANES, (l_out + 1) * LANES)
            for l_in in range(S5_CHUNK):
                t_ref[l_in * LANES:(l_in + 1) * LANES, cols] = kd_ref[l_out - l_in] if l_out >= l_in else zero
            blk = wc_ref[l_out]
            for g in range(S5_SUPER):
                c_ref[g * LANES:(g + 1) * LANES, cols] = jnp.where(col_grp == g, blk, zero)

    x = jnp.concatenate([x_ref[pl.ds(g, tr, stride=S5_SUPER), :] for g in range(S5_SUPER)], axis=-1)
    u = u_ref[...]
    pair = 2 * LANES
    y = jnp.concatenate(
        [jnp.dot(u[:, :(b + 1) * pair], t_ref[:(b + 1) * pair, b * pair:(b + 1) * pair],
                 preferred_element_type=F32) for b in range(S5_CHUNK // 2)], axis=-1)
    y += jnp.dot(x.astype(BF16), c_ref[...], preferred_element_type=F32)
    y = jax.nn.gelu(y)
    for l in range(S5_CHUNK):
        ys_ref[pl.ds(l, tr, stride=S5_CHUNK), :] = y[:, l * LANES:(l + 1) * LANES]
    y_ref[...] = ys_ref[...].astype(y_ref.dtype)


def _s5_core(uc, kd, eb, wc, a_same, a_swap, batch, seq, tr=512):
    ns, rows, width = uc.shape
    L = S5_CHUNK
    n = rows * L
    n_chunks = seq // L
    sw = 2 * S5_SUPER * S5_STATE
    tr = min(tr, rows)
    blocks_spec = pl.BlockSpec((None, L, LANES, LANES), lambda s, r: (s, 0, 0, 0))
    inc = pl.pallas_call(
        _s5_inc_kernel,
        grid=(ns, rows // tr),
        in_specs=[pl.BlockSpec((None, tr, width), lambda s, r: (s, r, 0)), blocks_spec],
        out_specs=pl.BlockSpec((None, tr * S5_SUPER, LANES), lambda s, r: (s, r, 0)),
        out_shape=jax.ShapeDtypeStruct((ns, rows * S5_SUPER, LANES), F32),
        scratch_shapes=[pltpu.VMEM((width, sw), BF16)],
        compiler_params=_cparams("parallel", "arbitrary"),
        name="s5_chunk_state",
    )(uc, eb)
    nseq = ns * batch
    cb = min(32, n_chunks)
    tile = (S5_SUPER, LANES)
    bcast = lambda a: jnp.broadcast_to(a[:, None], (ns, batch) + tile).reshape((nseq,) + tile)
    seq_spec = pl.BlockSpec((nseq, cb) + tile, lambda c: (0, c, 0, 0))
    tab_spec = pl.BlockSpec((nseq,) + tile, lambda c: (0, 0, 0))
    xprev = pl.pallas_call(
        functools.partial(_s5_scan_kernel, cb=cb),
        grid=(n_chunks // cb,),
        in_specs=[seq_spec, tab_spec, tab_spec],
        out_specs=seq_spec,
        out_shape=jax.ShapeDtypeStruct((nseq, n_chunks) + tile, F32),
        scratch_shapes=[pltpu.VMEM((nseq,) + tile, F32)],
        compiler_params=_cparams("arbitrary"),
        name="s5_chunk_scan",
    )(inc.reshape((nseq, n_chunks) + tile), bcast(a_same), bcast(a_swap))
    xprev = xprev.reshape(ns, rows * S5_SUPER, LANES)
    return pl.pallas_call(
        _s5_out_kernel,
        grid=(ns, rows // tr),
        in_specs=[pl.BlockSpec((None, tr, width), lambda s, r: (s, r, 0)),
                  pl.BlockSpec((None, tr * S5_SUPER, LANES), lambda s, r: (s, r, 0)),
                  blocks_spec, blocks_spec],
        out_specs=pl.BlockSpec((None, tr * L, LANES), lambda s, r: (s, r, 0)),
        out_shape=jax.ShapeDtypeStruct((ns, n, LANES), BF16),
        scratch_shapes=[pltpu.VMEM((tr * L, LANES), F32), pltpu.VMEM((width, width), BF16),
                        pltpu.VMEM((sw, width), BF16)],
        compiler_params=_cparams("parallel", "arbitrary"),
        name="s5_chunk_out",
    )(uc, xprev, kd, wc)


def _glu_router_kernel(y_ref, w_ref, res_ref, g_ref, wr_ref, tri_ref, h_ref, hn_ref, route_ref, cnt_ref,
                       count_ref):
    d = res_ref.shape[1]
    y = jnp.concatenate([y_ref[g] for g in range(N_SUPER)], axis=-1)
    z = jnp.dot(y, w_ref[...], preferred_element_type=F32)
    h = res_ref[...] + z[:, :d] * jax.nn.sigmoid(z[:, d:])
    h_ref[...] = h
    hn = _rms(h, g_ref[...])
    hn_ref[...] = hn.astype(hn_ref.dtype)
    hi = hn.astype(BF16)
    lo = (hn - hi.astype(F32)).astype(BF16)
    wr = wr_ref[...]
    first = jnp.dot(hi, wr, preferred_element_type=F32)
    logits = (first[:, :LANES] + first[:, LANES:]) + jnp.dot(lo, wr[:, :LANES], preferred_element_type=F32)
    lane = lax.broadcasted_iota(jnp.int32, logits.shape, 1)
    logits = jnp.where(lane < N_EXPERTS, logits, -jnp.inf)
    m1 = jnp.max(logits, axis=-1, keepdims=True)
    i1 = jnp.min(jnp.where(logits == m1, lane, LANES), axis=-1, keepdims=True)
    rest = jnp.where(lane == i1, -jnp.inf, logits)
    m2 = jnp.max(rest, axis=-1, keepdims=True)
    i2 = jnp.min(jnp.where(rest == m2, lane, LANES), axis=-1, keepdims=True)
    e2 = jnp.exp(m2 - m1)
    g1 = 1.0 / (1.0 + e2)

    @pl.when(pl.program_id(0) == 0)
    def _():
        count_ref[...] = jnp.zeros_like(count_ref)

    chosen = jnp.where(jnp.logical_or(lane == i1, lane == i2), 1.0, 0.0)
    before = jnp.dot(tri_ref[...], chosen.astype(BF16), preferred_element_type=F32) + count_ref[0:1, :]
    r1 = jnp.sum(jnp.where(lane == i1, before, 0.0), axis=-1, keepdims=True)
    r2 = jnp.sum(jnp.where(lane == i2, before, 0.0), axis=-1, keepdims=True)
    count_ref[0:1, :] += jnp.sum(chosen, axis=0, keepdims=True)
    cnt_ref[...] = count_ref[...]
    route_ref[...] = (jnp.where(lane == 0, i1.astype(F32), 0.0) + jnp.where(lane == 1, i2.astype(F32), 0.0)
                      + jnp.where(lane == 2, g1, 0.0) + jnp.where(lane == 3, e2 * g1, 0.0)
                      + jnp.where(lane == 4, r1, 0.0) + jnp.where(lane == 5, r2, 0.0))


def _glu_router(y, w_glu, res, g, w_router_pad, tm=512):
    ns, n, _ = y.shape
    d = res.shape[1]
    tri = jnp.asarray(np.tril(np.ones((tm, tm), np.float32), -1), BF16)
    return pl.pallas_call(
        _glu_router_kernel,
        grid=(n // tm,),
        in_specs=[pl.BlockSpec((ns, tm, LANES), lambda i: (0, i, 0)),
                  pl.BlockSpec(w_glu.shape, lambda i: (0, 0)),
                  pl.BlockSpec((tm, d), lambda i: (i, 0)),
                  pl.BlockSpec((1, d), lambda i: (0, 0)),
                  pl.BlockSpec(w_router_pad.shape, lambda i: (0, 0)),
                  pl.BlockSpec((tm, tm), lambda i: (0, 0))],
        out_specs=[pl.BlockSpec((tm, d), lambda i: (i, 0)),
                   pl.BlockSpec((tm, d), lambda i: (i, 0)),
                   pl.BlockSpec((tm, LANES), lambda i: (i, 0)),
                   pl.BlockSpec((8, LANES), lambda i: (0, 0))],
        out_shape=[jax.ShapeDtypeStruct((n, d), F32), jax.ShapeDtypeStruct((n, d), F32),
                   jax.ShapeDtypeStruct((n, LANES), F32), jax.ShapeDtypeStruct((8, LANES), F32)],
        scratch_shapes=[pltpu.VMEM((8, LANES), F32)],
        compiler_params=_cparams("arbitrary"),
        name="s5_glu_router",
    )(y, w_glu, res, g, w_router_pad, tri)


MOE_TM = 512
MOE_TF = 1792
DISPATCH_BLOCK = 512
COMBINE_TM = 512


def _moe_plan(route, counts, n_tok):
    ne, tm = N_EXPERTS, MOE_TM
    n_tiles = (2 * n_tok) // tm + ne
    eid = route[:, :2].astype(jnp.int32).reshape(-1)
    rank = route[:, 4:6].astype(jnp.int32).reshape(-1)
    onehot = (eid[:, None] == jnp.arange(ne, dtype=jnp.int32)[None, :]).astype(jnp.int32)
    cnt = counts[0, :ne].astype(jnp.int32)
    padded = ((cnt + tm - 1) // tm) * tm
    ends = jnp.cumsum(padded)
    off = ends - padded
    pos = (jnp.sum(off[None, :] * onehot, axis=1) + rank).astype(jnp.int32)
    tile_end = ends // tm
    n_valid = tile_end[-1:].astype(jnp.int32)
    tiles = jnp.arange(n_tiles, dtype=jnp.int32)
    tile_expert = jnp.minimum(jnp.sum(tiles[:, None] >= tile_end[None, :], axis=1), ne - 1).astype(jnp.int32)
    pad_start = (off + cnt).astype(jnp.int32)
    pad_len = (padded - cnt).astype(jnp.int32)
    return pos, tile_expert, n_valid, pad_start, pad_len, n_tiles


def _dispatch_kernel(pos_ref, pstart_ref, plen_ref, hn_ref, xs_ref, zero_ref, sem, *, n_tok):
    nb = n_tok // DISPATCH_BLOCK
    b = pl.program_id(0)

    def row_copy(r, k):
        dst = pos_ref[2 * (b * DISPATCH_BLOCK + r) + k]
        return pltpu.make_async_copy(hn_ref.at[pl.ds(r, 1)], xs_ref.at[pl.ds(dst, 1)], sem)

    for r in range(DISPATCH_BLOCK):
        row_copy(r, 0).start()
        row_copy(r, 1).start()

    for _ in range(2):
        pltpu.make_async_copy(hn_ref, xs_ref.at[pl.ds(0, DISPATCH_BLOCK)], sem).wait()

    @pl.when(b == nb - 1)
    def _():
        zero_ref[...] = jnp.zeros_like(zero_ref)

        def pad_copy(e, r):
            return pltpu.make_async_copy(zero_ref.at[pl.ds(0, 1)], xs_ref.at[pl.ds(pstart_ref[e] + r, 1)], sem)

        for e in range(N_EXPERTS):
            def pbody(r, _, e=e):
                pad_copy(e, r).start()
                return 0
            lax.fori_loop(0, plen_ref[e], pbody, 0)
        for e in range(N_EXPERTS):
            def wbody(r, _, e=e):
                pad_copy(e, r).wait()
                return 0
            lax.fori_loop(0, plen_ref[e], wbody, 0)


def _dispatch(hn, pos, pad_start, pad_len, n_rows):
    n_tok, d = hn.shape
    return pl.pallas_call(
        functools.partial(_dispatch_kernel, n_tok=n_tok),
        grid_spec=pltpu.PrefetchScalarGridSpec(
            num_scalar_prefetch=3,
            grid=(n_tok // DISPATCH_BLOCK,),
            in_specs=[pl.BlockSpec((DISPATCH_BLOCK, d), lambda b, pos, ps, plen: (b, 0))],
            out_specs=pl.BlockSpec(memory_space=pl.ANY),
            scratch_shapes=[pltpu.VMEM((8, d), F32), pltpu.SemaphoreType.DMA]),
        out_shape=jax.ShapeDtypeStruct((n_rows, d), F32),
        compiler_params=_cparams("arbitrary"),
        name="moe_dispatch",
    )(pos, pad_start, pad_len, hn)


def _moe_kernel(te_ref, nv_ref, x_ref, wg_ref, wu_ref, wd_ref, y_ref, xb_ref):
    i = pl.program_id(0)
    k = pl.program_id(1)

    @pl.when(i < nv_ref[0])
    def _():
        @pl.when(k == 0)
        def _():
            xb_ref[...] = x_ref[...].astype(BF16)

        hn = xb_ref[...]
        gate = jnp.dot(hn, wg_ref[...], preferred_element_type=F32)
        up = jnp.dot(hn, wu_ref[...], preferred_element_type=F32)
        act = (jax.nn.silu(gate) * up).astype(BF16)
        part = jnp.dot(act, wd_ref[...], preferred_element_type=F32)

        @pl.when(k == 0)
        def _():
            y_ref[...] = part

        @pl.when(k > 0)
        def _():
            y_ref[...] += part


def _moe_experts(xs, tile_expert, n_valid, wg, wu, wd, n_tiles):
    tm, tf = MOE_TM, MOE_TF
    d = xs.shape[1]
    f = wg.shape[2]
    nk = f // tf

    def row_map(i, k, te, nv):
        return (jnp.minimum(i, nv[0] - 1), 0)

    def kk(i, k, nv):
        return jnp.where(i < nv[0], k, nk - 1)

    def w_in_map(i, k, te, nv):
        return (te[jnp.minimum(i, nv[0] - 1)], 0, kk(i, k, nv))

    def w_out_map(i, k, te, nv):
        return (te[jnp.minimum(i, nv[0] - 1)], kk(i, k, nv), 0)

    return pl.pallas_call(
        _moe_kernel,
        grid_spec=pltpu.PrefetchScalarGridSpec(
            num_scalar_prefetch=2,
            grid=(n_tiles, nk),
            in_specs=[pl.BlockSpec((tm, d), row_map),
                      pl.BlockSpec((None, d, tf), w_in_map),
                      pl.BlockSpec((None, d, tf), w_in_map),
                      pl.BlockSpec((None, tf, d), w_out_map)],
            out_specs=pl.BlockSpec((tm, d), row_map),
            scratch_shapes=[pltpu.VMEM((tm, d), BF16)]),
        out_shape=jax.ShapeDtypeStruct(xs.shape, F32),
        compiler_params=_cparams("arbitrary", "arbitrary"),
        name="moe_experts",
    )(tile_expert, n_valid, xs, wg, wu, wd)


def _combine_kernel(pos_ref, y_ref, h_ref, route_ref, g_ref, out_ref, buf_ref, sem, *, tm, n_steps):
    i = pl.program_id(0)
    slot = i % 2

    def row_copy(step, sl, r, k):
        t = step * tm + r
        return pltpu.make_async_copy(y_ref.at[pl.ds(pos_ref[2 * t + k], 1)],
                                     buf_ref.at[sl, k, pl.ds(r, 1)], sem.at[sl])

    def issue(step, sl):
        for r in range(tm):
            row_copy(step, sl, r, 0).start()
            row_copy(step, sl, r, 1).start()

    @pl.when(i == 0)
    def _():
        issue(0, 0)

    @pl.when(i + 1 < n_steps)
    def _():
        issue(i + 1, 1 - slot)

    for k in range(2):
        pltpu.make_async_copy(y_ref.at[pl.ds(0, tm)], buf_ref.at[slot, k], sem.at[slot]).wait()

    route = route_ref[...]
    out = h_ref[...] + route[:, 2:3] * buf_ref[slot, 0] + route[:, 3:4] * buf_ref[slot, 1]
    out_ref[...] = _rms(out, g_ref[...])


def _combine(y, pos, h, route, g):
    n_tok, d = h.shape
    tm = COMBINE_TM
    n_steps = n_tok // tm
    return pl.pallas_call(
        functools.partial(_combine_kernel, tm=tm, n_steps=n_steps),
        grid_spec=pltpu.PrefetchScalarGridSpec(
            num_scalar_prefetch=1,
            grid=(n_steps,),
            in_specs=[pl.BlockSpec(memory_space=pl.ANY),
                      pl.BlockSpec((tm, d), lambda i, pos: (i, 0)),
                      pl.BlockSpec((tm, LANES), lambda i, pos: (i, 0)),
                      pl.BlockSpec((1, d), lambda i, pos: (0, 0))],
            out_specs=pl.BlockSpec((tm, d), lambda i, pos: (i, 0)),
            scratch_shapes=[pltpu.VMEM((2, 2, tm, d), F32), pltpu.SemaphoreType.DMA((2,))]),
        out_shape=jax.ShapeDtypeStruct((n_tok, d), F32),
        compiler_params=_cparams("arbitrary"),
        name="moe_combine",
    )(pos, y, h, route, g)


def _moe_layer(h, hn, route, counts, wg, wu, wd, g):
    n_tok = h.shape[0]
    pos, tile_expert, n_valid, pad_start, pad_len, n_tiles = _moe_plan(route, counts, n_tok)
    xs = _dispatch(hn, pos, pad_start, pad_len, n_tiles * MOE_TM)
    y = _moe_experts(xs, tile_expert, n_valid, wg, wu, wd, n_tiles)
    return _combine(y, pos, h, route, g)


def _mla_layer(h, positions, mix_g, w_in, q_norm, w_uq, kv_norm, w_ukv, w_o, ffn_g, w_gate, w_up, w_down,
               next_g):
    batch, seq = positions.shape
    win_ext, wq_ext, wkv_ext = _mla_weights(w_in, w_uq, w_ukv)
    pos = positions.astype(F32).reshape(batch * seq, 1)
    q, k, v = _mla_proj(h, pos, mix_g.reshape(1, -1), win_ext, q_norm.reshape(1, -1),
                        kv_norm.reshape(1, -1), wq_ext, wkv_ext)
    o = _attention(q, k, v, batch, seq)
    return _proj_ffn(o, h, w_o.astype(BF16), ffn_g.reshape(1, -1), w_gate.astype(BF16), w_up.astype(BF16),
                     w_down.astype(BF16), next_g.reshape(1, -1))


def _s5_layer(h, hn, batch, seq, w_in, lam_re, lam_im, log_dt, b_re, b_im, c_re, c_im, d_skip, w_glu,
              ffn_g, w_router):
    u = _mm_split(hn, w_in.astype(BF16))
    kd, eb, wc, a_same, a_swap = _s5_operators(lam_re, lam_im, log_dt, b_re, b_im, c_re, c_im, d_skip)
    y = _s5_core(u, kd, eb, wc, a_same, a_swap, batch, seq)
    wr_pad = jnp.concatenate([w_router, jnp.zeros((D_MODEL, LANES - N_EXPERTS), w_router.dtype)], axis=1)
    wr_hi = wr_pad.astype(BF16)
    wr_lo = (wr_pad - wr_hi.astype(F32)).astype(BF16)
    return _glu_router(y, w_glu.astype(BF16), h, ffn_g.reshape(1, -1), jnp.concatenate([wr_hi, wr_lo], axis=1))


def kernel(x, positions, mix_norm, ffn_norm, final_norm, mla_w_in, mla_q_norm, mla_w_uq, mla_kv_norm, mla_w_ukv, mla_w_o, ffn_w_gate, ffn_w_up, ffn_w_down, s5_w_in, s5_lambda_re, s5_lambda_im, s5_log_dt, s5_b_re, s5_b_im, s5_c_re, s5_c_im, s5_d, s5_w_glu, moe_w_router, moe_w_gate, moe_w_up, moe_w_down):
    batch, seq, d = x.shape
    h0 = x.reshape(batch * seq, d)
    h2, hn2 = _mla_layer(h0, positions, mix_norm[0], mla_w_in[0], mla_q_norm[0], mla_w_uq[0],
                         mla_kv_norm[0], mla_w_ukv[0], mla_w_o[0], ffn_norm[0],
                         ffn_w_gate[0], ffn_w_up[0], ffn_w_down[0], mix_norm[1])
    h3, hn3, route, counts = _s5_layer(h2, hn2, batch, seq, s5_w_in[0], s5_lambda_re[0], s5_lambda_im[0],
                                      s5_log_dt[0], s5_b_re[0], s5_b_im[0], s5_c_re[0], s5_c_im[0], s5_d[0],
                                      s5_w_glu[0], ffn_norm[1], moe_w_router[0])
    out = _moe_layer(h3, hn3, route, counts, moe_w_gate[0].astype(BF16), moe_w_up[0].astype(BF16),
                     moe_w_down[0].astype(BF16), final_norm.reshape(1, -1))
    return out.reshape(batch, seq, d)
```

```python
import functools
import math

import jax
import jax.numpy as jnp
import numpy as np
from jax import lax
from jax.experimental import pallas as pl
from jax.experimental.pallas import tpu as pltpu

F32 = jnp.float32
BF16 = jnp.bfloat16

D_MODEL = 1024
MLA_HEADS = 16
QK_NOPE = 64
QK_ROPE = 32
V_DIM = 64
Q_LORA = 384
KV_LORA = 256
ROPE_THETA = 10000.0
HEAD_PAD = 128
ATTN_HEADS_PER_STEP = 4
S5_GROUP = 16
S5_GROUPS = D_MODEL // S5_GROUP
S5_STATE = 64
S5_CHUNK = 16
S5_SUPER = 8
N_SUPER = S5_GROUPS // S5_SUPER
N_EXPERTS = 8
EPS = 1e-6
NEG_INF = -1e30
LANES = 128
VMEM_LIMIT = 56 * 1024 * 1024


def _cparams(*sem):
    return pltpu.CompilerParams(dimension_semantics=sem, vmem_limit_bytes=VMEM_LIMIT)


def _rms(x, g):
    return x * lax.rsqrt(jnp.mean(x * x, axis=-1, keepdims=True) + EPS) * g


def _mla_proj_kernel(x_ref, pos_ref, g_ref, win_ref, qn_ref, kvn_ref, wq_ref, wkv_ref, tab_ref,
                     q_ref, k_ref, v_ref):
    hn = _rms(x_ref[...], g_ref[...]).astype(BF16)
    proj = jnp.dot(hn, win_ref[...], preferred_element_type=F32)
    cq = _rms(proj[:, :Q_LORA], qn_ref[...]).astype(BF16)
    ckv = _rms(proj[:, Q_LORA:Q_LORA + KV_LORA], kvn_ref[...]).astype(BF16)
    ka = proj[:, 640:768]
    kb = proj[:, 768:896]
    tab = tab_ref[...]
    ang = pos_ref[...] * tab[0:1, :]
    cos = jnp.cos(ang)
    sin = jnp.sin(ang)
    kp = ka * (cos * tab[1:2, :]) + kb * (sin * tab[2:3, :])
    tq = cos * tab[3:4, :] + sin * tab[4:5, :]
    q = jnp.dot(cq, wq_ref[...], preferred_element_type=F32)
    kv = jnp.dot(ckv, wkv_ref[...], preferred_element_type=F32)
    for h in range(MLA_HEADS):
        sl = slice(h * HEAD_PAD, (h + 1) * HEAD_PAD)
        q_ref[:, sl] = (q[:, sl] * tq).astype(BF16)
        k_ref[:, sl] = (kv[:, sl] + kp).astype(BF16)
    v = kv[:, MLA_HEADS * HEAD_PAD:]
    vlane = lax.broadcasted_iota(jnp.int32, v.shape, 1) % HEAD_PAD
    v_ref[...] = jnp.where(vlane == V_DIM, 1.0, v).astype(BF16)


def _mla_tables():
    inv_freq = ROPE_THETA ** (-np.arange(0, QK_ROPE, 2, dtype=np.float32) / QK_ROPE)
    inv_freq = inv_freq.astype(np.float32)
    half = QK_ROPE // 2
    tab = np.zeros((8, LANES), np.float32)
    scale = math.log2(math.e) / math.sqrt(QK_NOPE + QK_ROPE)
    tab[3, :QK_NOPE] = scale
    kc = [1.0, -1.0, 1.0, 1.0]
    ks = [-1.0, 1.0, 1.0, 1.0]
    for grp in range(4):
        sl = slice(QK_NOPE + grp * half, QK_NOPE + (grp + 1) * half)
        tab[0, sl] = inv_freq
        tab[1, sl] = kc[grp]
        tab[2, sl] = ks[grp]
        if grp % 2 == 0:
            tab[3, sl] = scale
        else:
            tab[4, sl] = scale
    return jnp.asarray(tab)


def _mla_weights(w_in, w_uq, w_ukv):
    half = QK_ROPE // 2
    base = Q_LORA + KV_LORA
    k1 = w_in[:, base:base + half]
    k2 = w_in[:, base + half:base + QK_ROPE]
    z64 = jnp.zeros((D_MODEL, QK_NOPE), w_in.dtype)
    win_ext = jnp.concatenate([w_in[:, :base], z64, k1, k1, k2, k2, z64, k2, k2, k1, k1], axis=1)
    wq = w_uq.reshape(Q_LORA, MLA_HEADS, QK_NOPE + QK_ROPE)
    q1 = wq[:, :, QK_NOPE:QK_NOPE + half]
    q2 = wq[:, :, QK_NOPE + half:]
    wq_ext = jnp.concatenate([wq[:, :, :QK_NOPE], q1, q2, q2, q1], axis=2)
    wq_ext = wq_ext.reshape(Q_LORA, MLA_HEADS * HEAD_PAD)
    wkv = w_ukv.reshape(KV_LORA, MLA_HEADS, QK_NOPE + V_DIM)
    zk = jnp.zeros((KV_LORA, MLA_HEADS, HEAD_PAD - QK_NOPE), w_ukv.dtype)
    zv = jnp.zeros((KV_LORA, MLA_HEADS, HEAD_PAD - V_DIM), w_ukv.dtype)
    wk_ext = jnp.concatenate([wkv[:, :, :QK_NOPE], zk], axis=2).reshape(KV_LORA, -1)
    wv_ext = jnp.concatenate([wkv[:, :, QK_NOPE:], zv], axis=2).reshape(KV_LORA, -1)
    wkv_ext = jnp.concatenate([wk_ext, wv_ext], axis=1)
    return win_ext.astype(BF16), wq_ext.astype(BF16), wkv_ext.astype(BF16)


def _mla_proj(x2d, pos, g, win_ext, qn, kvn, wq_ext, wkv_ext, tm=512):
    n = x2d.shape[0]
    hw = MLA_HEADS * HEAD_PAD
    full = lambda a: pl.BlockSpec(a.shape, lambda i: (0,) * a.ndim)
    tab = _mla_tables()
    return pl.pallas_call(
        _mla_proj_kernel,
        grid=(n // tm,),
        in_specs=[pl.BlockSpec((tm, D_MODEL), lambda i: (i, 0)),
                  pl.BlockSpec((tm, 1), lambda i: (i, 0)),
                  full(g), full(win_ext), full(qn), full(kvn), full(wq_ext), full(wkv_ext), full(tab)],
        out_specs=[pl.BlockSpec((tm, hw), lambda i: (i, 0))] * 3,
        out_shape=[jax.ShapeDtypeStruct((n, hw), BF16)] * 3,
        compiler_params=_cparams("parallel"),
        name="mla_proj",
    )(x2d, pos, g, win_ext, qn, kvn, wq_ext, wkv_ext, tab)


def _attn_kernel(q_ref, k_ref, v_ref, o_ref, s0_ref, s1_ref, p0_ref, p1_ref, *, blk):
    i = pl.program_id(2)
    heads = range(ATTN_HEADS_PER_STEP)
    lanes = [slice(h * HEAD_PAD, (h + 1) * HEAD_PAD) for h in heads]
    q = [q_ref[:, lanes[h]] for h in heads]

    def stage_a(j, s_ref):
        off = pl.multiple_of(j * blk, blk)
        for h in heads:
            s_ref[h] = lax.dot_general(q[h], k_ref[pl.ds(off, blk), lanes[h]], (((1,), (1,)), ((), ())),
                                       preferred_element_type=F32)

    def stage_b(state, s_ref, p_ref, masked=False):
        out = []
        for h in heads:
            m, _, acc = state[h]
            s = s_ref[h]
            if masked:
                row = lax.broadcasted_iota(jnp.int32, (blk, blk), 0)
                col = lax.broadcasted_iota(jnp.int32, (blk, blk), 1)
                s = jnp.where(col <= row, s, NEG_INF)
            m_new = jnp.maximum(m, jnp.max(s, axis=-1, keepdims=True))
            p_ref[h] = jnp.exp2((s - m_new).astype(BF16))
            out.append((m_new, jnp.exp2(m - m_new), acc))
        return tuple(out)

    def stage_c(j, state, p_ref):
        off = pl.multiple_of(j * blk, blk)
        out = []
        for h in heads:
            m, alpha, acc = state[h]
            pv = jnp.dot(p_ref[h], v_ref[pl.ds(off, blk), lanes[h]], preferred_element_type=F32)
            out.append((m, alpha, alpha * acc + pv))
        return tuple(out)

    def steady(j, state, even):
        s_j, s_n, p_j, p_n = (s0_ref, s1_ref, p0_ref, p1_ref) if even else (s1_ref, s0_ref, p1_ref, p0_ref)
        state = stage_c(j, state, p_j)
        stage_a(j + 2, s_j)
        return stage_b(state, s_n, p_n)

    def drain(state, even_last):
        s_i, p_i, p_prev = (s0_ref, p0_ref, p1_ref) if even_last else (s1_ref, p1_ref, p0_ref)
        state = stage_c(i - 1, state, p_prev)
        state = stage_b(state, s_i, p_i, masked=True)
        finish(stage_c(i, state, p_i))

    def finish(state):
        lane = lax.broadcasted_iota(jnp.int32, (blk, HEAD_PAD), 1)
        for t in range(ATTN_HEADS_PER_STEP // 2):
            lo, hi = (state[2 * t + e][2] for e in range(2))
            lo = lo / lo[:, V_DIM:V_DIM + 1]
            hi = pltpu.roll(hi / hi[:, V_DIM:V_DIM + 1], V_DIM, 1)
            o_ref[:, lanes[t]] = jnp.where(lane < V_DIM, lo, hi).astype(o_ref.dtype)

    init = tuple((jnp.full((blk, 1), NEG_INF, F32), jnp.ones((blk, 1), F32), jnp.zeros((blk, HEAD_PAD), F32))
                 for _ in heads)
    stage_a(0, s0_ref)

    @pl.when(i == 0)
    def _():
        finish(stage_c(0, stage_b(init, s0_ref, p0_ref, masked=True), p0_ref))

    @pl.when(i > 0)
    def _():
        stage_a(1, s1_ref)
        state = stage_b(init, s0_ref, p0_ref)

        def run(j0, count, st):
            for t in range(count):
                st = steady(j0 + t, st, even=(t % 2 == 0))
            return st

        n_steady = i - 1
        state = lax.fori_loop(0, n_steady // 4, lambda g, st: run(4 * g, 4, st), state)
        j_rest = 4 * (n_steady // 4)
        for rest in range(4):
            @pl.when(n_steady % 4 == rest)
            def _(rest=rest):
                drain(run(j_rest, rest, state), even_last=(rest % 2 == 1))


def _attention(q, k, v, batch, seq, blk=512):
    n, hw = q.shape
    nq = seq // blk
    nh = ATTN_HEADS_PER_STEP
    width = nh * HEAD_PAD
    return pl.pallas_call(
        functools.partial(_attn_kernel, blk=blk),
        grid=(batch, MLA_HEADS // nh, nq),
        in_specs=[pl.BlockSpec((blk, width), lambda b, h, i: (b * nq + i, h)),
                  pl.BlockSpec((seq, width), lambda b, h, i: (b, h), pipeline_mode=pl.Buffered(1)),
                  pl.BlockSpec((seq, width), lambda b, h, i: (b, h), pipeline_mode=pl.Buffered(1))],
        out_specs=pl.BlockSpec((blk, nh * V_DIM), lambda b, h, i: (b * nq + i, h)),
        out_shape=jax.ShapeDtypeStruct((n, MLA_HEADS * V_DIM), BF16),
        scratch_shapes=[pltpu.VMEM((nh, blk, blk), F32), pltpu.VMEM((nh, blk, blk), F32),
                        pltpu.VMEM((nh, blk, blk), BF16), pltpu.VMEM((nh, blk, blk), BF16)],
        compiler_params=_cparams("parallel", "parallel", "arbitrary"),
        name="mla_attention",
    )(q, k, v)


def _proj_ffn_kernel(o_ref, res_ref, wo_ref, g1_ref, wg_ref, wu_ref, wd_ref, g2_ref, out_ref, outn_ref):
    h1 = res_ref[...] + jnp.dot(o_ref[...], wo_ref[...], preferred_element_type=F32)
    hn = _rms(h1, g1_ref[...]).astype(BF16)
    gate = jnp.dot(hn, wg_ref[...], preferred_element_type=F32)
    up = jnp.dot(hn, wu_ref[...], preferred_element_type=F32)
    act = (jax.nn.silu(gate) * up).astype(BF16)
    out = h1 + jnp.dot(act, wd_ref[...], preferred_element_type=F32)
    out_ref[...] = out
    outn_ref[...] = _rms(out, g2_ref[...]).astype(outn_ref.dtype)


def _proj_ffn(o, res, wo, g1, wg, wu, wd, g2, tm=512):
    n, d = res.shape
    row = lambda a: pl.BlockSpec((tm, a.shape[1]), lambda i: (i, 0))
    resident = lambda a: pl.BlockSpec(a.shape, lambda i: (0, 0), pipeline_mode=pl.Buffered(1))
    return pl.pallas_call(
        _proj_ffn_kernel,
        grid=(n // tm,),
        in_specs=[row(o), row(res), resident(wo), resident(g1), resident(wg), resident(wu), resident(wd),
                  resident(g2)],
        out_specs=[pl.BlockSpec((tm, d), lambda i: (i, 0))] * 2,
        out_shape=[jax.ShapeDtypeStruct((n, d), F32), jax.ShapeDtypeStruct((n, d), BF16)],
        compiler_params=_cparams("parallel"),
        name="proj_swiglu_dense",
    )(o, res, wo, g1, wg, wu, wd, g2)


def _mm_split_kernel(a_ref, w_ref, o_ref, r_ref):
    tm = a_ref.shape[0]
    r = jnp.dot(a_ref[...], w_ref[...], preferred_element_type=F32)
    for g in range(N_SUPER):
        r_ref[g] = r[:, g * LANES:(g + 1) * LANES]
    for g in range(N_SUPER):
        for l in range(S5_CHUNK):
            piece = r_ref[g, pl.ds(l, tm // S5_CHUNK, stride=S5_CHUNK), :]
            o_ref[g, :, l * LANES:(l + 1) * LANES] = piece.astype(o_ref.dtype)


def _mm_split(a, w, tm=512):
    n, k = a.shape
    return pl.pallas_call(
        _mm_split_kernel,
        grid=(n // tm,),
        in_specs=[pl.BlockSpec((tm, k), lambda i: (i, 0)),
                  pl.BlockSpec(w.shape, lambda i: (0, 0))],
        out_specs=pl.BlockSpec((N_SUPER, tm // S5_CHUNK, S5_CHUNK * LANES), lambda i: (0, i, 0)),
        out_shape=jax.ShapeDtypeStruct((N_SUPER, n // S5_CHUNK, S5_CHUNK * LANES), BF16),
        scratch_shapes=[pltpu.VMEM((N_SUPER, tm, LANES), F32)],
        compiler_params=_cparams("parallel"),
        name="s5_in_proj",
    )(a, w)


def _s5_param_kernel(lr_ref, li_ref, ldt_ref, btr_ref, bti_ref, cr_ref, ci_ref, d_ref,
                     kt_ref, er_ref, ei_ref, wr_ref, wi_ref, ac_ref):
    L = S5_CHUNK
    lr = lr_ref[...]
    li = li_ref[...]
    dt = jnp.exp(ldt_ref[...])
    mag = jnp.exp(lr * dt)
    ar = mag * jnp.cos(li * dt)
    ai = mag * jnp.sin(li * dt)
    den = lr * lr + li * li
    gr = ((ar - 1.0) * lr + ai * li) / den
    gi = (ai * lr - (ar - 1.0) * li) / den
    btr = btr_ref[...]
    bti = bti_ref[...]
    bbr = gr * btr - gi * bti
    bbi = gr * bti + gi * btr
    kf = lax.broadcasted_iota(jnp.int32, (L, 1), 0).astype(F32)
    pmag = jnp.exp(lr * dt * kf)
    pr = pmag * jnp.cos(li * dt * kf)
    pi = pmag * jnp.sin(li * dt * kf)
    cr = cr_ref[...]
    ci = ci_ref[...]
    pr3 = pr[:, None, :]
    pi3 = pi[:, None, :]
    wr = cr[None] * pr3 - ci[None] * pi3
    wi = cr[None] * pi3 + ci[None] * pr3
    dn = (((1,), (1,)), ((), ()))
    hp = lax.Precision.HIGHEST
    kt = (lax.dot_general(wr.reshape(L * S5_GROUP, S5_STATE), bbr, dn, precision=hp,
                          preferred_element_type=F32)
          - lax.dot_general(wi.reshape(L * S5_GROUP, S5_STATE), bbi, dn, precision=hp,
                            preferred_element_type=F32))
    row = lax.broadcasted_iota(jnp.int32, kt.shape, 0)
    col = lax.broadcasted_iota(jnp.int32, kt.shape, 1)
    kt_ref[...] = kt + jnp.where(row == col, d_ref[...], 0.0)
    er_ref[...] = pr3 * bbr[None] - pi3 * bbi[None]
    ei_ref[...] = pr3 * bbi[None] + pi3 * bbr[None]
    wr_ref[...] = wr * ar - wi * ai
    wi_ref[...] = wr * ai + wi * ar
    lastr = pr[L - 1:L, :]
    lasti = pi[L - 1:L, :]
    ac_ref[0:1, :] = lastr * ar - lasti * ai
    ac_ref[1:2, :] = lastr * ai + lasti * ar


def _s5_operators(lam_re, lam_im, log_dt, b_re, b_im, c_re, c_im, d_skip):
    G, P, H, L = S5_GROUPS, S5_STATE, S5_GROUP, S5_CHUNK
    per_g = lambda *shape: pl.BlockSpec((None,) + shape, lambda g: (g,) + (0,) * len(shape))
    kt, er, ei, wr, wi, ac = pl.pallas_call(
        _s5_param_kernel,
        grid=(G,),
        in_specs=[per_g(1, P), per_g(1, P), per_g(1, 1), per_g(H, P), per_g(H, P),
                  per_g(H, P), per_g(H, P), per_g(1, H)],
        out_specs=[per_g(L * H, H), per_g(L, H, P), per_g(L, H, P), per_g(L, H, P), per_g(L, H, P),
                   per_g(2, P)],
        out_shape=[jax.ShapeDtypeStruct((G, L * H, H), F32)]
        + [jax.ShapeDtypeStruct((G, L, H, P), F32)] * 4
        + [jax.ShapeDtypeStruct((G, 2, P), F32)],
        compiler_params=_cparams("parallel"),
        name="s5_operators",
    )(lam_re.reshape(G, 1, P), lam_im.reshape(G, 1, P), log_dt.reshape(G, 1, 1),
      b_re.transpose(0, 2, 1), b_im.transpose(0, 2, 1), c_re, c_im, d_skip.reshape(G, 1, H))

    ns, sg = N_SUPER, S5_SUPER
    eye = jnp.eye(sg, dtype=F32)
    ktg = kt.reshape(ns, sg, L, H, H)
    kd = jnp.einsum('SgtOI,gk->StgIkO', ktg, eye).reshape(ns, L, LANES, LANES)
    ecat = jnp.stack([er, ei], axis=3)[:, ::-1]
    eb = ecat.reshape(ns, sg, L, H, 2, P).transpose(0, 2, 1, 3, 4, 5).reshape(ns, L, LANES, LANES)
    wcat = jnp.stack([wr, -wi], axis=3).reshape(ns, sg, L, H, 2, P)
    wc = wcat.transpose(0, 2, 4, 5, 1, 3).reshape(ns, L, LANES, LANES)
    acr = ac[:, 0, :].reshape(ns, sg, P)
    aci = ac[:, 1, :].reshape(ns, sg, P)
    a_same = jnp.concatenate([acr, acr], axis=-1)
    a_swap = jnp.concatenate([-aci, aci], axis=-1)
    return kd.astype(BF16), eb.astype(BF16), wc.astype(BF16), a_same, a_swap


def _group_of(shape, axis):
    return lax.broadcasted_iota(jnp.int32, shape, axis) // S5_GROUP


def _s5_inc_kernel(u_ref, eb_ref, o_ref, b_ref):
    tr = u_ref.shape[0]

    @pl.when(pl.program_id(1) == 0)
    def _():
        row_grp = _group_of((LANES, LANES), 0)
        for l in range(S5_CHUNK):
            blk = eb_ref[l]
            for g in range(S5_SUPER):
                b_ref[l * LANES:(l + 1) * LANES, g * LANES:(g + 1) * LANES] = jnp.where(
                    row_grp == g, blk, jnp.zeros_like(blk))

    inc = jnp.dot(u_ref[...], b_ref[...], preferred_element_type=F32)
    for g in range(S5_SUPER):
        o_ref[pl.ds(g, tr, stride=S5_SUPER), :] = inc[:, g * LANES:(g + 1) * LANES]


def _s5_scan_kernel(inc_ref, asame_ref, aswap_ref, x_ref, state_ref, *, cb):
    @pl.when(pl.program_id(0) == 0)
    def _():
        state_ref[...] = jnp.zeros_like(state_ref)

    nseq = state_ref.shape[0]
    a_same = asame_ref[...].reshape(nseq * S5_SUPER, LANES)
    a_swap = aswap_ref[...].reshape(nseq * S5_SUPER, LANES)

    def body(c, x):
        x_ref[:, c] = x.reshape(nseq, S5_SUPER, LANES)
        inc = inc_ref[:, c].reshape(nseq * S5_SUPER, LANES)
        return x * a_same + pltpu.roll(x, LANES // 2, 1) * a_swap + inc

    x0 = state_ref[...].reshape(nseq * S5_SUPER, LANES)
    state_ref[...] = lax.fori_loop(0, cb, body, x0).reshape(nseq, S5_SUPER, LANES)


def _s5_out_kernel(u_ref, x_ref, kd_ref, wc_ref, y_ref, ys_ref, t_ref, c_ref):
    tr = u_ref.shape[0]

    @pl.when(pl.program_id(1) == 0)
    def _():
        zero = jnp.zeros((LANES, LANES), t_ref.dtype)
        col_grp = _group_of((LANES, LANES), 1)
        for l_out in range(S5_CHUNK):
            cols = slice(l_out * LANES, (l_out + 1) * LANES)
            for l_in in range(S5_CHUNK):
                t_ref[l_in * LANES:(l_in + 1) * LANES, cols] = kd_ref[l_out - l_in] if l_out >= l_in else zero
            blk = wc_ref[l_out]
            for g in range(S5_SUPER):
                c_ref[g * LANES:(g + 1) * LANES, cols] = jnp.where(col_grp == g, blk, zero)

    x = jnp.concatenate([x_ref[pl.ds(g, tr, stride=S5_SUPER), :] for g in range(S5_SUPER)], axis=-1)
    u = u_ref[...]
    pair = 2 * LANES
    y = jnp.concatenate(
        [jnp.dot(u[:, :(b + 1) * pair], t_ref[:(b + 1) * pair, b * pair:(b + 1) * pair],
                 preferred_element_type=F32) for b in range(S5_CHUNK // 2)], axis=-1)
    y += jnp.dot(x.astype(BF16), c_ref[...], preferred_element_type=F32)
    y = jax.nn.gelu(y)
    for l in range(S5_CHUNK):
        ys_ref[pl.ds(l, tr, stride=S5_CHUNK), :] = y[:, l * LANES:(l + 1) * LANES]
    y_ref[...] = ys_ref[...].astype(y_ref.dtype)


def _s5_core(uc, kd, eb, wc, a_same, a_swap, batch, seq, tr=512):
    ns, rows, width = uc.shape
    L = S5_CHUNK
    n = rows * L
    n_chunks = seq // L
    sw = 2 * S5_SUPER * S5_STATE
    tr = min(tr, rows)
    blocks_spec = pl.BlockSpec((None, L, LANES, LANES), lambda s, r: (s, 0, 0, 0))
    inc = pl.pallas_call(
        _s5_inc_kernel,
        grid=(ns, rows // tr),
        in_specs=[pl.BlockSpec((None, tr, width), lambda s, r: (s, r, 0)), blocks_spec],
        out_specs=pl.BlockSpec((None, tr * S5_SUPER, LANES), lambda s, r: (s, r, 0)),
        out_shape=jax.ShapeDtypeStruct((ns, rows * S5_SUPER, LANES), F32),
        scratch_shapes=[pltpu.VMEM((width, sw), BF16)],
        compiler_params=_cparams("parallel", "arbitrary"),
        name="s5_chunk_state",
    )(uc, eb)
    nseq = ns * batch
    cb = min(32, n_chunks)
    tile = (S5_SUPER, LANES)
    bcast = lambda a: jnp.broadcast_to(a[:, None], (ns, batch) + tile).reshape((nseq,) + tile)
    seq_spec = pl.BlockSpec((nseq, cb) + tile, lambda c: (0, c, 0, 0))
    tab_spec = pl.BlockSpec((nseq,) + tile, lambda c: (0, 0, 0))
    xprev = pl.pallas_call(
        functools.partial(_s5_scan_kernel, cb=cb),
        grid=(n_chunks // cb,),
        in_specs=[seq_spec, tab_spec, tab_spec],
        out_specs=seq_spec,
        out_shape=jax.ShapeDtypeStruct((nseq, n_chunks) + tile, F32),
        scratch_shapes=[pltpu.VMEM((nseq,) + tile, F32)],
        compiler_params=_cparams("arbitrary"),
        name="s5_chunk_scan",
    )(inc.reshape((nseq, n_chunks) + tile), bcast(a_same), bcast(a_swap))
    xprev = xprev.reshape(ns, rows * S5_SUPER, LANES)
    return pl.pallas_call(
        _s5_out_kernel,
        grid=(ns, rows // tr),
        in_specs=[pl.BlockSpec((None, tr, width), lambda s, r: (s, r, 0)),
                  pl.BlockSpec((None, tr * S5_SUPER, LANES), lambda s, r: (s, r, 0)),
                  blocks_spec, blocks_spec],
        out_specs=pl.BlockSpec((None, tr * L, LANES), lambda s, r: (s, r, 0)),
        out_shape=jax.ShapeDtypeStruct((ns, n, LANES), BF16),
        scratch_shapes=[pltpu.VMEM((tr * L, LANES), F32), pltpu.VMEM((width, width), BF16),
                        pltpu.VMEM((sw, width), BF16)],
        compiler_params=_cparams("parallel", "arbitrary"),
        name="s5_chunk_out",
    )(uc, xprev, kd, wc)


def _glu_router_kernel(y_ref, w_ref, res_ref, g_ref, wr_ref, tri_ref, h_ref, hn_ref, route_ref, cnt_ref,
                       count_ref):
    d = res_ref.shape[1]
    y = jnp.concatenate([y_ref[g] for g in range(N_SUPER)], axis=-1)
    z = jnp.dot(y, w_ref[...], preferred_element_type=F32)
    h = res_ref[...] + z[:, :d] * jax.nn.sigmoid(z[:, d:])
    h_ref[...] = h
    hn = _rms(h, g_ref[...])
    hn_ref[...] = hn.astype(hn_ref.dtype)
    hi = hn.astype(BF16)
    lo = (hn - hi.astype(F32)).astype(BF16)
    wr = wr_ref[...]
    first = jnp.dot(hi, wr, preferred_element_type=F32)
    logits = (first[:, :LANES] + first[:, LANES:]) + jnp.dot(lo, wr[:, :LANES], preferred_element_type=F32)
    lane = lax.broadcasted_iota(jnp.int32, logits.shape, 1)
    logits = jnp.where(lane < N_EXPERTS, logits, -jnp.inf)
    m1 = jnp.max(logits, axis=-1, keepdims=True)
    i1 = jnp.min(jnp.where(logits == m1, lane, LANES), axis=-1, keepdims=True)
    rest = jnp.where(lane == i1, -jnp.inf, logits)
    m2 = jnp.max(rest, axis=-1, keepdims=True)
    i2 = jnp.min(jnp.where(rest == m2, lane, LANES), axis=-1, keepdims=True)
    e2 = jnp.exp(m2 - m1)
    g1 = 1.0 / (1.0 + e2)

    @pl.when(pl.program_id(0) == 0)
    def _():
        count_ref[...] = jnp.zeros_like(count_ref)

    chosen = jnp.where(jnp.logical_or(lane == i1, lane == i2), 1.0, 0.0)
    before = jnp.dot(tri_ref[...], chosen.astype(BF16), preferred_element_type=F32) + count_ref[0:1, :]
    r1 = jnp.sum(jnp.where(lane == i1, before, 0.0), axis=-1, keepdims=True)
    r2 = jnp.sum(jnp.where(lane == i2, before, 0.0), axis=-1, keepdims=True)
    count_ref[0:1, :] += jnp.sum(chosen, axis=0, keepdims=True)
    cnt_ref[...] = count_ref[...]
    route_ref[...] = (jnp.where(lane == 0, i1.astype(F32), 0.0) + jnp.where(lane == 1, i2.astype(F32), 0.0)
                      + jnp.where(lane == 2, g1, 0.0) + jnp.where(lane == 3, e2 * g1, 0.0)
                      + jnp.where(lane == 4, r1, 0.0) + jnp.where(lane == 5, r2, 0.0))


def _glu_router(y, w_glu, res, g, w_router_pad, tm=512):
    ns, n, _ = y.shape
    d = res.shape[1]
    tri = jnp.asarray(np.tril(np.ones((tm, tm), np.float32), -1), BF16)
    return pl.pallas_call(
        _glu_router_kernel,
        grid=(n // tm,),
        in_specs=[pl.BlockSpec((ns, tm, LANES), lambda i: (0, i, 0)),
                  pl.BlockSpec(w_glu.shape, lambda i: (0, 0)),
                  pl.BlockSpec((tm, d), lambda i: (i, 0)),
                  pl.BlockSpec((1, d), lambda i: (0, 0)),
                  pl.BlockSpec(w_router_pad.shape, lambda i: (0, 0)),
                  pl.BlockSpec((tm, tm), lambda i: (0, 0))],
        out_specs=[pl.BlockSpec((tm, d), lambda i: (i, 0)),
                   pl.BlockSpec((tm, d), lambda i: (i, 0)),
                   pl.BlockSpec((tm, LANES), lambda i: (i, 0)),
                   pl.BlockSpec((8, LANES), lambda i: (0, 0))],
        out_shape=[jax.ShapeDtypeStruct((n, d), F32), jax.ShapeDtypeStruct((n, d), F32),
                   jax.ShapeDtypeStruct((n, LANES), F32), jax.ShapeDtypeStruct((8, LANES), F32)],
        scratch_shapes=[pltpu.VMEM((8, LANES), F32)],
        compiler_params=_cparams("arbitrary"),
        name="s5_glu_router",
    )(y, w_glu, res, g, w_router_pad, tri)


MOE_TM = 512
MOE_TF = 1792
DISPATCH_BLOCK = 512
COMBINE_TM = 512


def _moe_plan(route, counts, n_tok):
    ne, tm = N_EXPERTS, MOE_TM
    n_tiles = (2 * n_tok) // tm + ne
    eid = route[:, :2].astype(jnp.int32).reshape(-1)
    rank = route[:, 4:6].astype(jnp.int32).reshape(-1)
    onehot = (eid[:, None] == jnp.arange(ne, dtype=jnp.int32)[None, :]).astype(jnp.int32)
    cnt = counts[0, :ne].astype(jnp.int32)
    padded = ((cnt + tm - 1) // tm) * tm
    ends = jnp.cumsum(padded)
    off = ends - padded
    pos = (jnp.sum(off[None, :] * onehot, axis=1) + rank).astype(jnp.int32)
    tile_end = ends // tm
    n_valid = tile_end[-1:].astype(jnp.int32)
    tiles = jnp.arange(n_tiles, dtype=jnp.int32)
    tile_expert = jnp.minimum(jnp.sum(tiles[:, None] >= tile_end[None, :], axis=1), ne - 1).astype(jnp.int32)
    pad_start = (off + cnt).astype(jnp.int32)
    pad_len = (padded - cnt).astype(jnp.int32)
    pad_len = pad_len.at[ne - 1].add((n_tiles * tm - ends[-1]).astype(jnp.int32))
    return pos, tile_expert, n_valid, pad_start, pad_len, n_tiles


def _dispatch_kernel(pos_ref, pstart_ref, plen_ref, hn_ref, xs_ref, zero_ref, sem, *, n_tok):
    nb = n_tok // DISPATCH_BLOCK
    b = pl.program_id(0)

    def row_copy(r, k):
        dst = pos_ref[2 * (b * DISPATCH_BLOCK + r) + k]
        return pltpu.make_async_copy(hn_ref.at[pl.ds(r, 1)], xs_ref.at[pl.ds(dst, 1)], sem)

    for r in range(DISPATCH_BLOCK):
        row_copy(r, 0).start()
        row_copy(r, 1).start()

    for _ in range(2):
        pltpu.make_async_copy(hn_ref, xs_ref.at[pl.ds(0, DISPATCH_BLOCK)], sem).wait()

    @pl.when(b == nb - 1)
    def _():
        zero_ref[...] = jnp.zeros_like(zero_ref)

        def pad_copy(e, r):
            return pltpu.make_async_copy(zero_ref.at[pl.ds(0, 1)], xs_ref.at[pl.ds(pstart_ref[e] + r, 1)], sem)

        for e in range(N_EXPERTS):
            def pbody(r, _, e=e):
                pad_copy(e, r).start()
                return 0
            lax.fori_loop(0, plen_ref[e], pbody, 0)
        for e in range(N_EXPERTS):
            def wbody(r, _, e=e):
                pad_copy(e, r).wait()
                return 0
            lax.fori_loop(0, plen_ref[e], wbody, 0)


def _dispatch(hn, pos, pad_start, pad_len, n_rows):
    n_tok, d = hn.shape
    return pl.pallas_call(
        functools.partial(_dispatch_kernel, n_tok=n_tok),
        grid_spec=pltpu.PrefetchScalarGridSpec(
            num_scalar_prefetch=3,
            grid=(n_tok // DISPATCH_BLOCK,),
            in_specs=[pl.BlockSpec((DISPATCH_BLOCK, d), lambda b, pos, ps, plen: (b, 0))],
            out_specs=pl.BlockSpec(memory_space=pl.ANY),
            scratch_shapes=[pltpu.VMEM((8, d), F32), pltpu.SemaphoreType.DMA]),
        out_shape=jax.ShapeDtypeStruct((n_rows, d), F32),
        compiler_params=_cparams("arbitrary"),
        name="moe_dispatch",
    )(pos, pad_start, pad_len, hn)


def _moe_kernel(te_ref, nv_ref, x_ref, wg_ref, wu_ref, wd_ref, y_ref, xb_ref):
    i = pl.program_id(0)
    k = pl.program_id(1)

    @pl.when(i < nv_ref[0])
    def _():
        @pl.when(k == 0)
        def _():
            xb_ref[...] = x_ref[...].astype(BF16)

        hn = xb_ref[...]
        gate = jnp.dot(hn, wg_ref[...], preferred_element_type=F32)
        up = jnp.dot(hn, wu_ref[...], preferred_element_type=F32)
        act = (jax.nn.silu(gate) * up).astype(BF16)
        part = jnp.dot(act, wd_ref[...], preferred_element_type=F32)

        @pl.when(k == 0)
        def _():
            y_ref[...] = part

        @pl.when(k > 0)
        def _():
            y_ref[...] += part

    @pl.when(jnp.logical_and(i >= nv_ref[0], k == 0))
    def _():
        y_ref[...] = jnp.zeros_like(y_ref)


def _moe_experts(xs, tile_expert, n_valid, wg, wu, wd, n_tiles):
    tm, tf = MOE_TM, MOE_TF
    d = xs.shape[1]
    f = wg.shape[2]
    nk = f // tf

    def row_map(i, k, te, nv):
        return (jnp.minimum(i, nv[0] - 1), 0)

    def kk(i, k, nv):
        return jnp.where(i < nv[0], k, nk - 1)

    def w_in_map(i, k, te, nv):
        return (te[jnp.minimum(i, nv[0] - 1)], 0, kk(i, k, nv))

    def w_out_map(i, k, te, nv):
        return (te[jnp.minimum(i, nv[0] - 1)], kk(i, k, nv), 0)

    return pl.pallas_call(
        _moe_kernel,
        grid_spec=pltpu.PrefetchScalarGridSpec(
            num_scalar_prefetch=2,
            grid=(n_tiles, nk),
            in_specs=[pl.BlockSpec((tm, d), row_map),
                      pl.BlockSpec((None, d, tf), w_in_map),
                      pl.BlockSpec((None, d, tf), w_in_map),
                      pl.BlockSpec((None, tf, d), w_out_map)],
            out_specs=pl.BlockSpec((tm, d), lambda i, k, te, nv: (i, 0)),
            scratch_shapes=[pltpu.VMEM((tm, d), BF16)]),
        out_shape=jax.ShapeDtypeStruct(xs.shape, F32),
        compiler_params=_cparams("arbitrary", "arbitrary"),
        name="moe_experts",
    )(tile_expert, n_valid, xs, wg, wu, wd)


def _combine_kernel(pos_ref, y_ref, h_ref, route_ref, g_ref, out_ref, buf_ref, sem, *, tm, n_steps):
    i = pl.program_id(0)
    slot = i % 2

    def row_copy(step, sl, r, k):
        t = step * tm + r
        return pltpu.make_async_copy(y_ref.at[pl.ds(pos_ref[2 * t + k], 1)],
                                     buf_ref.at[sl, k, pl.ds(r, 1)], sem.at[sl])

    def issue(step, sl):
        for r in range(tm):
            row_copy(step, sl, r, 0).start()
            row_copy(step, sl, r, 1).start()

    @pl.when(i == 0)
    def _():
        issue(0, 0)

    @pl.when(i + 1 < n_steps)
    def _():
        issue(i + 1, 1 - slot)

    for k in range(2):
        pltpu.make_async_copy(y_ref.at[pl.ds(0, tm)], buf_ref.at[slot, k], sem.at[slot]).wait()

    route = route_ref[...]
    out = h_ref[...] + route[:, 2:3] * buf_ref[slot, 0] + route[:, 3:4] * buf_ref[slot, 1]
    out_ref[...] = _rms(out, g_ref[...])


def _combine(y, pos, h, route, g):
    n_tok, d = h.shape
    tm = COMBINE_TM
    n_steps = n_tok // tm
    return pl.pallas_call(
        functools.partial(_combine_kernel, tm=tm, n_steps=n_steps),
        grid_spec=pltpu.PrefetchScalarGridSpec(
            num_scalar_prefetch=1,
            grid=(n_steps,),
            in_specs=[pl.BlockSpec(memory_space=pl.ANY),
                      pl.BlockSpec((tm, d), lambda i, pos: (i, 0)),
                      pl.BlockSpec((tm, LANES), lambda i, pos: (i, 0)),
                      pl.BlockSpec((1, d), lambda i, pos: (0, 0))],
            out_specs=pl.BlockSpec((tm, d), lambda i, pos: (i, 0)),
            scratch_shapes=[pltpu.VMEM((2, 2, tm, d), F32), pltpu.SemaphoreType.DMA((2,))]),
        out_shape=jax.ShapeDtypeStruct((n_tok, d), F32),
        compiler_params=_cparams("arbitrary"),
        name="moe_combine",
    )(pos, y, h, route, g)


def _moe_layer(h, hn, route, counts, wg, wu, wd, g):
    n_tok = h.shape[0]
    pos, tile_expert, n_valid, pad_start, pad_len, n_tiles = _moe_plan(route, counts, n_tok)
    xs = _dispatch(hn, pos, pad_start, pad_len, n_tiles * MOE_TM)
    y = _moe_experts(xs, tile_expert, n_valid, wg, wu, wd, n_tiles)
    return _combine(y, pos, h, route, g)


def _mla_layer(h, positions, mix_g, w_in, q_norm, w_uq, kv_norm, w_ukv, w_o, ffn_g, w_gate, w_up, w_down,
               next_g):
    batch, seq = positions.shape
    win_ext, wq_ext, wkv_ext = _mla_weights(w_in, w_uq, w_ukv)
    pos = positions.astype(F32).reshape(batch * seq, 1)
    q, k, v = _mla_proj(h, pos, mix_g.reshape(1, -1), win_ext, q_norm.reshape(1, -1),
                        kv_norm.reshape(1, -1), wq_ext, wkv_ext)
    o = _attention(q, k, v, batch, seq)
    return _proj_ffn(o, h, w_o.astype(BF16), ffn_g.reshape(1, -1), w_gate.astype(BF16), w_up.astype(BF16),
                     w_down.astype(BF16), next_g.reshape(1, -1))


def _s5_layer(h, hn, batch, seq, w_in, lam_re, lam_im, log_dt, b_re, b_im, c_re, c_im, d_skip, w_glu,
              ffn_g, w_router):
    u = _mm_split(hn, w_in.astype(BF16))
    kd, eb, wc, a_same, a_swap = _s5_operators(lam_re, lam_im, log_dt, b_re, b_im, c_re, c_im, d_skip)
    y = _s5_core(u, kd, eb, wc, a_same, a_swap, batch, seq)
    wr_pad = jnp.concatenate([w_router, jnp.zeros((D_MODEL, LANES - N_EXPERTS), w_router.dtype)], axis=1)
    wr_hi = wr_pad.astype(BF16)
    wr_lo = (wr_pad - wr_hi.astype(F32)).astype(BF16)
    return _glu_router(y, w_glu.astype(BF16), h, ffn_g.reshape(1, -1), jnp.concatenate([wr_hi, wr_lo], axis=1))


def kernel(x, positions, mix_norm, ffn_norm, final_norm, mla_w_in, mla_q_norm, mla_w_uq, mla_kv_norm, mla_w_ukv, mla_w_o, ffn_w_gate, ffn_w_up, ffn_w_down, s5_w_in, s5_lambda_re, s5_lambda_im, s5_log_dt, s5_b_re, s5_b_im, s5_c_re, s5_c_im, s5_d, s5_w_glu, moe_w_router, moe_w_gate, moe_w_up, moe_w_down):
    batch, seq, d = x.shape
    h0 = x.reshape(batch * seq, d)
    h2, hn2 = _mla_layer(h0, positions, mix_norm[0], mla_w_in[0], mla_q_norm[0], mla_w_uq[0],
                         mla_kv_norm[0], mla_w_ukv[0], mla_w_o[0], ffn_norm[0],
                         ffn_w_gate[0], ffn_w_up[0], ffn_w_down[0], mix_norm[1])
    h3, hn3, route, counts = _s5_layer(h2, hn2, batch, seq, s5_w_in[0], s5_lambda_re[0], s5_lambda_im[0],
                                      s5_log_dt[0], s5_b_re[0], s5_b_im[0], s5_c_re[0], s5_c_im[0], s5_d[0],
                                      s5_w_glu[0], ffn_norm[1], moe_w_router[0])
    out = _moe_layer(h3, hn3, route, counts, moe_w_gate[0].astype(BF16), moe_w_up[0].astype(BF16),
                     moe_w_down[0].astype(BF16), final_norm.reshape(1, -1))
    return out.reshape(batch, seq, d)
```

```python
import functools
import math

import jax
import jax.numpy as jnp
import numpy as np
from jax import lax
from jax.experimental import pallas as pl
from jax.experimental.pallas import tpu as pltpu

F32 = jnp.float32
BF16 = jnp.bfloat16

D_MODEL = 1024
MLA_HEADS = 16
QK_NOPE = 64
QK_ROPE = 32
V_DIM = 64
Q_LORA = 384
KV_LORA = 256
ROPE_THETA = 10000.0
HEAD_PAD = 128
ATTN_HEADS_PER_STEP = 4
S5_GROUP = 16
S5_GROUPS = D_MODEL // S5_GROUP
S5_STATE = 64
S5_CHUNK = 16
S5_SUPER = 8
N_SUPER = S5_GROUPS // S5_SUPER
N_EXPERTS = 8
EPS = 1e-6
NEG_INF = -1e30
LANES = 128
VMEM_LIMIT = 56 * 1024 * 1024


def _cparams(*sem):
    return pltpu.CompilerParams(dimension_semantics=sem, vmem_limit_bytes=VMEM_LIMIT)


def _rms(x, g):
    return x * lax.rsqrt(jnp.mean(x * x, axis=-1, keepdims=True) + EPS) * g


def _mla_proj_kernel(x_ref, pos_ref, g_ref, win_ref, qn_ref, kvn_ref, wq_ref, wkv_ref, tab_ref,
                     q_ref, k_ref, v_ref):
    hn = _rms(x_ref[...], g_ref[...]).astype(BF16)
    proj = jnp.dot(hn, win_ref[...], preferred_element_type=F32)
    cq = _rms(proj[:, :Q_LORA], qn_ref[...]).astype(BF16)
    ckv = _rms(proj[:, Q_LORA:Q_LORA + KV_LORA], kvn_ref[...]).astype(BF16)
    ka = proj[:, 640:768]
    kb = proj[:, 768:896]
    tab = tab_ref[...]
    ang = pos_ref[...] * tab[0:1, :]
    cos = jnp.cos(ang)
    sin = jnp.sin(ang)
    kp = ka * (cos * tab[1:2, :]) + kb * (sin * tab[2:3, :])
    tq = cos * tab[3:4, :] + sin * tab[4:5, :]
    q = jnp.dot(cq, wq_ref[...], preferred_element_type=F32)
    kv = jnp.dot(ckv, wkv_ref[...], preferred_element_type=F32)
    for h in range(MLA_HEADS):
        sl = slice(h * HEAD_PAD, (h + 1) * HEAD_PAD)
        q_ref[:, sl] = (q[:, sl] * tq).astype(BF16)
        k_ref[:, sl] = (kv[:, sl] + kp).astype(BF16)
    v = kv[:, MLA_HEADS * HEAD_PAD:]
    vlane = lax.broadcasted_iota(jnp.int32, v.shape, 1) % HEAD_PAD
    v_ref[...] = jnp.where(vlane == V_DIM, 1.0, v).astype(BF16)


def _mla_tables():
    inv_freq = ROPE_THETA ** (-np.arange(0, QK_ROPE, 2, dtype=np.float32) / QK_ROPE)
    inv_freq = inv_freq.astype(np.float32)
    half = QK_ROPE // 2
    tab = np.zeros((8, LANES), np.float32)
    scale = math.log2(math.e) / math.sqrt(QK_NOPE + QK_ROPE)
    tab[3, :QK_NOPE] = scale
    kc = [1.0, -1.0, 1.0, 1.0]
    ks = [-1.0, 1.0, 1.0, 1.0]
    for grp in range(4):
        sl = slice(QK_NOPE + grp * half, QK_NOPE + (grp + 1) * half)
        tab[0, sl] = inv_freq
        tab[1, sl] = kc[grp]
        tab[2, sl] = ks[grp]
        if grp % 2 == 0:
            tab[3, sl] = scale
        else:
            tab[4, sl] = scale
    return jnp.asarray(tab)


def _mla_weights(w_in, w_uq, w_ukv):
    half = QK_ROPE // 2
    base = Q_LORA + KV_LORA
    k1 = w_in[:, base:base + half]
    k2 = w_in[:, base + half:base + QK_ROPE]
    z64 = jnp.zeros((D_MODEL, QK_NOPE), w_in.dtype)
    win_ext = jnp.concatenate([w_in[:, :base], z64, k1, k1, k2, k2, z64, k2, k2, k1, k1], axis=1)
    wq = w_uq.reshape(Q_LORA, MLA_HEADS, QK_NOPE + QK_ROPE)
    q1 = wq[:, :, QK_NOPE:QK_NOPE + half]
    q2 = wq[:, :, QK_NOPE + half:]
    wq_ext = jnp.concatenate([wq[:, :, :QK_NOPE], q1, q2, q2, q1], axis=2)
    wq_ext = wq_ext.reshape(Q_LORA, MLA_HEADS * HEAD_PAD)
    wkv = w_ukv.reshape(KV_LORA, MLA_HEADS, QK_NOPE + V_DIM)
    zk = jnp.zeros((KV_LORA, MLA_HEADS, HEAD_PAD - QK_NOPE), w_ukv.dtype)
    zv = jnp.zeros((KV_LORA, MLA_HEADS, HEAD_PAD - V_DIM), w_ukv.dtype)
    wk_ext = jnp.concatenate([wkv[:, :, :QK_NOPE], zk], axis=2).reshape(KV_LORA, -1)
    wv_ext = jnp.concatenate([wkv[:, :, QK_NOPE:], zv], axis=2).reshape(KV_LORA, -1)
    wkv_ext = jnp.concatenate([wk_ext, wv_ext], axis=1)
    return win_ext.astype(BF16), wq_ext.astype(BF16), wkv_ext.astype(BF16)


def _mla_proj(x2d, pos, g, win_ext, qn, kvn, wq_ext, wkv_ext, tm=512):
    n = x2d.shape[0]
    hw = MLA_HEADS * HEAD_PAD
    full = lambda a: pl.BlockSpec(a.shape, lambda i: (0,) * a.ndim)
    tab = _mla_tables()
    return pl.pallas_call(
        _mla_proj_kernel,
        grid=(n // tm,),
        in_specs=[pl.BlockSpec((tm, D_MODEL), lambda i: (i, 0)),
                  pl.BlockSpec((tm, 1), lambda i: (i, 0)),
                  full(g), full(win_ext), full(qn), full(kvn), full(wq_ext), full(wkv_ext), full(tab)],
        out_specs=[pl.BlockSpec((tm, hw), lambda i: (i, 0))] * 3,
        out_shape=[jax.ShapeDtypeStruct((n, hw), BF16)] * 3,
        compiler_params=_cparams("parallel"),
        name="mla_proj",
    )(x2d, pos, g, win_ext, qn, kvn, wq_ext, wkv_ext, tab)


def _attn_kernel(q_ref, k_ref, v_ref, o_ref, s0_ref, s1_ref, p0_ref, p1_ref, *, blk):
    i = pl.program_id(2)
    heads = range(ATTN_HEADS_PER_STEP)
    lanes = [slice(h * HEAD_PAD, (h + 1) * HEAD_PAD) for h in heads]
    q = [q_ref[:, lanes[h]] for h in heads]

    def stage_a(j, s_ref):
        off = pl.multiple_of(j * blk, blk)
        for h in heads:
            s_ref[h] = lax.dot_general(q[h], k_ref[pl.ds(off, blk), lanes[h]], (((1,), (1,)), ((), ())),
                                       preferred_element_type=F32)

    def stage_b(state, s_ref, p_ref, masked=False):
        out = []
        for h in heads:
            m, _, acc = state[h]
            s = s_ref[h]
            if masked:
                row = lax.broadcasted_iota(jnp.int32, (blk, blk), 0)
                col = lax.broadcasted_iota(jnp.int32, (blk, blk), 1)
                s = jnp.where(col <= row, s, NEG_INF)
            m_new = jnp.maximum(m, jnp.max(s, axis=-1, keepdims=True))
            p_ref[h] = jnp.exp2((s - m_new).astype(BF16))
            out.append((m_new, jnp.exp2(m - m_new), acc))
        return tuple(out)

    def stage_c(j, state, p_ref):
        off = pl.multiple_of(j * blk, blk)
        out = []
        for h in heads:
            m, alpha, acc = state[h]
            pv = jnp.dot(p_ref[h], v_ref[pl.ds(off, blk), lanes[h]], preferred_element_type=F32)
            out.append((m, alpha, alpha * acc + pv))
        return tuple(out)

    def steady(j, state, even):
        s_j, s_n, p_j, p_n = (s0_ref, s1_ref, p0_ref, p1_ref) if even else (s1_ref, s0_ref, p1_ref, p0_ref)
        state = stage_c(j, state, p_j)
        stage_a(j + 2, s_j)
        return stage_b(state, s_n, p_n)

    def drain(state, even_last):
        s_i, p_i, p_prev = (s0_ref, p0_ref, p1_ref) if even_last else (s1_ref, p1_ref, p0_ref)
        state = stage_c(i - 1, state, p_prev)
        state = stage_b(state, s_i, p_i, masked=True)
        finish(stage_c(i, state, p_i))

    def finish(state):
        lane = lax.broadcasted_iota(jnp.int32, (blk, HEAD_PAD), 1)
        for t in range(ATTN_HEADS_PER_STEP // 2):
            lo, hi = (state[2 * t + e][2] for e in range(2))
            lo = lo / lo[:, V_DIM:V_DIM + 1]
            hi = pltpu.roll(hi / hi[:, V_DIM:V_DIM + 1], V_DIM, 1)
            o_ref[:, lanes[t]] = jnp.where(lane < V_DIM, lo, hi).astype(o_ref.dtype)

    init = tuple((jnp.full((blk, 1), NEG_INF, F32), jnp.ones((blk, 1), F32), jnp.zeros((blk, HEAD_PAD), F32))
                 for _ in heads)
    stage_a(0, s0_ref)

    @pl.when(i == 0)
    def _():
        finish(stage_c(0, stage_b(init, s0_ref, p0_ref, masked=True), p0_ref))

    @pl.when(i > 0)
    def _():
        stage_a(1, s1_ref)
        state = stage_b(init, s0_ref, p0_ref)

        def run(j0, count, st):
            for t in range(count):
                st = steady(j0 + t, st, even=(t % 2 == 0))
            return st

        n_steady = i - 1
        state = lax.fori_loop(0, n_steady // 4, lambda g, st: run(4 * g, 4, st), state)
        j_rest = 4 * (n_steady // 4)
        for rest in range(4):
            @pl.when(n_steady % 4 == rest)
            def _(rest=rest):
                drain(run(j_rest, rest, state), even_last=(rest % 2 == 1))


def _attention(q, k, v, batch, seq, blk=512):
    n, hw = q.shape
    nq = seq // blk
    nh = ATTN_HEADS_PER_STEP
    width = nh * HEAD_PAD
    return pl.pallas_call(
        functools.partial(_attn_kernel, blk=blk),
        grid=(batch, MLA_HEADS // nh, nq),
        in_specs=[pl.BlockSpec((blk, width), lambda b, h, i: (b * nq + i, h)),
                  pl.BlockSpec((seq, width), lambda b, h, i: (b, h), pipeline_mode=pl.Buffered(1)),
                  pl.BlockSpec((seq, width), lambda b, h, i: (b, h), pipeline_mode=pl.Buffered(1))],
        out_specs=pl.BlockSpec((blk, nh * V_DIM), lambda b, h, i: (b * nq + i, h)),
        out_shape=jax.ShapeDtypeStruct((n, MLA_HEADS * V_DIM), BF16),
        scratch_shapes=[pltpu.VMEM((nh, blk, blk), F32), pltpu.VMEM((nh, blk, blk), F32),
                        pltpu.VMEM((nh, blk, blk), BF16), pltpu.VMEM((nh, blk, blk), BF16)],
        compiler_params=_cparams("parallel", "parallel", "arbitrary"),
        name="mla_attention",
    )(q, k, v)


def _proj_ffn_kernel(o_ref, res_ref, wo_ref, g1_ref, wg_ref, wu_ref, wd_ref, g2_ref, out_ref, outn_ref):
    h1 = res_ref[...] + jnp.dot(o_ref[...], wo_ref[...], preferred_element_type=F32)
    hn = _rms(h1, g1_ref[...]).astype(BF16)
    gate = jnp.dot(hn, wg_ref[...], preferred_element_type=F32)
    up = jnp.dot(hn, wu_ref[...], preferred_element_type=F32)
    act = (jax.nn.silu(gate) * up).astype(BF16)
    out = h1 + jnp.dot(act, wd_ref[...], preferred_element_type=F32)
    out_ref[...] = out
    outn_ref[...] = _rms(out, g2_ref[...]).astype(outn_ref.dtype)


def _proj_ffn(o, res, wo, g1, wg, wu, wd, g2, tm=512):
    n, d = res.shape
    row = lambda a: pl.BlockSpec((tm, a.shape[1]), lambda i: (i, 0))
    resident = lambda a: pl.BlockSpec(a.shape, lambda i: (0, 0), pipeline_mode=pl.Buffered(1))
    return pl.pallas_call(
        _proj_ffn_kernel,
        grid=(n // tm,),
        in_specs=[row(o), row(res), resident(wo), resident(g1), resident(wg), resident(wu), resident(wd),
                  resident(g2)],
        out_specs=[pl.BlockSpec((tm, d), lambda i: (i, 0))] * 2,
        out_shape=[jax.ShapeDtypeStruct((n, d), F32), jax.ShapeDtypeStruct((n, d), BF16)],
        compiler_params=_cparams("parallel"),
        name="proj_swiglu_dense",
    )(o, res, wo, g1, wg, wu, wd, g2)


def _mm_split_kernel(a_ref, w_ref, o_ref, r_ref):
    tm = a_ref.shape[0]
    r = jnp.dot(a_ref[...], w_ref[...], preferred_element_type=F32)
    for g in range(N_SUPER):
        r_ref[g] = r[:, g * LANES:(g + 1) * LANES]
    for g in range(N_SUPER):
        for l in range(S5_CHUNK):
            piece = r_ref[g, pl.ds(l, tm // S5_CHUNK, stride=S5_CHUNK), :]
            o_ref[g, :, l * LANES:(l + 1) * LANES] = piece.astype(o_ref.dtype)


def _mm_split(a, w, tm=512):
    n, k = a.shape
    return pl.pallas_call(
        _mm_split_kernel,
        grid=(n // tm,),
        in_specs=[pl.BlockSpec((tm, k), lambda i: (i, 0)),
                  pl.BlockSpec(w.shape, lambda i: (0, 0))],
        out_specs=pl.BlockSpec((N_SUPER, tm // S5_CHUNK, S5_CHUNK * LANES), lambda i: (0, i, 0)),
        out_shape=jax.ShapeDtypeStruct((N_SUPER, n // S5_CHUNK, S5_CHUNK * LANES), BF16),
        scratch_shapes=[pltpu.VMEM((N_SUPER, tm, LANES), F32)],
        compiler_params=_cparams("parallel"),
        name="s5_in_proj",
    )(a, w)


def _s5_param_kernel(lr_ref, li_ref, ldt_ref, btr_ref, bti_ref, cr_ref, ci_ref, d_ref,
                     kt_ref, er_ref, ei_ref, wr_ref, wi_ref, ac_ref):
    L = S5_CHUNK
    lr = lr_ref[...]
    li = li_ref[...]
    dt = jnp.exp(ldt_ref[...])
    mag = jnp.exp(lr * dt)
    ar = mag * jnp.cos(li * dt)
    ai = mag * jnp.sin(li * dt)
    den = lr * lr + li * li
    gr = ((ar - 1.0) * lr + ai * li) / den
    gi = (ai * lr - (ar - 1.0) * li) / den
    btr = btr_ref[...]
    bti = bti_ref[...]
    bbr = gr * btr - gi * bti
    bbi = gr * bti + gi * btr
    kf = lax.broadcasted_iota(jnp.int32, (L, 1), 0).astype(F32)
    pmag = jnp.exp(lr * dt * kf)
    pr = pmag * jnp.cos(li * dt * kf)
    pi = pmag * jnp.sin(li * dt * kf)
    cr = cr_ref[...]
    ci = ci_ref[...]
    pr3 = pr[:, None, :]
    pi3 = pi[:, None, :]
    wr = cr[None] * pr3 - ci[None] * pi3
    wi = cr[None] * pi3 + ci[None] * pr3
    dn = (((1,), (1,)), ((), ()))
    hp = lax.Precision.HIGHEST
    kt = (lax.dot_general(wr.reshape(L * S5_GROUP, S5_STATE), bbr, dn, precision=hp,
                          preferred_element_type=F32)
          - lax.dot_general(wi.reshape(L * S5_GROUP, S5_STATE), bbi, dn, precision=hp,
                            preferred_element_type=F32))
    row = lax.broadcasted_iota(jnp.int32, kt.shape, 0)
    col = lax.broadcasted_iota(jnp.int32, kt.shape, 1)
    kt_ref[...] = kt + jnp.where(row == col, d_ref[...], 0.0)
    er_ref[...] = pr3 * bbr[None] - pi3 * bbi[None]
    ei_ref[...] = pr3 * bbi[None] + pi3 * bbr[None]
    wr_ref[...] = wr * ar - wi * ai
    wi_ref[...] = wr * ai + wi * ar
    lastr = pr[L - 1:L, :]
    lasti = pi[L - 1:L, :]
    ac_ref[0:1, :] = lastr * ar - lasti * ai
    ac_ref[1:2, :] = lastr * ai + lasti * ar


def _s5_operators(lam_re, lam_im, log_dt, b_re, b_im, c_re, c_im, d_skip):
    G, P, H, L = S5_GROUPS, S5_STATE, S5_GROUP, S5_CHUNK
    per_g = lambda *shape: pl.BlockSpec((None,) + shape, lambda g: (g,) + (0,) * len(shape))
    kt, er, ei, wr, wi, ac = pl.pallas_call(
        _s5_param_kernel,
        grid=(G,),
        in_specs=[per_g(1, P), per_g(1, P), per_g(1, 1), per_g(H, P), per_g(H, P),
                  per_g(H, P), per_g(H, P), per_g(1, H)],
        out_specs=[per_g(L * H, H), per_g(L, H, P), per_g(L, H, P), per_g(L, H, P), per_g(L, H, P),
                   per_g(2, P)],
        out_shape=[jax.ShapeDtypeStruct((G, L * H, H), F32)]
        + [jax.ShapeDtypeStruct((G, L, H, P), F32)] * 4
        + [jax.ShapeDtypeStruct((G, 2, P), F32)],
        compiler_params=_cparams("parallel"),
        name="s5_operators",
    )(lam_re.reshape(G, 1, P), lam_im.reshape(G, 1, P), log_dt.reshape(G, 1, 1),
      b_re.transpose(0, 2, 1), b_im.transpose(0, 2, 1), c_re, c_im, d_skip.reshape(G, 1, H))

    ns, sg = N_SUPER, S5_SUPER
    eye = jnp.eye(sg, dtype=F32)
    ktg = kt.reshape(ns, sg, L, H, H)
    kd = jnp.einsum('SgtOI,gk->StgIkO', ktg, eye).reshape(ns, L, LANES, LANES)
    ecat = jnp.stack([er, ei], axis=3)[:, ::-1]
    eb = ecat.reshape(ns, sg, L, H, 2, P).transpose(0, 2, 1, 3, 4, 5).reshape(ns, L, LANES, LANES)
    wcat = jnp.stack([wr, -wi], axis=3).reshape(ns, sg, L, H, 2, P)
    wc = wcat.transpose(0, 2, 4, 5, 1, 3).reshape(ns, L, LANES, LANES)
    acr = ac[:, 0, :].reshape(ns, sg, P)
    aci = ac[:, 1, :].reshape(ns, sg, P)
    a_same = jnp.concatenate([acr, acr], axis=-1)
    a_swap = jnp.concatenate([-aci, aci], axis=-1)
    return kd.astype(BF16), eb.astype(BF16), wc.astype(BF16), a_same, a_swap


def _group_of(shape, axis):
    return lax.broadcasted_iota(jnp.int32, shape, axis) // S5_GROUP


def _s5_inc_kernel(u_ref, eb_ref, o_ref, b_ref):
    tr = u_ref.shape[0]

    @pl.when(pl.program_id(1) == 0)
    def _():
        row_grp = _group_of((LANES, LANES), 0)
        for l in range(S5_CHUNK):
            blk = eb_ref[l]
            for g in range(S5_SUPER):
                b_ref[l * LANES:(l + 1) * LANES, g * LANES:(g + 1) * LANES] = jnp.where(
                    row_grp == g, blk, jnp.zeros_like(blk))

    inc = jnp.dot(u_ref[...], b_ref[...], preferred_element_type=F32)
    for g in range(S5_SUPER):
        o_ref[pl.ds(g, tr, stride=S5_SUPER), :] = inc[:, g * LANES:(g + 1) * LANES]


def _s5_scan_kernel(inc_ref, asame_ref, aswap_ref, x_ref, state_ref, *, cb):
    @pl.when(pl.program_id(0) == 0)
    def _():
        state_ref[...] = jnp.zeros_like(state_ref)

    nseq = state_ref.shape[0]
    a_same = asame_ref[...].reshape(nseq * S5_SUPER, LANES)
    a_swap = aswap_ref[...].reshape(nseq * S5_SUPER, LANES)

    def body(c, x):
        x_ref[:, c] = x.reshape(nseq, S5_SUPER, LANES)
        inc = inc_ref[:, c].reshape(nseq * S5_SUPER, LANES)
        return x * a_same + pltpu.roll(x, LANES // 2, 1) * a_swap + inc

    x0 = state_ref[...].reshape(nseq * S5_SUPER, LANES)
    state_ref[...] = lax.fori_loop(0, cb, body, x0).reshape(nseq, S5_SUPER, LANES)


def _s5_out_kernel(u_ref, x_ref, kd_ref, wc_ref, y_ref, ys_ref, t_ref, c_ref):
    tr = u_ref.shape[0]

    @pl.when(pl.program_id(1) == 0)
    def _():
        zero = jnp.zeros((LANES, LANES), t_ref.dtype)
        col_grp = _group_of((LANES, LANES), 1)
        for l_out in range(S5_CHUNK):
            cols = slice(l_out * LANES, (l_out + 1) * LANES)
            for l_in in range(S5_CHUNK):
                t_ref[l_in * LANES:(l_in + 1) * LANES, cols] = kd_ref[l_out - l_in] if l_out >= l_in else zero
            blk = wc_ref[l_out]
            for g in range(S5_SUPER):
                c_ref[g * LANES:(g + 1) * LANES, cols] = jnp.where(col_grp == g, blk, zero)

    x = jnp.concatenate([x_ref[pl.ds(g, tr, stride=S5_SUPER), :] for g in range(S5_SUPER)], axis=-1)
    u = u_ref[...]
    pair = 2 * LANES
    y = jnp.concatenate(
        [jnp.dot(u[:, :(b + 1) * pair], t_ref[:(b + 1) * pair, b * pair:(b + 1) * pair],
                 preferred_element_type=F32) for b in range(S5_CHUNK // 2)], axis=-1)
    y += jnp.dot(x.astype(BF16), c_ref[...], preferred_element_type=F32)
    y = jax.nn.gelu(y)
    for l in range(S5_CHUNK):
        ys_ref[pl.ds(l, tr, stride=S5_CHUNK), :] = y[:, l * LANES:(l + 1) * LANES]
    y_ref[...] = ys_ref[...].astype(y_ref.dtype)


def _s5_core(uc, kd, eb, wc, a_same, a_swap, batch, seq, tr=512):
    ns, rows, width = uc.shape
    L = S5_CHUNK
    n = rows * L
    n_chunks = seq // L
    sw = 2 * S5_SUPER * S5_STATE
    tr = min(tr, rows)
    blocks_spec = pl.BlockSpec((None, L, LANES, LANES), lambda s, r: (s, 0, 0, 0))
    inc = pl.pallas_call(
        _s5_inc_kernel,
        grid=(ns, rows // tr),
        in_specs=[pl.BlockSpec((None, tr, width), lambda s, r: (s, r, 0)), blocks_spec],
        out_specs=pl.BlockSpec((None, tr * S5_SUPER, LANES), lambda s, r: (s, r, 0)),
        out_shape=jax.ShapeDtypeStruct((ns, rows * S5_SUPER, LANES), F32),
        scratch_shapes=[pltpu.VMEM((width, sw), BF16)],
        compiler_params=_cparams("parallel", "arbitrary"),
        name="s5_chunk_state",
    )(uc, eb)
    nseq = ns * batch
    cb = min(32, n_chunks)
    tile = (S5_SUPER, LANES)
    bcast = lambda a: jnp.broadcast_to(a[:, None], (ns, batch) + tile).reshape((nseq,) + tile)
    seq_spec = pl.BlockSpec((nseq, cb) + tile, lambda c: (0, c, 0, 0))
    tab_spec = pl.BlockSpec((nseq,) + tile, lambda c: (0, 0, 0))
    xprev = pl.pallas_call(
        functools.partial(_s5_scan_kernel, cb=cb),
        grid=(n_chunks // cb,),
        in_specs=[seq_spec, tab_spec, tab_spec],
        out_specs=seq_spec,
        out_shape=jax.ShapeDtypeStruct((nseq, n_chunks) + tile, F32),
        scratch_shapes=[pltpu.VMEM((nseq,) + tile, F32)],
        compiler_params=_cparams("arbitrary"),
        name="s5_chunk_scan",
    )(inc.reshape((nseq, n_chunks) + tile), bcast(a_same), bcast(a_swap))
    xprev = xprev.reshape(ns, rows * S5_SUPER, LANES)
    return pl.pallas_call(
        _s5_out_kernel,
        grid=(ns, rows // tr),
        in_specs=[pl.BlockSpec((None, tr, width), lambda s, r: (s, r, 0)),
                  pl.BlockSpec((None, tr * S5_SUPER, LANES), lambda s, r: (s, r, 0)),
                  blocks_spec, blocks_spec],
        out_specs=pl.BlockSpec((None, tr * L, LANES), lambda s, r: (s, r, 0)),
        out_shape=jax.ShapeDtypeStruct((ns, n, LANES), BF16),
        scratch_shapes=[pltpu.VMEM((tr * L, LANES), F32), pltpu.VMEM((width, width), BF16),
                        pltpu.VMEM((sw, width), BF16)],
        compiler_params=_cparams("parallel", "arbitrary"),
        name="s5_chunk_out",
    )(uc, xprev, kd, wc)


def _glu_router_kernel(y_ref, w_ref, res_ref, g_ref, wr_ref, tri_ref, h_ref, hn_ref, route_ref, cnt_ref,
                       count_ref):
    d = res_ref.shape[1]
    y = jnp.concatenate([y_ref[g] for g in range(N_SUPER)], axis=-1)
    z = jnp.dot(y, w_ref[...], preferred_element_type=F32)
    h = res_ref[...] + z[:, :d] * jax.nn.sigmoid(z[:, d:])
    h_ref[...] = h
    hn = _rms(h, g_ref[...])
    hn_ref[...] = hn.astype(hn_ref.dtype)
    hi = hn.astype(BF16)
    lo = (hn - hi.astype(F32)).astype(BF16)
    wr = wr_ref[...]
    first = jnp.dot(hi, wr, preferred_element_type=F32)
    logits = (first[:, :LANES] + first[:, LANES:]) + jnp.dot(lo, wr[:, :LANES], preferred_element_type=F32)
    lane = lax.broadcasted_iota(jnp.int32, logits.shape, 1)
    logits = jnp.where(lane < N_EXPERTS, logits, -jnp.inf)
    m1 = jnp.max(logits, axis=-1, keepdims=True)
    i1 = jnp.min(jnp.where(logits == m1, lane, LANES), axis=-1, keepdims=True)
    rest = jnp.where(lane == i1, -jnp.inf, logits)
    m2 = jnp.max(rest, axis=-1, keepdims=True)
    i2 = jnp.min(jnp.where(rest == m2, lane, LANES), axis=-1, keepdims=True)
    e2 = jnp.exp(m2 - m1)
    g1 = 1.0 / (1.0 + e2)

    @pl.when(pl.program_id(0) == 0)
    def _():
        count_ref[...] = jnp.zeros_like(count_ref)

    chosen = jnp.where(jnp.logical_or(lane == i1, lane == i2), 1.0, 0.0)
    before = jnp.dot(tri_ref[...], chosen.astype(BF16), preferred_element_type=F32) + count_ref[0:1, :]
    r1 = jnp.sum(jnp.where(lane == i1, before, 0.0), axis=-1, keepdims=True)
    r2 = jnp.sum(jnp.where(lane == i2, before, 0.0), axis=-1, keepdims=True)
    count_ref[0:1, :] += jnp.sum(chosen, axis=0, keepdims=True)
    cnt_ref[...] = count_ref[...]
    route_ref[...] = (jnp.where(lane == 0, i1.astype(F32), 0.0) + jnp.where(lane == 1, i2.astype(F32), 0.0)
                      + jnp.where(lane == 2, g1, 0.0) + jnp.where(lane == 3, e2 * g1, 0.0)
                      + jnp.where(lane == 4, r1, 0.0) + jnp.where(lane == 5, r2, 0.0))


def _glu_router(y, w_glu, res, g, w_router_pad, tm=512):
    ns, n, _ = y.shape
    d = res.shape[1]
    tri = jnp.asarray(np.tril(np.ones((tm, tm), np.float32), -1), BF16)
    return pl.pallas_call(
        _glu_router_kernel,
        grid=(n // tm,),
        in_specs=[pl.BlockSpec((ns, tm, LANES), lambda i: (0, i, 0)),
                  pl.BlockSpec(w_glu.shape, lambda i: (0, 0)),
                  pl.BlockSpec((tm, d), lambda i: (i, 0)),
                  pl.BlockSpec((1, d), lambda i: (0, 0)),
                  pl.BlockSpec(w_router_pad.shape, lambda i: (0, 0)),
                  pl.BlockSpec((tm, tm), lambda i: (0, 0))],
        out_specs=[pl.BlockSpec((tm, d), lambda i: (i, 0)),
                   pl.BlockSpec((tm, d), lambda i: (i, 0)),
                   pl.BlockSpec((tm, LANES), lambda i: (i, 0)),
                   pl.BlockSpec((8, LANES), lambda i: (0, 0))],
        out_shape=[jax.ShapeDtypeStruct((n, d), F32), jax.ShapeDtypeStruct((n, d), F32),
                   jax.ShapeDtypeStruct((n, LANES), F32), jax.ShapeDtypeStruct((8, LANES), F32)],
        scratch_shapes=[pltpu.VMEM((8, LANES), F32)],
        compiler_params=_cparams("arbitrary"),
        name="s5_glu_router",
    )(y, w_glu, res, g, w_router_pad, tri)


MOE_TM = 512
MOE_TF = 1792
DISPATCH_BLOCK = 512
COMBINE_TM = 512


def _moe_plan(route, counts, n_tok):
    ne, tm = N_EXPERTS, MOE_TM
    n_tiles = (2 * n_tok) // tm + ne
    eid = route[:, :2].astype(jnp.int32).reshape(-1)
    rank = route[:, 4:6].astype(jnp.int32).reshape(-1)
    onehot = (eid[:, None] == jnp.arange(ne, dtype=jnp.int32)[None, :]).astype(jnp.int32)
    cnt = counts[0, :ne].astype(jnp.int32)
    padded = ((cnt + tm - 1) // tm) * tm
    ends = jnp.cumsum(padded)
    off = ends - padded
    pos = (jnp.sum(off[None, :] * onehot, axis=1) + rank).astype(jnp.int32)
    tile_end = ends // tm
    n_valid = tile_end[-1:].astype(jnp.int32)
    tiles = jnp.arange(n_tiles, dtype=jnp.int32)
    tile_expert = jnp.minimum(jnp.sum(tiles[:, None] >= tile_end[None, :], axis=1), ne - 1).astype(jnp.int32)
    pad_start = (off + cnt).astype(jnp.int32)
    pad_len = (padded - cnt).astype(jnp.int32)
    pad_len = pad_len.at[ne - 1].add((n_tiles * tm - ends[-1]).astype(jnp.int32))
    return pos, tile_expert, n_valid, pad_start, pad_len, n_tiles


def _dispatch_kernel(pos_ref, pstart_ref, plen_ref, hn_ref, xs_ref, zero_ref, sem, *, n_tok):
    nb = n_tok // DISPATCH_BLOCK
    b = pl.program_id(0)

    def row_copy(r, k):
        dst = pos_ref[2 * (b * DISPATCH_BLOCK + r) + k]
        return pltpu.make_async_copy(hn_ref.at[pl.ds(r, 1)], xs_ref.at[pl.ds(dst, 1)], sem)

    for r in range(DISPATCH_BLOCK):
        row_copy(r, 0).start(priority=0)
        row_copy(r, 1).start(priority=1)

    for _ in range(2):
        pltpu.make_async_copy(hn_ref, xs_ref.at[pl.ds(0, DISPATCH_BLOCK)], sem).wait()

    @pl.when(b == nb - 1)
    def _():
        zero_ref[...] = jnp.zeros_like(zero_ref)

        def pad_copy(e, r):
            return pltpu.make_async_copy(zero_ref.at[pl.ds(0, 1)], xs_ref.at[pl.ds(pstart_ref[e] + r, 1)], sem)

        for e in range(N_EXPERTS):
            def pbody(r, _, e=e):
                pad_copy(e, r).start()
                return 0
            lax.fori_loop(0, plen_ref[e], pbody, 0)
        for e in range(N_EXPERTS):
            def wbody(r, _, e=e):
                pad_copy(e, r).wait()
                return 0
            lax.fori_loop(0, plen_ref[e], wbody, 0)


def _dispatch(hn, pos, pad_start, pad_len, n_rows):
    n_tok, d = hn.shape
    return pl.pallas_call(
        functools.partial(_dispatch_kernel, n_tok=n_tok),
        grid_spec=pltpu.PrefetchScalarGridSpec(
            num_scalar_prefetch=3,
            grid=(n_tok // DISPATCH_BLOCK,),
            in_specs=[pl.BlockSpec((DISPATCH_BLOCK, d), lambda b, pos, ps, plen: (b, 0))],
            out_specs=pl.BlockSpec(memory_space=pl.ANY),
            scratch_shapes=[pltpu.VMEM((8, d), F32), pltpu.SemaphoreType.DMA]),
        out_shape=jax.ShapeDtypeStruct((n_rows, d), F32),
        compiler_params=_cparams("arbitrary"),
        name="moe_dispatch",
    )(pos, pad_start, pad_len, hn)


def _moe_kernel(te_ref, nv_ref, x_ref, wg_ref, wu_ref, wd_ref, y_ref, xb_ref):
    i = pl.program_id(0)
    k = pl.program_id(1)

    @pl.when(i < nv_ref[0])
    def _():
        @pl.when(k == 0)
        def _():
            xb_ref[...] = x_ref[...].astype(BF16)

        hn = xb_ref[...]
        gate = jnp.dot(hn, wg_ref[...], preferred_element_type=F32)
        up = jnp.dot(hn, wu_ref[...], preferred_element_type=F32)
        act = (jax.nn.silu(gate) * up).astype(BF16)
        part = jnp.dot(act, wd_ref[...], preferred_element_type=F32)

        @pl.when(k == 0)
        def _():
            y_ref[...] = part

        @pl.when(k > 0)
        def _():
            y_ref[...] += part

    @pl.when(jnp.logical_and(i >= nv_ref[0], k == 0))
    def _():
        y_ref[...] = jnp.zeros_like(y_ref)


def _moe_experts(xs, tile_expert, n_valid, wg, wu, wd, n_tiles):
    tm, tf = MOE_TM, MOE_TF
    d = xs.shape[1]
    f = wg.shape[2]
    nk = f // tf

    def row_map(i, k, te, nv):
        return (jnp.minimum(i, nv[0] - 1), 0)

    def kk(i, k, nv):
        return jnp.where(i < nv[0], k, nk - 1)

    def w_in_map(i, k, te, nv):
        return (te[jnp.minimum(i, nv[0] - 1)], 0, kk(i, k, nv))

    def w_out_map(i, k, te, nv):
        return (te[jnp.minimum(i, nv[0] - 1)], kk(i, k, nv), 0)

    return pl.pallas_call(
        _moe_kernel,
        grid_spec=pltpu.PrefetchScalarGridSpec(
            num_scalar_prefetch=2,
            grid=(n_tiles, nk),
            in_specs=[pl.BlockSpec((tm, d), row_map),
                      pl.BlockSpec((None, d, tf), w_in_map),
                      pl.BlockSpec((None, d, tf), w_in_map),
                      pl.BlockSpec((None, tf, d), w_out_map)],
            out_specs=pl.BlockSpec((tm, d), lambda i, k, te, nv: (i, 0)),
            scratch_shapes=[pltpu.VMEM((tm, d), BF16)]),
        out_shape=jax.ShapeDtypeStruct(xs.shape, F32),
        compiler_params=_cparams("arbitrary", "arbitrary"),
        name="moe_experts",
    )(tile_expert, n_valid, xs, wg, wu, wd)


def _combine_kernel(pos_ref, y_ref, h_ref, route_ref, g_ref, out_ref, buf_ref, sem, *, tm, n_steps):
    i = pl.program_id(0)
    slot = i % 2

    def row_copy(step, sl, r, k):
        t = step * tm + r
        return pltpu.make_async_copy(y_ref.at[pl.ds(pos_ref[2 * t + k], 1)],
                                     buf_ref.at[sl, k, pl.ds(r, 1)], sem.at[sl])

    def issue(step, sl):
        for r in range(tm):
            row_copy(step, sl, r, 0).start(priority=0)
            row_copy(step, sl, r, 1).start(priority=1)

    @pl.when(i == 0)
    def _():
        issue(0, 0)

    @pl.when(i + 1 < n_steps)
    def _():
        issue(i + 1, 1 - slot)

    for k in range(2):
        pltpu.make_async_copy(y_ref.at[pl.ds(0, tm)], buf_ref.at[slot, k], sem.at[slot]).wait()

    route = route_ref[...]
    out = h_ref[...] + route[:, 2:3] * buf_ref[slot, 0] + route[:, 3:4] * buf_ref[slot, 1]
    out_ref[...] = _rms(out, g_ref[...])


def _combine(y, pos, h, route, g):
    n_tok, d = h.shape
    tm = COMBINE_TM
    n_steps = n_tok // tm
    return pl.pallas_call(
        functools.partial(_combine_kernel, tm=tm, n_steps=n_steps),
        grid_spec=pltpu.PrefetchScalarGridSpec(
            num_scalar_prefetch=1,
            grid=(n_steps,),
            in_specs=[pl.BlockSpec(memory_space=pl.ANY),
                      pl.BlockSpec((tm, d), lambda i, pos: (i, 0)),
                      pl.BlockSpec((tm, LANES), lambda i, pos: (i, 0)),
                      pl.BlockSpec((1, d), lambda i, pos: (0, 0))],
            out_specs=pl.BlockSpec((tm, d), lambda i, pos: (i, 0)),
            scratch_shapes=[pltpu.VMEM((2, 2, tm, d), F32), pltpu.SemaphoreType.DMA((2,))]),
        out_shape=jax.ShapeDtypeStruct((n_tok, d), F32),
        compiler_params=_cparams("arbitrary"),
        name="moe_combine",
    )(pos, y, h, route, g)


def _moe_layer(h, hn, route, counts, wg, wu, wd, g):
    n_tok = h.shape[0]
    pos, tile_expert, n_valid, pad_start, pad_len, n_tiles = _moe_plan(route, counts, n_tok)
    xs = _dispatch(hn, pos, pad_start, pad_len, n_tiles * MOE_TM)
    y = _moe_experts(xs, tile_expert, n_valid, wg, wu, wd, n_tiles)
    return _combine(y, pos, h, route, g)


def _mla_layer(h, positions, mix_g, w_in, q_norm, w_uq, kv_norm, w_ukv, w_o, ffn_g, w_gate, w_up, w_down,
               next_g):
    batch, seq = positions.shape
    win_ext, wq_ext, wkv_ext = _mla_weights(w_in, w_uq, w_ukv)
    pos = positions.astype(F32).reshape(batch * seq, 1)
    q, k, v = _mla_proj(h, pos, mix_g.reshape(1, -1), win_ext, q_norm.reshape(1, -1),
                        kv_norm.reshape(1, -1), wq_ext, wkv_ext)
    o = _attention(q, k, v, batch, seq)
    return _proj_ffn(o, h, w_o.astype(BF16), ffn_g.reshape(1, -1), w_gate.astype(BF16), w_up.astype(BF16),
                     w_down.astype(BF16), next_g.reshape(1, -1))


def _s5_layer(h, hn, batch, seq, w_in, lam_re, lam_im, log_dt, b_re, b_im, c_re, c_im, d_skip, w_glu,
              ffn_g, w_router):
    u = _mm_split(hn, w_in.astype(BF16))
    kd, eb, wc, a_same, a_swap = _s5_operators(lam_re, lam_im, log_dt, b_re, b_im, c_re, c_im, d_skip)
    y = _s5_core(u, kd, eb, wc, a_same, a_swap, batch, seq)
    wr_pad = jnp.concatenate([w_router, jnp.zeros((D_MODEL, LANES - N_EXPERTS), w_router.dtype)], axis=1)
    wr_hi = wr_pad.astype(BF16)
    wr_lo = (wr_pad - wr_hi.astype(F32)).astype(BF16)
    return _glu_router(y, w_glu.astype(BF16), h, ffn_g.reshape(1, -1), jnp.concatenate([wr_hi, wr_lo], axis=1))


def kernel(x, positions, mix_norm, ffn_norm, final_norm, mla_w_in, mla_q_norm, mla_w_uq, mla_kv_norm, mla_w_ukv, mla_w_o, ffn_w_gate, ffn_w_up, ffn_w_down, s5_w_in, s5_lambda_re, s5_lambda_im, s5_log_dt, s5_b_re, s5_b_im, s5_c_re, s5_c_im, s5_d, s5_w_glu, moe_w_router, moe_w_gate, moe_w_up, moe_w_down):
    batch, seq, d = x.shape
    h0 = x.reshape(batch * seq, d)
    h2, hn2 = _mla_layer(h0, positions, mix_norm[0], mla_w_in[0], mla_q_norm[0], mla_w_uq[0],
                         mla_kv_norm[0], mla_w_ukv[0], mla_w_o[0], ffn_norm[0],
                         ffn_w_gate[0], ffn_w_up[0], ffn_w_down[0], mix_norm[1])
    h3, hn3, route, counts = _s5_layer(h2, hn2, batch, seq, s5_w_in[0], s5_lambda_re[0], s5_lambda_im[0],
                                      s5_log_dt[0], s5_b_re[0], s5_b_im[0], s5_c_re[0], s5_c_im[0], s5_d[0],
                                      s5_w_glu[0], ffn_norm[1], moe_w_router[0])
    out = _moe_layer(h3, hn3, route, counts, moe_w_gate[0].astype(BF16), moe_w_up[0].astype(BF16),
                     moe_w_down[0].astype(BF16), final_norm.reshape(1, -1))
    return out.reshape(batch, seq, d)
```
